```python
import math
import functools
import jax
import jax.numpy as jnp
from jax import lax
import numpy as np

D_MODEL = 1024
BATCH = 8
SEQ = 4096
DEPTH = 2
DEC_BATCH = 32
DEC_SEQ = 8
PAST_LEN = 16384
PAGE_SIZE = 128

F32 = jnp.float32
HEAD_DIM = 64
W_A = D_MODEL // 4
H_A = W_A // HEAD_DIM
RANK_W = 32
RANK_A = 32
RANK_G = 64
PROJ_A = 3 * W_A + RANK_W + RANK_A + RANK_G
RWKV_SPLITS = (W_A, 2 * W_A, 3 * W_A, 3 * W_A + RANK_W, 3 * W_A + RANK_W + RANK_A)
W_B = D_MODEL // 4
H_B = W_B // HEAD_DIM
PROJ_B = 4 * W_B
HGRN_CHUNK = 64
W_C = D_MODEL // 2
H_C = W_C // HEAD_DIM
PROJ_C = 3 * W_C
SB_BLOCK = 128
SB_BIAS_INIT = -8.0
PROJ_IN = PROJ_A + PROJ_B + PROJ_C
N_MEM = 256
H_MEM = 4
DH_MEM = D_MODEL // H_MEM
D_FF = 2816
N_EXPERTS = 8
TOP_K = 2
D_FF_E = 3584
MOE_BLOCK = 128
N_DENSE = (DEPTH + 1) // 2
N_MOE = DEPTH // 2
ALPHA = (2 * DEPTH) ** 0.25
BETA = (8 * DEPTH) ** -0.25
LN_EPS = 1e-5
GN_EPS = 64e-5
RMS_EPS = 1e-6

kernel_name = 'hymba_rwkv7_hgrn2_stickbreak_deepnorm_step'


def layer_norm(x, g, b):
    xf = x.astype(F32)
    mu = jnp.mean(xf, -1, keepdims=True)
    var = jnp.mean(jnp.square(xf - mu), -1, keepdims=True)
    return ((xf - mu) * lax.rsqrt(var + LN_EPS) * g + b).astype(x.dtype)


def rwkv7_group(z, z_last, S0, p):
    B, L, _ = z.shape
    zf = z.astype(F32)
    prev = jnp.concatenate([z_last[:, None].astype(F32), zf[:, :-1]], axis=1)
    zs = zf + (prev - zf) * p['rwkv_mu']
    r, k, v, wd, ad, gd = jnp.split(zs, RWKV_SPLITS, axis=-1)
    log_decay = -jnp.exp(-jax.nn.softplus(-(p['rwkv_w0'] + jnp.tanh(wd) @ p['rwkv_w_up'])) - 0.5)
    a = jax.nn.sigmoid(p['rwkv_a0'] + ad @ p['rwkv_a_up'])
    g = jax.nn.sigmoid(gd) @ p['rwkv_g_up']
    heads = lambda t: t.reshape(B, L, H_A, HEAD_DIM)
    kk = heads(k * p['rwkv_k_k'])
    kk = kk / jnp.maximum(jnp.linalg.norm(kk, axis=-1, keepdims=True), 1e-12)
    k = heads(k * (1.0 + (a - 1.0) * p['rwkv_k_a']))
    r, v, w, a = heads(r), heads(v), heads(jnp.exp(log_decay)), heads(a)

    def step(S, inp):
        r_t, w_t, k_t, v_t, kk_t, a_t = inp
        S_kk = jnp.einsum('bhvk,bhk->bhv', S, kk_t)
        S = (S * w_t[:, :, None, :] - S_kk[..., None] * (kk_t * a_t)[:, :, None, :]
             + v_t[..., None] * k_t[:, :, None, :])
        return S, jnp.einsum('bhvk,bhk->bhv', S, r_t)

    xs = tuple(jnp.moveaxis(t, 1, 0) for t in (r, w, k, v, kk, a))
    S, o = lax.scan(step, S0.astype(F32), xs)
    o = jnp.moveaxis(o, 0, 1)
    mu = jnp.mean(o, -1, keepdims=True)
    var = jnp.mean(jnp.square(o - mu), -1, keepdims=True)
    o = ((o - mu) * lax.rsqrt(var + GN_EPS)).reshape(B, L, W_A) * p['rwkv_gn_g'] + p['rwkv_gn_b']
    bonus = jnp.sum(r * k * p['rwkv_r_k'], -1, keepdims=True) * v
    o = (o + bonus.reshape(B, L, W_A)) * g
    return o.astype(z.dtype), S.astype(S0.dtype), z[:, -1]


def hgrn2_chunked(q, k, v, logf, S0):
    B, L, H, _ = q.shape
    C = math.gcd(L, HGRN_CHUNK)
    N = L // C
    to_chunks = lambda t: t.reshape(B, N, C, H, t.shape[-1]).transpose(1, 0, 3, 2, 4)
    causal = jnp.tril(jnp.ones((C, C), bool))

    def step(S, inp):
        qc, kc, vc, lc = inp
        b = jnp.cumsum(lc, axis=2)
        o = jnp.einsum('bhtd,bhdv->bhtv', qc * jnp.exp(b), S)
        diff = b[:, :, :, None, :] - b[:, :, None, :, :]
        dec = jnp.exp(jnp.where(causal[:, :, None], diff, -jnp.inf))
        att = jnp.einsum('bhtd,bhsd,bhtsd->bhts', qc, kc, dec)
        o = o + jnp.einsum('bhts,bhsv->bhtv', att, vc)
        b_last = b[:, :, -1]
        S = (jnp.exp(b_last)[..., None] * S
             + jnp.einsum('bhsd,bhsv->bhdv', kc * jnp.exp(b_last[:, :, None] - b), vc))
        return S, o

    S, o = lax.scan(step, S0, tuple(to_chunks(t) for t in (q, k, v, logf)))
    return o.transpose(1, 0, 3, 2, 4).reshape(B, L, H, -1), S


def hgrn2_group(z, S0, lb, p):
    B, L, _ = z.shape
    q, fz, i, gate = jnp.split(z.astype(F32), 4, axis=-1)
    f = lb + (1.0 - lb) * jax.nn.sigmoid(fz)
    heads = lambda t: t.reshape(B, L, H_B, HEAD_DIM)
    o, S = hgrn2_chunked(heads(q), heads(1.0 - f), heads(i), heads(jnp.log(f)), S0.astype(F32))
    o = o * lax.rsqrt(jnp.mean(jnp.square(o), -1, keepdims=True) + RMS_EPS)
    o = o.reshape(B, L, W_B) * p['hgrn_norm_g'] * jax.nn.silu(gate)
    return o.astype(z.dtype), S.astype(S0.dtype)


def stick_breaking(q, k, v, bias, q_pos, k_pos):
    z = (jnp.einsum('bqhd,bkhd->bhqk', q, k).astype(F32) * (HEAD_DIM ** -0.5)
         + bias.astype(F32)[None, :, None, None])
    mask = k_pos[None, :] < q_pos[:, None]
    log_not = jnp.where(mask, jax.nn.log_sigmoid(-z), 0.0)
    suffix = lax.cumsum(log_not, axis=3, reverse=True) - log_not
    A = jnp.where(mask, jnp.exp(jax.nn.log_sigmoid(z) + suffix), 0.0)
    return jnp.einsum('bhqk,bkhd->bqhd', A.astype(v.dtype), v)


def sb_prompt(q, k, v, bias):
    B, L, H, d = q.shape
    nb = L // SB_BLOCK
    qb = q.reshape(B, nb, SB_BLOCK, H, d).transpose(1, 0, 2, 3, 4)
    k_pos = jnp.arange(L)

    def block(args):
        qi, i = args
        return stick_breaking(qi, k, v, bias, i * SB_BLOCK + jnp.arange(SB_BLOCK), k_pos)

    o = lax.map(block, (qb, jnp.arange(nb)))
    return o.transpose(1, 0, 2, 3, 4).reshape(B, L, H, d)


def mixer_block(x, p, lb, rwkv_S0, shift0, hgrn_S0, sb_past_k, sb_past_v):
    B, L, _ = x.shape
    z = x @ p['w_in']
    zA, zB, zC = jnp.split(z, (PROJ_A, PROJ_A + PROJ_B), axis=-1)
    oA, rwkv_S, shift = rwkv7_group(zA, shift0, rwkv_S0, p)
    oB, hgrn_S = hgrn2_group(zB, hgrn_S0, lb, p)
    q, k, v = (t.reshape(B, L, H_C, HEAD_DIM) for t in jnp.split(zC, 3, axis=-1))
    if sb_past_k is None:
        oC = sb_prompt(q, k, v, p['sb_bias'])
    else:
        past = sb_past_k.shape[1]
        kf = jnp.concatenate([sb_past_k.astype(k.dtype), k], axis=1)
        vf = jnp.concatenate([sb_past_v.astype(v.dtype), v], axis=1)
        oC = stick_breaking(q, kf, vf, p['sb_bias'], past + jnp.arange(L), jnp.arange(past + L))
    o = jnp.concatenate([oA, oB, oC.reshape(B, L, W_C)], axis=-1) @ p['w_out']
    return o, (rwkv_S, shift, hgrn_S, k, v)


def mem_attend(x, mk, mv, wq, wo):
    B, L, _ = x.shape
    q = (x @ wq).reshape(B, L, H_MEM, DH_MEM)
    s = jnp.einsum('blhd,bmhd->bhlm', q, mk.astype(q.dtype)).astype(F32) * (DH_MEM ** -0.5)
    a = jax.nn.softmax(s, axis=-1).astype(x.dtype)
    return jnp.einsum('bhlm,bmhd->blhd', a, mv.astype(x.dtype)).reshape(B, L, D_MODEL) @ wo


def swiglu(x, wg, wu, wd):
    return (jax.nn.silu(x @ wg) * (x @ wu)) @ wd


def moe_swiglu(x, router, wg, wu, wd):
    B, L, D = x.shape
    xt = x.reshape(-1, D)
    T = xt.shape[0]
    logits = (xt @ router).astype(F32)
    top_val, top_idx = lax.top_k(logits, TOP_K)
    gates = jax.nn.softmax(top_val, axis=-1)
    e_flat = top_idx.reshape(-1)
    tok_flat = jnp.repeat(jnp.arange(T), TOP_K)
    order = jnp.argsort(e_flat)
    e_s, tok_s, g_s = e_flat[order], tok_flat[order], gates.reshape(-1)[order]
    counts = jnp.bincount(e_flat, length=N_EXPERTS)
    padded = (counts + MOE_BLOCK - 1) // MOE_BLOCK * MOE_BLOCK
    start = jnp.cumsum(counts) - counts
    pstart = jnp.cumsum(padded) - padded
    dest = pstart[e_s] + jnp.arange(T * TOP_K) - start[e_s]
    n_blocks = -(-(T * TOP_K) // MOE_BLOCK) + N_EXPERTS
    row_tok = jnp.zeros((n_blocks * MOE_BLOCK,), jnp.int32).at[dest].set(tok_s)
    block_e = jnp.minimum(jnp.searchsorted(jnp.cumsum(padded), jnp.arange(n_blocks) * MOE_BLOCK, side='right'), N_EXPERTS - 1)
    xb = xt[row_tok].reshape(n_blocks, MOE_BLOCK, D)

    def expert_block(args):
        xi, e = args
        return swiglu(xi, wg[e], wu[e], wd[e])

    yb = lax.map(expert_block, (xb, block_e)).reshape(-1, D)
    y = jnp.zeros_like(xt).at[tok_s].add(yb[dest] * g_s[:, None].astype(x.dtype))
    return y.reshape(B, L, D)


def layer_forward(x, p, lb, states, sb_past_k, sb_past_v, mem_k, mem_v, ffn):
    h, new = mixer_block(x, p, lb, states[0], states[1], states[2], sb_past_k, sb_past_v)
    x = layer_norm(ALPHA * x + h, p['ln_mix_g'], p['ln_mix_b'])
    x = layer_norm(ALPHA * x + mem_attend(x, mem_k, mem_v, p['mem_wq'], p['mem_wo']), p['ln_mem_g'], p['ln_mem_b'])
    x = layer_norm(ALPHA * x + ffn(x), p['ln_ffn_g'], p['ln_ffn_b'])
    return x, new


def setup_inputs(seed: int = 0) -> dict:
    key = jax.random.key(seed)
    ks = iter(jax.random.split(key, 64))
    nrm = lambda shape, scale: scale * jax.random.normal(next(ks), shape, F32)
    gain = lambda shape: 1.0 + nrm(shape, 0.02)
    n_pages = PAST_LEN // PAGE_SIZE
    n_used = DEC_BATCH * n_pages
    n_pool = n_used + (n_used + 3) // 4
    page_table = jax.random.permutation(next(ks), n_pool)[:n_used].reshape(DEC_BATCH, n_pages).astype(jnp.int32)
    d_s = D_MODEL ** -0.5
    return {
        'x_prompt': nrm((BATCH, SEQ, D_MODEL), 1.0),
        'x_sample': nrm((DEC_BATCH, DEC_SEQ, D_MODEL), 1.0),
        'cache_sb_k': nrm((DEPTH, n_pool, PAGE_SIZE, H_C, HEAD_DIM), 1.0),
        'cache_sb_v': nrm((DEPTH, n_pool, PAGE_SIZE, H_C, HEAD_DIM), 1.0),
        'state_rwkv': nrm((DEPTH, DEC_BATCH, H_A, HEAD_DIM, HEAD_DIM), 0.5),
        'state_rwkv_shift': nrm((DEPTH, DEC_BATCH, PROJ_A), 1.0),
        'state_hgrn': nrm((DEPTH, DEC_BATCH, H_B, HEAD_DIM, HEAD_DIM), 0.5),
        'cache_mem_k': nrm((DEPTH, DEC_BATCH, N_MEM, H_MEM, DH_MEM), 1.0),
        'cache_mem_v': nrm((DEPTH, DEC_BATCH, N_MEM, H_MEM, DH_MEM), 1.0),
        'page_table': page_table,
        'mem_prompt': nrm((BATCH, N_MEM, D_MODEL), 1.0),
        'ln_in_g': gain((D_MODEL,)),
        'ln_in_b': nrm((D_MODEL,), 0.02),
        'w_in': nrm((DEPTH, D_MODEL, PROJ_IN), d_s),
        'rwkv_mu': jax.random.uniform(next(ks), (DEPTH, PROJ_A), F32),
        'rwkv_w0': nrm((DEPTH, W_A), 1.0) - 1.5,
        'rwkv_w_up': nrm((DEPTH, RANK_W, W_A), 0.5 * RANK_W ** -0.5),
        'rwkv_a0': nrm((DEPTH, W_A), 0.1),
        'rwkv_a_up': nrm((DEPTH, RANK_A, W_A), 0.5 * RANK_A ** -0.5),
        'rwkv_g_up': nrm((DEPTH, RANK_G, W_A), RANK_G ** -0.5),
        'rwkv_k_k': gain((DEPTH, W_A)),
        'rwkv_k_a': gain((DEPTH, W_A)),
        'rwkv_r_k': nrm((DEPTH, H_A, HEAD_DIM), 0.1),
        'rwkv_gn_g': gain((DEPTH, W_A)),
        'rwkv_gn_b': nrm((DEPTH, W_A), 0.02),
        'hgrn_lb': nrm((DEPTH, W_B), 0.5),
        'hgrn_norm_g': gain((DEPTH, W_B)),
        'sb_bias': SB_BIAS_INIT + nrm((DEPTH, H_C), 0.1),
        'w_out': nrm((DEPTH, D_MODEL, D_MODEL), d_s * BETA),
        'ln_mix_g': gain((DEPTH, D_MODEL)),
        'ln_mix_b': nrm((DEPTH, D_MODEL), 0.02),
        'mem_wq': nrm((DEPTH, D_MODEL, D_MODEL), d_s),
        'mem_wk': nrm((DEPTH, D_MODEL, D_MODEL), d_s),
        'mem_wv': nrm((DEPTH, D_MODEL, D_MODEL), d_s),
        'mem_wo': nrm((DEPTH, D_MODEL, D_MODEL), d_s * BETA),
        'ln_mem_g': gain((DEPTH, D_MODEL)),
        'ln_mem_b': nrm((DEPTH, D_MODEL), 0.02),
        'ffn_w_gate': nrm((N_DENSE, D_MODEL, D_FF), d_s),
        'ffn_w_up': nrm((N_DENSE, D_MODEL, D_FF), d_s),
        'ffn_w_down': nrm((N_DENSE, D_FF, D_MODEL), D_FF ** -0.5 * BETA),
        'moe_router': nrm((N_MOE, D_MODEL, N_EXPERTS), d_s),
        'moe_w_gate': nrm((N_MOE, N_EXPERTS, D_MODEL, D_FF_E), d_s),
        'moe_w_up': nrm((N_MOE, N_EXPERTS, D_MODEL, D_FF_E), d_s),
        'moe_w_down': nrm((N_MOE, N_EXPERTS, D_FF_E, D_MODEL), D_FF_E ** -0.5 * BETA),
        'ln_ffn_g': gain((DEPTH, D_MODEL)),
        'ln_ffn_b': nrm((DEPTH, D_MODEL), 0.02),
    }


def reference(x_prompt, x_sample, cache_sb_k, cache_sb_v, state_rwkv, state_rwkv_shift, state_hgrn,
              cache_mem_k, cache_mem_v, page_table, mem_prompt, ln_in_g, ln_in_b, w_in, rwkv_mu, rwkv_w0,
              rwkv_w_up, rwkv_a0, rwkv_a_up, rwkv_g_up, rwkv_k_k, rwkv_k_a, rwkv_r_k, rwkv_gn_g, rwkv_gn_b,
              hgrn_lb, hgrn_norm_g, sb_bias, w_out, ln_mix_g, ln_mix_b, mem_wq, mem_wk, mem_wv, mem_wo, ln_mem_g,
              ln_mem_b, ffn_w_gate, ffn_w_up, ffn_w_down, moe_router, moe_w_gate, moe_w_up, moe_w_down,
              ln_ffn_g, ln_ffn_b):
    B = x_prompt.shape[0]
    Bs = x_sample.shape[0]
    past_len = page_table.shape[1] * PAGE_SIZE
    lb_sm = jax.nn.softmax(hgrn_lb.astype(F32), axis=0)
    lb_all = jnp.cumsum(lb_sm, axis=0) - lb_sm[0]
    xp = layer_norm(x_prompt, ln_in_g, ln_in_b)
    xs = layer_norm(x_sample, ln_in_g, ln_in_b)
    zero_states = (jnp.zeros((B, H_A, HEAD_DIM, HEAD_DIM), x_prompt.dtype),
                   jnp.zeros((B, PROJ_A), x_prompt.dtype),
                   jnp.zeros((B, H_B, HEAD_DIM, HEAD_DIM), x_prompt.dtype))
    sbk_p, sbv_p, rw_p, sh_p, hg_p, mk_p, mv_p = [], [], [], [], [], [], []
    sbk_s, sbv_s, rw_s, sh_s, hg_s = [], [], [], [], []
    for l in range(DEPTH):
        p = dict(w_in=w_in[l], w_out=w_out[l], rwkv_mu=rwkv_mu[l], rwkv_w0=rwkv_w0[l], rwkv_w_up=rwkv_w_up[l],
                 rwkv_a0=rwkv_a0[l], rwkv_a_up=rwkv_a_up[l], rwkv_g_up=rwkv_g_up[l], rwkv_k_k=rwkv_k_k[l],
                 rwkv_k_a=rwkv_k_a[l], rwkv_r_k=rwkv_r_k[l], rwkv_gn_g=rwkv_gn_g[l], rwkv_gn_b=rwkv_gn_b[l],
                 hgrn_norm_g=hgrn_norm_g[l], sb_bias=sb_bias[l], ln_mix_g=ln_mix_g[l], ln_mix_b=ln_mix_b[l],
                 mem_wq=mem_wq[l], mem_wo=mem_wo[l], ln_mem_g=ln_mem_g[l], ln_mem_b=ln_mem_b[l],
                 ln_ffn_g=ln_ffn_g[l], ln_ffn_b=ln_ffn_b[l])
        j = l // 2
        if l % 2 == 0:
            ffn = functools.partial(swiglu, wg=ffn_w_gate[j], wu=ffn_w_up[j], wd=ffn_w_down[j])
        else:
            ffn = functools.partial(moe_swiglu, router=moe_router[j], wg=moe_w_gate[j], wu=moe_w_up[j], wd=moe_w_down[j])
        mk = (mem_prompt @ mem_wk[l]).reshape(B, N_MEM, H_MEM, DH_MEM)
        mv = (mem_prompt @ mem_wv[l]).reshape(B, N_MEM, H_MEM, DH_MEM)
        xp, (S_a, sh, S_b, k_new, v_new) = layer_forward(xp, p, lb_all[l], zero_states, None, None, mk, mv, ffn)
        sbk_p.append(k_new); sbv_p.append(v_new); rw_p.append(S_a); sh_p.append(sh); hg_p.append(S_b)
        mk_p.append(mk); mv_p.append(mv)
        past_k = cache_sb_k[l][page_table].reshape(Bs, past_len, H_C, HEAD_DIM)
        past_v = cache_sb_v[l][page_table].reshape(Bs, past_len, H_C, HEAD_DIM)
        states = (state_rwkv[l], state_rwkv_shift[l], state_hgrn[l])
        xs, (S_a, sh, S_b, k_new, v_new) = layer_forward(xs, p, lb_all[l], states, past_k, past_v,
                                                         cache_mem_k[l], cache_mem_v[l], ffn)
        sbk_s.append(k_new); sbv_s.append(v_new); rw_s.append(S_a); sh_s.append(sh); hg_s.append(S_b)
    y_prompt = xp
    y_sample = xs
    return (y_prompt, y_sample,
            jnp.stack(sbk_p), jnp.stack(sbv_p), jnp.stack(rw_p), jnp.stack(sh_p), jnp.stack(hg_p),
            jnp.stack(mk_p), jnp.stack(mv_p),
            jnp.stack(sbk_s), jnp.stack(sbv_s), jnp.stack(rw_s), jnp.stack(sh_s), jnp.stack(hg_s))
```

```python
import functools
import math

import jax
import jax.numpy as jnp
from jax import lax
from jax.experimental import pallas as pl
from jax.experimental.pallas import tpu as pltpu

F32 = jnp.float32
BF16 = jnp.bfloat16

HEAD_DIM = 64
LANES = 128
PAGE_SIZE = 128
N_MEM_HEADS = 4
TOP_K = 2
LN_EPS = 1e-5
GN_EPS = 64e-5
RMS_EPS = 1e-6
VMEM_LIMIT = 56 * 1024 * 1024


def _cparams(sem):
    return pltpu.CompilerParams(dimension_semantics=sem, vmem_limit_bytes=VMEM_LIMIT)


def _const_spec(shape):
    nd = len(shape)
    return pl.BlockSpec(shape, lambda *_: (0,) * nd, pipeline_mode=pl.Buffered(1))


def _bdot(a, b):
    return jnp.dot(a.astype(BF16), b.astype(BF16), preferred_element_type=F32)


def _bdot_t(a, b):
    return lax.dot_general(a.astype(BF16), b.astype(BF16), (((1,), (1,)), ((), ())),
                           preferred_element_type=F32)


def _split2(x):
    hi = x.astype(BF16)
    lo = (x - hi.astype(F32)).astype(BF16)
    return hi, lo


def _dot2(x, w_bf16):
    hi, lo = _split2(x)
    return (jnp.dot(hi, w_bf16, preferred_element_type=F32)
            + jnp.dot(lo, w_bf16, preferred_element_type=F32))


def _layer_norm(x, g, b):
    mu = jnp.mean(x, -1, keepdims=True)
    xc = x - mu
    var = jnp.mean(xc * xc, -1, keepdims=True)
    return xc * lax.rsqrt(var + LN_EPS) * g + b


def _sigmoid(x):
    return 1.0 / (1.0 + jnp.exp(-x))


def _silu(x):
    return x * _sigmoid(x)


def _softplus(x):
    return jnp.maximum(x, 0.0) + jnp.log(1.0 + jnp.exp(-jnp.abs(x)))


def _head_block_ones(width):
    r = lax.broadcasted_iota(jnp.int32, (width, width), 0) // HEAD_DIM
    c = lax.broadcasted_iota(jnp.int32, (width, width), 1) // HEAD_DIM
    return jnp.where(r == c, 1.0, 0.0).astype(BF16)


def _in_proj_kernel(x_ref, g_ref, b_ref, w_ref, xn_ref, *z_refs, apply_ln):
    x = x_ref[...]
    if apply_ln:
        x = _layer_norm(x, g_ref[...], b_ref[...])
    xn_ref[...] = x
    z = _bdot(x, w_ref[...])
    c0 = 0
    for z_ref in z_refs:
        z_ref[...] = z[:, c0:c0 + z_ref.shape[1]]
        c0 += z_ref.shape[1]


def in_proj(x, g, b, w_bf16, *, apply_ln, widths, tm=512):
    T, D = x.shape
    N = w_bf16.shape[1]
    assert sum(widths) == N and all(wd % LANES == 0 for wd in widths)
    tm = min(tm, T)
    row = lambda width: pl.BlockSpec((tm, width), lambda i: (i, 0))
    return pl.pallas_call(
        functools.partial(_in_proj_kernel, apply_ln=apply_ln),
        grid=(T // tm,),
        in_specs=[row(D), _const_spec((1, D)), _const_spec((1, D)), _const_spec((D, N))],
        out_specs=[row(D)] + [row(wd) for wd in widths],
        out_shape=[jax.ShapeDtypeStruct((T, D), F32)]
                  + [jax.ShapeDtypeStruct((T, wd), F32) for wd in widths],
        compiler_params=_cparams(("parallel",)),
    )(x, g.reshape(1, D), b.reshape(1, D), w_bf16)


def _matmul_kernel(x_ref, w_ref, o_ref):
    o_ref[...] = _bdot(x_ref[...], w_ref[...])


def matmul(x, w_bf16, tm=512):
    T, K = x.shape
    N = w_bf16.shape[1]
    tm = min(tm, T)
    return pl.pallas_call(
        _matmul_kernel,
        grid=(T // tm,),
        in_specs=[pl.BlockSpec((tm, K), lambda i: (i, 0)), _const_spec((K, N))],
        out_specs=pl.BlockSpec((tm, N), lambda i: (i, 0)),
        out_shape=jax.ShapeDtypeStruct((T, N), F32),
        compiler_params=_cparams(("parallel",)),
    )(x, w_bf16)


def _mix_out_kernel(oa_ref, ob_ref, oc_ref, wa_ref, wb_ref, wc_ref, res_ref, g_ref, b_ref, o_ref,
                    *, alpha):
    h = (_bdot(oa_ref[...], wa_ref[...]) + _bdot(ob_ref[...], wb_ref[...])
         + _bdot(oc_ref[...], wc_ref[...]))
    o_ref[...] = _layer_norm(alpha * res_ref[...] + h, g_ref[...], b_ref[...])


def mix_out(oa, ob, oc, w_out_bf16, res, g, b, *, alpha, tm=512):
    T, D = res.shape
    wa, wb, wc = oa.shape[1], ob.shape[1], oc.shape[1]
    tm = min(tm, T)
    row = lambda width: pl.BlockSpec((tm, width), lambda i: (i, 0))
    return pl.pallas_call(
        functools.partial(_mix_out_kernel, alpha=alpha),
        grid=(T // tm,),
        in_specs=[row(wa), row(wb), row(wc),
                  _const_spec((wa, D)), _const_spec((wb, D)), _const_spec((wc, D)),
                  row(D), _const_spec((1, D)), _const_spec((1, D))],
        out_specs=row(D),
        out_shape=jax.ShapeDtypeStruct((T, D), F32),
        compiler_params=_cparams(("parallel",)),
    )(oa, ob, oc, w_out_bf16[:wa], w_out_bf16[wa:wa + wb], w_out_bf16[wa + wb:], res,
      g.reshape(1, D), b.reshape(1, D))


def _res_ln_kernel(x_ref, y_ref, g_ref, b_ref, o_ref, *, alpha):
    o_ref[...] = _layer_norm(alpha * x_ref[...] + y_ref[...], g_ref[...], b_ref[...])


def res_ln(x, y, g, b, *, alpha, tm=1024):
    T, D = x.shape
    tm = min(tm, T)
    row = pl.BlockSpec((tm, D), lambda i: (i, 0))
    return pl.pallas_call(
        functools.partial(_res_ln_kernel, alpha=alpha),
        grid=(T // tm,),
        in_specs=[row, row, _const_spec((1, D)), _const_spec((1, D))],
        out_specs=row,
        out_shape=jax.ShapeDtypeStruct((T, D), F32),
        compiler_params=_cparams(("parallel",)),
    )(x, y, g.reshape(1, D), b.reshape(1, D))


def _mem_kernel(x_ref, wq_ref, mk_ref, mv_ref, wo_ref, g_ref, b_ref, o_ref, att_ref,
                *, alpha, nb, rows):
    x = x_ref[...]
    q = _bdot(x, wq_ref[...])
    D = q.shape[1]
    dh = D // N_MEM_HEADS
    scale = dh ** -0.5
    for bi in range(nb):
        r0 = bi * rows
        for h in range(N_MEM_HEADS):
            c0 = h * dh
            qh = q[r0:r0 + rows, c0:c0 + dh]
            kh = mk_ref[bi, :, c0:c0 + dh]
            vh = mv_ref[bi, :, c0:c0 + dh]
            s = _bdot_t(qh, kh) * scale
            s = s - jnp.max(s, -1, keepdims=True)
            p = jnp.exp(s)
            p = p / jnp.sum(p, -1, keepdims=True)
            att_ref[r0:r0 + rows, c0:c0 + dh] = _bdot(p, vh)
    y = _bdot(att_ref[...], wo_ref[...])
    o_ref[...] = _layer_norm(alpha * x + y, g_ref[...], b_ref[...])


def mem_block(x, wq_bf16, mk, mv, wo_bf16, g, b, *, alpha, seq_len, tm=512):
    T, D = x.shape
    B, n_mem, _ = mk.shape
    if seq_len >= tm:
        nb, rows = 1, tm
        mem_map = lambda i: (i // (seq_len // tm), 0, 0)
    else:
        nb, rows = min(B, 8), seq_len
        tm = nb * rows
        mem_map = lambda i: (i, 0, 0)
    row = pl.BlockSpec((tm, D), lambda i: (i, 0))
    return pl.pallas_call(
        functools.partial(_mem_kernel, alpha=alpha, nb=nb, rows=rows),
        grid=(T // tm,),
        in_specs=[row, _const_spec((D, D)),
                  pl.BlockSpec((nb, n_mem, D), mem_map), pl.BlockSpec((nb, n_mem, D), mem_map),
                  _const_spec((D, D)), _const_spec((1, D)), _const_spec((1, D))],
        out_specs=row,
        out_shape=jax.ShapeDtypeStruct((T, D), F32),
        scratch_shapes=[pltpu.VMEM((tm, D), F32)],
        compiler_params=_cparams(("parallel",)),
    )(x, wq_bf16, mk, mv, wo_bf16, g.reshape(1, D), b.reshape(1, D))


def _ffn_kernel(x_ref, wg_ref, wu_ref, wd_ref, g_ref, b_ref, o_ref, *, alpha, tf):
    x = x_ref[...]
    xb = x.astype(BF16)
    F = wg_ref.shape[1]
    y = jnp.zeros(x.shape, F32)
    for f0 in range(0, F, tf):
        gate = jnp.dot(xb, wg_ref[:, f0:f0 + tf], preferred_element_type=F32)
        up = jnp.dot(xb, wu_ref[:, f0:f0 + tf], preferred_element_type=F32)
        y = y + _bdot(_silu(gate) * up, wd_ref[f0:f0 + tf, :])
    o_ref[...] = _layer_norm(alpha * x + y, g_ref[...], b_ref[...])


def ffn_block(x, wg_bf16, wu_bf16, wd_bf16, g, b, *, alpha, tm=512, tf=256):
    T, D = x.shape
    F = wg_bf16.shape[1]
    tm = min(tm, T)
    row = pl.BlockSpec((tm, D), lambda i: (i, 0))
    return pl.pallas_call(
        functools.partial(_ffn_kernel, alpha=alpha, tf=tf),
        grid=(T // tm,),
        in_specs=[row, _const_spec((D, F)), _const_spec((D, F)), _const_spec((F, D)),
                  _const_spec((1, D)), _const_spec((1, D))],
        out_specs=row,
        out_shape=jax.ShapeDtypeStruct((T, D), F32),
        compiler_params=_cparams(("parallel",)),
    )(x, wg_bf16, wu_bf16, wd_bf16, g.reshape(1, D), b.reshape(1, D))


def _sb_prompt_kernel(bias_ref, q_ref, k_ref, v_ref, o_ref, *, tq, tk, scale):
    hp = pl.program_id(1)
    qi = pl.program_id(2)
    q = q_ref[0] * scale
    half = lax.broadcasted_iota(jnp.int32, (tq, LANES), 1) // HEAD_DIM
    qm = [jnp.where(half == h2, q, 0.0).astype(BF16) for h2 in range(2)]
    bias = [bias_ref[2 * hp + h2] for h2 in range(2)]
    rr = lax.broadcasted_iota(jnp.int32, (2 * tk, 2 * tk), 0) % tk
    cc = lax.broadcasted_iota(jnp.int32, (2 * tk, 2 * tk), 1)
    mw = jnp.where((cc >= tk) | (rr > cc), 1.0, 0.0).astype(BF16)
    nd = tq // tk
    q_pos = qi * tq + lax.broadcasted_iota(jnp.int32, (tq, tk), 0)
    k_off = lax.broadcasted_iota(jnp.int32, (tq, tk), 1)

    def block(j, carry, masked):
        start = pl.multiple_of(j * tk, tk)
        kb = k_ref[0, pl.ds(start, tk), :].astype(BF16)
        vb = v_ref[0, pl.ds(start, tk), :].astype(BF16)
        out = []
        for h2 in range(2):
            acc, c = carry[2 * h2], carry[2 * h2 + 1]
            z = _bdot_t(qm[h2], kb) + bias[h2]
            sp = _softplus(z)
            if masked:
                m = (j * tk + k_off) < q_pos
                spm = jnp.where(m, sp, 0.0)
            else:
                spm = sp
            hi, lo = _split2(spm)
            r = jnp.dot(jnp.concatenate([hi, lo], axis=1), mw, preferred_element_type=F32)
            a = jnp.exp(z - sp - r[:, :tk] - c)
            if masked:
                a = jnp.where(m, a, 0.0)
            out.append(acc + jnp.dot(a.astype(BF16), vb, preferred_element_type=F32))
            out.append(c + r[:, tk:])
        return tuple(out)

    carry = (jnp.zeros((tq, LANES), F32), jnp.zeros((tq, tk), F32)) * 2
    for d in range(nd):
        carry = block(qi * nd + (nd - 1 - d), carry, True)
    n_full = qi * nd
    carry = lax.fori_loop(0, n_full, lambda i, cr: block(n_full - 1 - i, cr, False), carry)
    o_ref[0] = jnp.where(half == 0, carry[0], carry[2])


def sb_prompt(q, k, v, bias, *, n_heads, tq=256, tk=128):
    B, L, _ = q.shape
    tq = min(tq, L)
    tk = min(tk, tq)
    grid_spec = pltpu.PrefetchScalarGridSpec(
        num_scalar_prefetch=1,
        grid=(B, n_heads // 2, L // tq),
        in_specs=[pl.BlockSpec((1, tq, LANES), lambda b, hp, qi, bias: (b, qi, hp)),
                  pl.BlockSpec((1, L, LANES), lambda b, hp, qi, bias: (b, 0, hp)),
                  pl.BlockSpec((1, L, LANES), lambda b, hp, qi, bias: (b, 0, hp))],
        out_specs=pl.BlockSpec((1, tq, LANES), lambda b, hp, qi, bias: (b, qi, hp)),
    )
    return pl.pallas_call(
        functools.partial(_sb_prompt_kernel, tq=tq, tk=tk, scale=HEAD_DIM ** -0.5),
        grid_spec=grid_spec,
        out_shape=jax.ShapeDtypeStruct((B, L, n_heads * HEAD_DIM), F32),
        compiler_params=_cparams(("parallel", "parallel", "arbitrary")),
    )(bias.astype(F32), q, k, v)


def _sb_sample_kernel(pt_ref, lay_ref, qbd_ref, bias_ref, kn_ref, vn_ref, *rest,
                      pp, n_heads, n_q, scale):
    k_refs, v_refs = rest[:pp], rest[pp:2 * pp]
    o_ref, acc_ref, c_ref = rest[2 * pp:]
    j = pl.program_id(1)
    qbd = qbd_ref[0]
    bias = bias_ref[...]

    def process(kp, vp, mask, rows):
        zt = jnp.dot(kp.astype(BF16), qbd, preferred_element_type=F32) * scale + bias
        sp = _softplus(zt)
        spm = sp if mask is None else jnp.where(mask, sp, 0.0)
        hi, lo = _split2(spm)
        rr = lax.broadcasted_iota(jnp.int32, (rows, 2 * rows), 0)
        cc = lax.broadcasted_iota(jnp.int32, (rows, 2 * rows), 1) % rows
        later = jnp.where(cc > rr, 1.0, 0.0).astype(BF16)
        r = jnp.dot(later, jnp.concatenate([hi, lo], axis=0), preferred_element_type=F32)
        c = c_ref[...]
        at = jnp.exp(zt - sp - r - c)
        if mask is not None:
            at = jnp.where(mask, at, 0.0)
        acc_ref[...] += lax.dot_general(at.astype(BF16), vp.astype(BF16),
                                        (((0,), (0,)), ((), ())), preferred_element_type=F32)
        c_ref[...] = c + r[0:1] + spm[0:1]

    @pl.when(j == 0)
    def _():
        acc_ref[...] = jnp.zeros(acc_ref.shape, F32)
        c_ref[...] = jnp.zeros(c_ref.shape, F32)
        rows = 16
        pad = jnp.zeros((rows - n_q, kn_ref.shape[2]), F32)
        kn = jnp.concatenate([kn_ref[0], pad], axis=0)
        vn = jnp.concatenate([vn_ref[0], pad], axis=0)
        s = lax.broadcasted_iota(jnp.int32, (rows, LANES), 0)
        col = lax.broadcasted_iota(jnp.int32, (rows, LANES), 1)
        valid = (s < col % n_q) & (col < n_heads * n_q)
        process(kn, vn, valid, rows)

    for i in range(pp):
        process(k_refs[i][...], v_refs[i][...], None, PAGE_SIZE)

    @pl.when(j == pl.num_programs(1) - 1)
    def _():
        acc = acc_ref[...]
        lane_head = lax.broadcasted_iota(jnp.int32, (n_q, acc.shape[1]), 1) // HEAD_DIM
        out = jnp.zeros((n_q, acc.shape[1]), F32)
        for h in range(n_heads):
            out = jnp.where(lane_head == h, acc[h * n_q:(h + 1) * n_q, :], out)
        o_ref[0] = out


def sb_sample(q, k_new, v_new, bias, cache_k, cache_v, page_table, layer, *, n_heads, pp=4):
    Bs, n_q, HD = q.shape
    n_pages = page_table.shape[1]
    pp = min(pp, n_pages)
    eye = jnp.eye(n_heads, dtype=F32)
    qbd = jnp.einsum('bqhd,hg->bhdgq', q.reshape(Bs, n_q, n_heads, HEAD_DIM), eye)
    qbd = qbd.reshape(Bs, HD, n_heads * n_q)
    qbd = jnp.pad(qbd, ((0, 0), (0, 0), (0, LANES - n_heads * n_q))).astype(BF16)
    bias_row = jnp.pad(jnp.repeat(bias.astype(F32), n_q), (0, LANES - n_heads * n_q)).reshape(1, LANES)

    def page_spec(i):
        return pl.BlockSpec(
            (None, None, PAGE_SIZE, HD),
            lambda b, j, pt, lay: (lay[0], pt[b, n_pages - 1 - (j * pp + i)], 0, 0))

    new_spec = pl.BlockSpec((1, n_q, HD), lambda b, j, pt, lay: (b, 0, 0))
    grid_spec = pltpu.PrefetchScalarGridSpec(
        num_scalar_prefetch=2,
        grid=(Bs, n_pages // pp),
        in_specs=[pl.BlockSpec((1, HD, LANES), lambda b, j, pt, lay: (b, 0, 0)),
                  pl.BlockSpec((1, LANES), lambda b, j, pt, lay: (0, 0)),
                  new_spec, new_spec]
                 + [page_spec(i) for i in range(pp)] + [page_spec(i) for i in range(pp)],
        out_specs=pl.BlockSpec((1, n_q, HD), lambda b, j, pt, lay: (b, 0, 0)),
        scratch_shapes=[pltpu.VMEM((LANES, HD), F32), pltpu.VMEM((1, LANES), F32)],
    )
    return pl.pallas_call(
        functools.partial(_sb_sample_kernel, pp=pp, n_heads=n_heads, n_q=n_q, scale=HEAD_DIM ** -0.5),
        grid_spec=grid_spec,
        out_shape=jax.ShapeDtypeStruct((Bs, n_q, HD), F32),
        compiler_params=_cparams(("parallel", "arbitrary")),
    )(page_table, jnp.full((1,), layer, jnp.int32), qbd, bias_row, k_new, v_new,
      *([cache_k] * pp), *([cache_v] * pp))


def _cumsum_rows(x, tril_bf16):
    C = x.shape[0]
    if C < 16:
        rows = [x[0:1]]
        for i in range(1, C):
            rows.append(rows[-1] + x[i:i + 1])
        return jnp.concatenate(rows, axis=0)
    h1 = x.astype(BF16)
    r1 = x - h1.astype(F32)
    h2 = r1.astype(BF16)
    h3 = (r1 - h2.astype(F32)).astype(BF16)
    dot = lambda h: jnp.dot(tril_bf16, h, preferred_element_type=F32)
    return dot(h1) + dot(h2) + dot(h3)


def _hgrn_kernel(z_ref, lb_ref, ng_ref, s0_ref, o_ref, sT_ref, st_ref, *, C, SB, n_chunks):
    ci = pl.program_id(1)
    W = lb_ref.shape[1]
    n_heads = W // HEAD_DIM
    n_sub = C // SB

    @pl.when(ci == 0)
    def _():
        st_ref[...] = s0_ref[0]

    ones_blk = _head_block_ones(W)
    blockmask = (lax.broadcasted_iota(jnp.int32, (W, W), 0) // HEAD_DIM
                 == lax.broadcasted_iota(jnp.int32, (W, W), 1) // HEAD_DIM)
    lane_head = lax.broadcasted_iota(jnp.int32, (SB, W), 1) // HEAD_DIM
    t_sub = lax.broadcasted_iota(jnp.int32, (SB, W), 0)
    tril = jnp.where(lax.broadcasted_iota(jnp.int32, (C, C), 0)
                     >= lax.broadcasted_iota(jnp.int32, (C, C), 1), 1.0, 0.0).astype(BF16)
    s_col = lax.broadcasted_iota(jnp.int32, (n_heads * SB, C), 1)
    lb = lb_ref[...]
    ng = ng_ref[...]

    def chunk(cc, carry):
        r0 = pl.multiple_of(cc * C, C)
        zc = z_ref[0, pl.ds(r0, C), :]
        q, fz, v, gate = zc[:, :W], zc[:, W:2 * W], zc[:, 2 * W:3 * W], zc[:, 3 * W:]
        f = lb + (1.0 - lb) * _sigmoid(fz)
        k = 1.0 - f
        b = _cumsum_rows(jnp.log(f), tril)
        st = st_ref[...]
        o_inter = _bdot_t(q * jnp.exp(b), st)
        outs = []
        for I in range(n_sub):
            lo_, hi_ = I * SB, (I + 1) * SB
            bI, qI, kI, vI = b[lo_:hi_], q[lo_:hi_], k[lo_:hi_], v[lo_:hi_]
            ds = []
            for s in range(SB):
                e = jnp.exp(jnp.minimum(bI - bI[s:s + 1], 0.0))
                ds.append(jnp.where(t_sub >= s, e * qI * kI[s:s + 1], 0.0))
            G = jnp.dot(jnp.concatenate(ds, axis=0).astype(BF16), ones_blk,
                        preferred_element_type=F32)
            od = G[0:SB] * vI[0:1]
            for s in range(1, SB):
                od = od + G[s * SB:(s + 1) * SB] * vI[s:s + 1]
            if I > 0:
                rho = b[lo_ - 1:lo_]
                qs = qI * jnp.exp(bI - rho)
                kt = k * jnp.exp(jnp.minimum(rho - b, 0.0))
                qst = jnp.concatenate([jnp.where(lane_head == h, qs, 0.0) for h in range(n_heads)],
                                      axis=0)
                att = jnp.where(s_col < lo_, _bdot_t(qst, kt), 0.0)
                R = _bdot(att, v)
                for h in range(n_heads):
                    od = od + jnp.where(lane_head == h, R[h * SB:(h + 1) * SB], 0.0)
            outs.append(od)
        o = o_inter + (outs[0] if n_sub == 1 else jnp.concatenate(outs, axis=0))
        blast = b[C - 1:C]
        kd = k * jnp.exp(blast - b)
        upd = lax.dot_general(v.astype(BF16), kd.astype(BF16), (((0,), (0,)), ((), ())),
                              preferred_element_type=F32)
        st_ref[...] = st * jnp.exp(blast) + jnp.where(blockmask, upd, 0.0)
        ms = _dot2(o * o, ones_blk) * (1.0 / HEAD_DIM)
        o_ref[0, pl.ds(r0, C), :] = o * lax.rsqrt(ms + RMS_EPS) * ng * _silu(gate)
        return carry

    lax.fori_loop(0, n_chunks, chunk, 0)

    @pl.when(ci == pl.num_programs(1) - 1)
    def _():
        sT_ref[0] = st_ref[...]


def hgrn2(zb, lb, norm_g, s0, *, rows_per_step=256):
    B, L, W4 = zb.shape
    W = W4 // 4
    H = W // HEAD_DIM
    C = math.gcd(L, 64)
    SB = min(16, C)
    rows = min(rows_per_step, L)
    eye = jnp.eye(H, dtype=F32)
    st0 = jnp.einsum('bhdv,hg->bhvgd', s0, eye).reshape(B, W, W)
    o, sT = pl.pallas_call(
        functools.partial(_hgrn_kernel, C=C, SB=SB, n_chunks=rows // C),
        grid=(B, L // rows),
        in_specs=[pl.BlockSpec((1, rows, W4), lambda b, i: (b, i, 0)),
                  _const_spec((1, W)), _const_spec((1, W)),
                  pl.BlockSpec((1, W, W), lambda b, i: (b, 0, 0))],
        out_specs=[pl.BlockSpec((1, rows, W), lambda b, i: (b, i, 0)),
                   pl.BlockSpec((1, W, W), lambda b, i: (b, 0, 0))],
        out_shape=[jax.ShapeDtypeStruct((B, L, W), F32), jax.ShapeDtypeStruct((B, W, W), F32)],
        scratch_shapes=[pltpu.VMEM((W, W), F32)],
        compiler_params=_cparams(("parallel", "arbitrary")),
    )(zb, lb.reshape(1, W), norm_g.reshape(1, W), st0)
    s5 = sT.reshape(B, H, HEAD_DIM, H, HEAD_DIM)
    s_new = jnp.einsum('bhvgd,hg->bhdv', s5, eye)
    return o, s_new


def _rwkv_pre_kernel(z_ref, sh0_ref, mu_ref, w0_ref, wup_ref, a0_ref, aup_ref, gup_ref,
                     kkw_ref, kaw_ref, rk_ref,
                     r_o, w_o, k2_o, kk_o, ka_o, v_o, bonus_o, g_o, shift_o, prev_ref):
    ti = pl.program_id(1)
    z = z_ref[0]
    tm, P = z.shape
    W = w0_ref.shape[1]

    @pl.when(ti == 0)
    def _():
        prev_ref[...] = sh0_ref[0]

    row = lax.broadcasted_iota(jnp.int32, (tm, P), 0)
    prev = jnp.where(row == 0, prev_ref[...], pltpu.roll(z, 1, axis=0))
    last = z[tm - 1:tm]
    prev_ref[...] = last
    shift_o[0] = last
    zs = z + (prev - z) * mu_ref[...]
    r, k, v, x4 = zs[:, :W], zs[:, W:2 * W], zs[:, 2 * W:3 * W], zs[:, 3 * W:]
    ones_blk = _head_block_ones(W)
    u = w0_ref[...] + _bdot(jnp.tanh(x4), wup_ref[...])
    w = jnp.exp(-jnp.exp(-_softplus(-u) - 0.5))
    a = _sigmoid(a0_ref[...] + _bdot(x4, aup_ref[...]))
    kkr = k * kkw_ref[...]
    kk = kkr / jnp.maximum(jnp.sqrt(_dot2(kkr * kkr, ones_blk)), 1e-12)
    k2 = k * (1.0 + (a - 1.0) * kaw_ref[...])
    r_o[0] = r
    w_o[0] = w
    k2_o[0] = k2
    kk_o[0] = kk
    ka_o[0] = kk * a
    for hp in range(W // LANES):
        v_o[0, hp] = v[:, hp * LANES:(hp + 1) * LANES]
    bonus_o[0] = _dot2(r * k2 * rk_ref[...], ones_blk) * v
    g_o[0] = _bdot(_sigmoid(x4), gup_ref[...])


def _rwkv_scan_kernel(src_ref, v_ref, s0_ref, o_ref, sT_ref, st_ref, lhs_ref, *, P, steps):
    c = pl.program_id(1)

    @pl.when(c == 0)
    def _():
        st_ref[...] = s0_ref[...]

    n_op = src_ref.shape[2]
    for p in range(P):
        x = src_ref[p, 0].reshape(n_op * HEAD_DIM, LANES)
        hi, lo = _split2(x)
        lhs_ref[p] = jnp.concatenate([hi, lo], axis=1)

    rr = lax.broadcasted_iota(jnp.int32, (2 * LANES, LANES), 0)
    cc = lax.broadcasted_iota(jnp.int32, (2 * LANES, LANES), 1)
    t_of_row = jnp.where((rr // HEAD_DIM) % 2 == cc // HEAD_DIM, rr % HEAD_DIM, -1)

    def step(t, carry):
        sel = jnp.where(t_of_row == t, 1.0, 0.0).astype(BF16)
        for p in range(P):
            cb = jnp.dot(lhs_ref[p], sel, preferred_element_type=F32)
            kk, w, ka, k2, r = (cb[i * HEAD_DIM:(i + 1) * HEAD_DIM] for i in range(5))
            s = st_ref[p]
            skk = jnp.sum(s * kk, axis=0, keepdims=True)
            sn = s * w - ka * skk + k2 * v_ref[p, pl.ds(t, 1), :]
            st_ref[p] = sn
            o_ref[p, pl.ds(t, 1), :] = jnp.sum(sn * r, axis=0, keepdims=True)
        return carry

    lax.fori_loop(0, steps, step, 0)

    @pl.when(c == pl.num_programs(1) - 1)
    def _():
        sT_ref[...] = st_ref[...]


def _rwkv_post_kernel(o_ref, bonus_ref, g_ref, gng_ref, gnb_ref, out_ref):
    o = jnp.concatenate([o_ref[0, hp] for hp in range(o_ref.shape[1])], axis=1)
    ones_blk = _head_block_ones(o.shape[1])
    inv = 1.0 / HEAD_DIM
    d = o - _dot2(o, ones_blk) * inv
    var = _dot2(d * d, ones_blk) * inv
    out_ref[0] = (d * lax.rsqrt(var + GN_EPS) * gng_ref[...] + gnb_ref[...] + bonus_ref[0]) * g_ref[0]


def rwkv7(za, shift0, s0, p, *, tm=512, pairs_per_step=16):
    B, L, P = za.shape
    W = p['rwkv_w0'].shape[0]
    H = W // HEAD_DIM
    HP = W // LANES
    tm = min(tm, L)
    n_low = P - 3 * W
    rank_w, rank_a = p['rwkv_w_up'].shape[0], p['rwkv_a_up'].shape[0]
    pad_rows = lambda m, r0: jnp.zeros((n_low, W), F32).at[r0:r0 + m.shape[0]].set(m).astype(BF16)
    wup = pad_rows(p['rwkv_w_up'], 0)
    aup = pad_rows(p['rwkv_a_up'], rank_w)
    gup = pad_rows(p['rwkv_g_up'], rank_w + rank_a)
    vec = lambda a: a.reshape(1, -1)
    tok = pl.BlockSpec((1, tm, W), lambda b, i: (b, i, 0))
    tok_sd = jax.ShapeDtypeStruct((B, L, W), F32)
    pair_spec = pl.BlockSpec((1, HP, tm, LANES), lambda b, i: (b, 0, i, 0))
    pair_sd = jax.ShapeDtypeStruct((B, HP, L, LANES), F32)
    r, w, k2, kk, ka, v, bonus, g, shift = pl.pallas_call(
        _rwkv_pre_kernel,
        grid=(B, L // tm),
        in_specs=[pl.BlockSpec((1, tm, P), lambda b, i: (b, i, 0)),
                  pl.BlockSpec((1, 1, P), lambda b, i: (b, 0, 0)),
                  _const_spec((1, P)), _const_spec((1, W)), _const_spec((n_low, W)),
                  _const_spec((1, W)), _const_spec((n_low, W)), _const_spec((n_low, W)),
                  _const_spec((1, W)), _const_spec((1, W)), _const_spec((1, W))],
        out_specs=[tok, tok, tok, tok, tok, pair_spec, tok, tok,
                   pl.BlockSpec((1, 1, P), lambda b, i: (b, 0, 0))],
        out_shape=[tok_sd, tok_sd, tok_sd, tok_sd, tok_sd, pair_sd, tok_sd, tok_sd,
                   jax.ShapeDtypeStruct((B, 1, P), F32)],
        scratch_shapes=[pltpu.VMEM((1, P), F32)],
        compiler_params=_cparams(("parallel", "arbitrary")),
    )(za, shift0.reshape(B, 1, P), vec(p['rwkv_mu']), vec(p['rwkv_w0']), wup, vec(p['rwkv_a0']),
      aup, gup, vec(p['rwkv_k_k']), vec(p['rwkv_k_a']), vec(p['rwkv_r_k']))

    steps = min(HEAD_DIM, L)
    NC = L // steps
    NP = B * HP
    src = jnp.stack([kk, w, ka, k2, r])
    src = src.reshape(5, B, NC, steps, HP, 2, HEAD_DIM).transpose(1, 4, 2, 0, 6, 5, 3)
    src = jnp.pad(src, ((0, 0),) * 6 + ((0, HEAD_DIM - steps),))
    src = src.reshape(NP, NC, 5, HEAD_DIM, LANES)
    st0 = s0.reshape(B, HP, 2, HEAD_DIM, HEAD_DIM).transpose(0, 1, 4, 2, 3).reshape(NP, HEAD_DIM, LANES)
    PP = min(pairs_per_step, NP)
    o, sT = pl.pallas_call(
        functools.partial(_rwkv_scan_kernel, P=PP, steps=steps),
        grid=(NP // PP, NC),
        in_specs=[pl.BlockSpec((PP, 1, 5, HEAD_DIM, LANES), lambda g_, c: (g_, c, 0, 0, 0)),
                  pl.BlockSpec((PP, steps, LANES), lambda g_, c: (g_, c, 0)),
                  pl.BlockSpec((PP, HEAD_DIM, LANES), lambda g_, c: (g_, 0, 0))],
        out_specs=[pl.BlockSpec((PP, steps, LANES), lambda g_, c: (g_, c, 0)),
                   pl.BlockSpec((PP, HEAD_DIM, LANES), lambda g_, c: (g_, 0, 0))],
        out_shape=[jax.ShapeDtypeStruct((NP, L, LANES), F32),
                   jax.ShapeDtypeStruct((NP, HEAD_DIM, LANES), F32)],
        scratch_shapes=[pltpu.VMEM((PP, HEAD_DIM, LANES), F32),
                        pltpu.VMEM((PP, 5 * HEAD_DIM, 2 * LANES), BF16)],
        compiler_params=_cparams(("parallel", "arbitrary")),
    )(src, v.reshape(NP, L, LANES), st0)
    s_new = sT.reshape(B, HP, HEAD_DIM, 2, HEAD_DIM).transpose(0, 1, 3, 4, 2).reshape(B, H, HEAD_DIM, HEAD_DIM)

    out = pl.pallas_call(
        _rwkv_post_kernel,
        grid=(B, L // tm),
        in_specs=[pair_spec, tok, tok, _const_spec((1, W)), _const_spec((1, W))],
        out_specs=tok,
        out_shape=tok_sd,
        compiler_params=_cparams(("parallel", "parallel")),
    )(o.reshape(B, HP, L, LANES), bonus, g, vec(p['rwkv_gn_g']), vec(p['rwkv_gn_b']))
    return out, s_new, shift.reshape(B, P)


def _route_kernel(x_ref, rt_ref, g_ref, s_ref):
    logits = lax.dot_general(rt_ref[...], x_ref[...], (((1,), (1,)), ((), ())),
                             precision=lax.Precision.HIGHEST, preferred_element_type=F32)
    n_e = logits.shape[0]
    e_id = lax.broadcasted_iota(jnp.int32, logits.shape, 0)
    m1 = jnp.max(logits, axis=0, keepdims=True)
    i1 = jnp.min(jnp.where(logits == m1, e_id, n_e), axis=0, keepdims=True)
    rest = jnp.where(e_id == i1, -jnp.inf, logits)
    m2 = jnp.max(rest, axis=0, keepdims=True)
    i2 = jnp.min(jnp.where(rest == m2, e_id, n_e), axis=0, keepdims=True)
    t = jnp.exp(m2 - m1)
    g1 = 1.0 / (1.0 + t)
    g2 = t / (1.0 + t)
    g_ref[...] = jnp.where(e_id == i1, g1, jnp.where(e_id == i2, g2, 0.0))
    s_ref[...] = jnp.where(e_id == i1, 1.0, jnp.where(e_id == i2, 1.0, 0.0))


def _moe_kernel(xb_ref, g_ref, s_ref, wg_ref, wu_ref, wd_ref, o_ref, rank_ref, xg_ref, y_ref, *, R):
    e = pl.program_id(1)
    f = pl.program_id(2)
    n_f = pl.num_programs(2)
    tm = xb_ref.shape[0]

    @pl.when((e == 0) & (f == 0))
    def _():
        before = jnp.where(lax.broadcasted_iota(jnp.int32, (tm, tm), 0)
                           < lax.broadcasted_iota(jnp.int32, (tm, tm), 1), 1.0, 0.0).astype(BF16)
        rank_ref[...] = jnp.dot(s_ref[...].astype(BF16), before, preferred_element_type=F32)
        o_ref[...] = jnp.zeros(o_ref.shape, F32)

    sel = s_ref[pl.ds(e, 1), :]
    key = jnp.where(sel > 0.0, rank_ref[pl.ds(e, 1), :], -1.0)
    n_chunks = (jnp.sum(sel).astype(jnp.int32) + (R - 1)) // R
    r_id = lax.broadcasted_iota(jnp.int32, (R, tm), 0)

    def one_hot(c):
        return jnp.where(key == (r_id + c * R).astype(F32), 1.0, 0.0).astype(BF16)

    def rows_of(c):
        return pl.ds(pl.multiple_of(c * R, R), R)

    @pl.when(f == 0)
    def _():
        def gather(c, carry):
            xg_ref[rows_of(c), :] = jnp.dot(one_hot(c), xb_ref[...],
                                            preferred_element_type=F32).astype(BF16)
            return carry
        lax.fori_loop(0, n_chunks, gather, 0)

    def expert(c, carry):
        xg = xg_ref[rows_of(c), :]
        gate = jnp.dot(xg, wg_ref[...], preferred_element_type=F32)
        up = jnp.dot(xg, wu_ref[...], preferred_element_type=F32)
        y = _bdot(_silu(gate) * up, wd_ref[...])

        @pl.when(f == 0)
        def _():
            y_ref[rows_of(c), :] = y

        @pl.when(f > 0)
        def _():
            y_ref[rows_of(c), :] += y
        return carry

    lax.fori_loop(0, n_chunks, expert, 0)

    @pl.when(f == n_f - 1)
    def _():
        gt = g_ref[pl.ds(e, 1), :]
        h1 = gt.astype(BF16)
        r1 = gt - h1.astype(F32)
        h2 = r1.astype(BF16)
        h3 = (r1 - h2.astype(F32)).astype(BF16)
        g3 = jnp.concatenate([h1, h2, h3, jnp.zeros((5, tm), BF16)], axis=0)

        def scatter(c, carry):
            p = one_hot(c)
            g_row = jnp.sum(lax.dot_general(p, g3, (((1,), (1,)), ((), ())),
                                            preferred_element_type=F32), axis=1, keepdims=True)
            yw = (y_ref[rows_of(c), :] * g_row).astype(BF16)
            o_ref[...] += lax.dot_general(p, yw, (((0,), (0,)), ((), ())),
                                          preferred_element_type=F32)
            return carry
        lax.fori_loop(0, n_chunks, scatter, 0)


def moe_ffn(x, router, wg_bf16, wu_bf16, wd_bf16, *, tm=1024, tf=896, R=256):
    T, D = x.shape
    n_e, _, F = wg_bf16.shape
    tm = min(tm, T)
    R = min(R, tm)
    gates, sel = pl.pallas_call(
        _route_kernel,
        grid=(T // tm,),
        in_specs=[pl.BlockSpec((tm, D), lambda i: (i, 0)), _const_spec((n_e, D))],
        out_specs=[pl.BlockSpec((n_e, tm), lambda i: (0, i))] * 2,
        out_shape=[jax.ShapeDtypeStruct((n_e, T), F32)] * 2,
        compiler_params=_cparams(("parallel",)),
    )(x, router.T)
    return pl.pallas_call(
        functools.partial(_moe_kernel, R=R),
        grid=(T // tm, n_e, F // tf),
        in_specs=[pl.BlockSpec((tm, D), lambda i, e, f: (i, 0)),
                  pl.BlockSpec((n_e, tm), lambda i, e, f: (0, i)),
                  pl.BlockSpec((n_e, tm), lambda i, e, f: (0, i)),
                  pl.BlockSpec((None, D, tf), lambda i, e, f: (e, 0, f)),
                  pl.BlockSpec((None, D, tf), lambda i, e, f: (e, 0, f)),
                  pl.BlockSpec((None, tf, D), lambda i, e, f: (e, f, 0))],
        out_specs=pl.BlockSpec((tm, D), lambda i, e, f: (i, 0)),
        out_shape=jax.ShapeDtypeStruct((T, D), F32),
        scratch_shapes=[pltpu.VMEM((n_e, tm), F32), pltpu.VMEM((tm, D), BF16),
                        pltpu.VMEM((tm, D), F32)],
        compiler_params=_cparams(("parallel", "arbitrary", "arbitrary")),
    )(x.astype(BF16), gates, sel, wg_bf16, wu_bf16, wd_bf16)


def _layer(x, seq_len, p, lb, states, sb_past, mem_k, mem_v, ffn, alpha, first):
    T, D = x.shape
    B = T // seq_len
    W = p['rwkv_w0'].shape[0]
    pa = p['rwkv_mu'].shape[0]
    wc = p['sb_bias'].shape[0] * HEAD_DIM
    x, za, zb, q, k, v = in_proj(x, p['ln_in_g'], p['ln_in_b'], p['w_in'], apply_ln=first,
                                 widths=(pa, 4 * W, wc, wc, wc))
    to3 = lambda a: a.reshape(B, seq_len, a.shape[-1])
    o_a, rwkv_s, shift = rwkv7(to3(za), states[1], states[0], p)
    o_b, hgrn_s = hgrn2(to3(zb), lb, p['hgrn_norm_g'], states[2])
    n_heads = p['sb_bias'].shape[0]
    if sb_past is None:
        o_c = sb_prompt(to3(q), to3(k), to3(v), p['sb_bias'], n_heads=n_heads)
    else:
        o_c = sb_sample(to3(q), to3(k), to3(v), p['sb_bias'], *sb_past, n_heads=n_heads)
    x = mix_out(o_a.reshape(T, W), o_b.reshape(T, W), o_c.reshape(T, wc), p['w_out'], x,
                p['ln_mix_g'], p['ln_mix_b'], alpha=alpha)
    x = mem_block(x, p['mem_wq'], mem_k, mem_v, p['mem_wo'], p['ln_mem_g'], p['ln_mem_b'],
                  alpha=alpha, seq_len=seq_len)
    x = ffn(x)
    return x, (rwkv_s, shift, hgrn_s, to3(k), to3(v))


def kernel(x_prompt, x_sample, cache_sb_k, cache_sb_v, state_rwkv, state_rwkv_shift, state_hgrn,
           cache_mem_k, cache_mem_v, page_table, mem_prompt, ln_in_g, ln_in_b, w_in, rwkv_mu, rwkv_w0,
           rwkv_w_up, rwkv_a0, rwkv_a_up, rwkv_g_up, rwkv_k_k, rwkv_k_a, rwkv_r_k, rwkv_gn_g, rwkv_gn_b,
           hgrn_lb, hgrn_norm_g, sb_bias, w_out, ln_mix_g, ln_mix_b, mem_wq, mem_wk, mem_wv, mem_wo,
           ln_mem_g, ln_mem_b, ffn_w_gate, ffn_w_up, ffn_w_down, moe_router, moe_w_gate, moe_w_up,
           moe_w_down, ln_ffn_g, ln_ffn_b):
    B, L, D = x_prompt.shape
    Bs, Ls, _ = x_sample.shape
    depth = w_in.shape[0]
    H = state_rwkv.shape[2]
    n_heads_c = sb_bias.shape[1]
    hd_c = n_heads_c * HEAD_DIM
    n_mem = mem_prompt.shape[1]
    alpha = (2 * depth) ** 0.25
    bf = lambda a: a.astype(BF16)

    lb_sm = jax.nn.softmax(hgrn_lb.astype(F32), axis=0)
    lb_all = jnp.cumsum(lb_sm, axis=0) - lb_sm[0]

    cache_k = cache_sb_k.reshape(cache_sb_k.shape[0], cache_sb_k.shape[1], PAGE_SIZE, hd_c)
    cache_v = cache_sb_v.reshape(cache_sb_v.shape[0], cache_sb_v.shape[1], PAGE_SIZE, hd_c)
    mem_flat = mem_prompt.reshape(B * n_mem, D)
    zeros_p = (jnp.zeros((B, H, HEAD_DIM, HEAD_DIM), F32), jnp.zeros((B, rwkv_mu.shape[1]), F32),
               jnp.zeros((B, H, HEAD_DIM, HEAD_DIM), F32))

    xp = x_prompt.reshape(B * L, D)
    xs = x_sample.reshape(Bs * Ls, D)
    outs_p = [[] for _ in range(7)]
    outs_s = [[] for _ in range(5)]
    for l in range(depth):
        p = dict(ln_in_g=ln_in_g, ln_in_b=ln_in_b, w_in=bf(w_in[l]), w_out=bf(w_out[l]),
                 rwkv_mu=rwkv_mu[l], rwkv_w0=rwkv_w0[l], rwkv_w_up=rwkv_w_up[l], rwkv_a0=rwkv_a0[l],
                 rwkv_a_up=rwkv_a_up[l], rwkv_g_up=rwkv_g_up[l], rwkv_k_k=rwkv_k_k[l],
                 rwkv_k_a=rwkv_k_a[l], rwkv_r_k=rwkv_r_k[l], rwkv_gn_g=rwkv_gn_g[l],
                 rwkv_gn_b=rwkv_gn_b[l], hgrn_norm_g=hgrn_norm_g[l], sb_bias=sb_bias[l],
                 ln_mix_g=ln_mix_g[l], ln_mix_b=ln_mix_b[l], mem_wq=bf(mem_wq[l]),
                 mem_wo=bf(mem_wo[l]), ln_mem_g=ln_mem_g[l], ln_mem_b=ln_mem_b[l])
        j = l // 2
        if l % 2 == 0:
            wg, wu, wd = bf(ffn_w_gate[j]), bf(ffn_w_up[j]), bf(ffn_w_down[j])
            ffn = lambda x, wg=wg, wu=wu, wd=wd, l=l: ffn_block(
                x, wg, wu, wd, ln_ffn_g[l], ln_ffn_b[l], alpha=alpha)
        else:
            wg, wu, wd = bf(moe_w_gate[j]), bf(moe_w_up[j]), bf(moe_w_down[j])
            ffn = lambda x, wg=wg, wu=wu, wd=wd, j=j, l=l: res_ln(
                x, moe_ffn(x, moe_router[j], wg, wu, wd), ln_ffn_g[l], ln_ffn_b[l], alpha=alpha)
        first = l == 0
        mkv = matmul(mem_flat, bf(jnp.concatenate([mem_wk[l], mem_wv[l]], axis=1)))
        mk = mkv[:, :D].reshape(B, n_mem, D)
        mv = mkv[:, D:].reshape(B, n_mem, D)
        xp, (s_a, sh, s_b, k_new, v_new) = _layer(xp, L, p, lb_all[l], zeros_p, None, mk, mv, ffn,
                                                  alpha, first)
        for lst, val in zip(outs_p, (k_new.reshape(B, L, n_heads_c, HEAD_DIM),
                                     v_new.reshape(B, L, n_heads_c, HEAD_DIM), s_a, sh, s_b,
                                     mk.reshape(B, n_mem, N_MEM_HEADS, D // N_MEM_HEADS),
                                     mv.reshape(B, n_mem, N_MEM_HEADS, D // N_MEM_HEADS))):
            lst.append(val)
        states = (state_rwkv[l], state_rwkv_shift[l], state_hgrn[l])
        xs, (s_a, sh, s_b, k_new, v_new) = _layer(
            xs, Ls, p, lb_all[l], states, (cache_k, cache_v, page_table, l),
            cache_mem_k[l].reshape(Bs, n_mem, D), cache_mem_v[l].reshape(Bs, n_mem, D), ffn,
            alpha, first)
        for lst, val in zip(outs_s, (k_new.reshape(Bs, Ls, n_heads_c, HEAD_DIM),
                                     v_new.reshape(Bs, Ls, n_heads_c, HEAD_DIM), s_a, sh, s_b)):
            lst.append(val)
    return (xp.reshape(B, L, D), xs.reshape(Bs, Ls, D),
            *(jnp.stack(o) for o in outs_p), *(jnp.stack(o) for o in outs_s))
```

```python
import functools
import math

import jax
import jax.numpy as jnp
from jax import lax
from jax.experimental import pallas as pl
from jax.experimental.pallas import tpu as pltpu

F32 = jnp.float32
BF16 = jnp.bfloat16

HEAD_DIM = 64
LANES = 128
PAGE_SIZE = 128
N_MEM_HEADS = 4
TOP_K = 2
LN_EPS = 1e-5
GN_EPS = 64e-5
RMS_EPS = 1e-6
VMEM_LIMIT = 56 * 1024 * 1024


def _cparams(sem):
    return pltpu.CompilerParams(dimension_semantics=sem, vmem_limit_bytes=VMEM_LIMIT)


def _const_spec(shape):
    nd = len(shape)
    return pl.BlockSpec(shape, lambda *_: (0,) * nd, pipeline_mode=pl.Buffered(1))


def _bdot(a, b):
    return jnp.dot(a.astype(BF16), b.astype(BF16), preferred_element_type=F32)


def _bdot_t(a, b):
    return lax.dot_general(a.astype(BF16), b.astype(BF16), (((1,), (1,)), ((), ())),
                           preferred_element_type=F32)


def _split2(x):
    hi = x.astype(BF16)
    lo = (x - hi.astype(F32)).astype(BF16)
    return hi, lo


def _dot2(x, w_bf16):
    hi, lo = _split2(x)
    return (jnp.dot(hi, w_bf16, preferred_element_type=F32)
            + jnp.dot(lo, w_bf16, preferred_element_type=F32))


def _layer_norm(x, g, b):
    mu = jnp.mean(x, -1, keepdims=True)
    xc = x - mu
    var = jnp.mean(xc * xc, -1, keepdims=True)
    return xc * lax.rsqrt(var + LN_EPS) * g + b


def _sigmoid(x):
    return 1.0 / (1.0 + jnp.exp(-x))


def _silu(x):
    return x * _sigmoid(x)


def _softplus(x):
    return jnp.maximum(x, 0.0) + jnp.log(1.0 + jnp.exp(-jnp.abs(x)))


def _head_block_ones(width):
    r = lax.broadcasted_iota(jnp.int32, (width, width), 0) // HEAD_DIM
    c = lax.broadcasted_iota(jnp.int32, (width, width), 1) // HEAD_DIM
    return jnp.where(r == c, 1.0, 0.0).astype(BF16)


def _in_proj_kernel(x_ref, g_ref, b_ref, w_ref, *refs, apply_ln, n_alias, n_heads):
    xn_ref, za_ref, zb_ref, q_ref, k_ref, v_ref, k4_ref, v4_ref = refs[n_alias:]
    x = x_ref[...]
    if apply_ln:
        x = _layer_norm(x, g_ref[...], b_ref[...])
    xn_ref[...] = x
    z = _bdot(x, w_ref[...])
    c0 = 0
    for z_ref in (za_ref, zb_ref, q_ref, k_ref, v_ref):
        z_ref[...] = z[:, c0:c0 + z_ref.shape[1]]
        c0 += z_ref.shape[1]
    tm = x.shape[0]
    hd = k_ref.shape[1]
    for src0, dst in ((c0 - 2 * hd, k4_ref), (c0 - hd, v4_ref)):
        for h in range(n_heads):
            dst[pl.ds(h, tm, stride=n_heads), :] = z[:, src0 + h * HEAD_DIM:src0 + (h + 1) * HEAD_DIM]


def in_proj(x, g, b, w_bf16, kv_stack, layer, depth, *, apply_ln, widths, tm=512):
    T, D = x.shape
    N = w_bf16.shape[1]
    assert sum(widths) == N and all(wd % LANES == 0 for wd in widths)
    n_heads = widths[-1] // HEAD_DIM
    tm = min(tm, T)
    nt = T // tm
    row = lambda width: pl.BlockSpec((tm, width), lambda i: (i, 0))
    stack_spec = pl.BlockSpec((tm * n_heads, HEAD_DIM), lambda i: (layer * nt + i, 0))
    stack_sd = jax.ShapeDtypeStruct((depth * T * n_heads, HEAD_DIM), F32)
    n_alias = 0 if kv_stack is None else 2
    n_out = 6
    return pl.pallas_call(
        functools.partial(_in_proj_kernel, apply_ln=apply_ln, n_alias=n_alias, n_heads=n_heads),
        grid=(nt,),
        in_specs=[row(D), _const_spec((1, D)), _const_spec((1, D)), _const_spec((D, N))]
                 + [pl.BlockSpec(memory_space=pl.ANY)] * n_alias,
        out_specs=[row(D)] + [row(wd) for wd in widths] + [stack_spec, stack_spec],
        out_shape=[jax.ShapeDtypeStruct((T, D), F32)]
                  + [jax.ShapeDtypeStruct((T, wd), F32) for wd in widths] + [stack_sd, stack_sd],
        input_output_aliases={4 + a: n_out + a for a in range(n_alias)},
        compiler_params=_cparams(("parallel",)),
    )(x, g.reshape(1, D), b.reshape(1, D), w_bf16, *(kv_stack or ()))


def _matmul2_kernel(x_ref, w1_ref, w2_ref, o1_ref, o2_ref):
    xb = x_ref[...].astype(BF16)
    o1_ref[...] = jnp.dot(xb, w1_ref[...], preferred_element_type=F32)
    o2_ref[...] = jnp.dot(xb, w2_ref[...], preferred_element_type=F32)


def matmul2(x, w1_bf16, w2_bf16, tm=512):
    T, K = x.shape
    N = w1_bf16.shape[1]
    tm = min(tm, T)
    out = pl.BlockSpec((tm, N), lambda i: (i, 0))
    return pl.pallas_call(
        _matmul2_kernel,
        grid=(T // tm,),
        in_specs=[pl.BlockSpec((tm, K), lambda i: (i, 0)), _const_spec((K, N)), _const_spec((K, N))],
        out_specs=[out, out],
        out_shape=[jax.ShapeDtypeStruct((T, N), F32)] * 2,
        compiler_params=_cparams(("parallel",)),
    )(x, w1_bf16, w2_bf16)


def _mix_out_kernel(oa_ref, ob_ref, oc_ref, wa_ref, wb_ref, wc_ref, res_ref, g_ref, b_ref, o_ref,
                    *, alpha):
    h = (_bdot(oa_ref[...], wa_ref[...]) + _bdot(ob_ref[...], wb_ref[...])
         + _bdot(oc_ref[...], wc_ref[...]))
    o_ref[...] = _layer_norm(alpha * res_ref[...] + h, g_ref[...], b_ref[...])


def mix_out(oa, ob, oc, w_out_bf16, res, g, b, *, alpha, tm=512):
    T, D = res.shape
    wa, wb, wc = oa.shape[1], ob.shape[1], oc.shape[1]
    tm = min(tm, T)
    row = lambda width: pl.BlockSpec((tm, width), lambda i: (i, 0))
    return pl.pallas_call(
        functools.partial(_mix_out_kernel, alpha=alpha),
        grid=(T // tm,),
        in_specs=[row(wa), row(wb), row(wc),
                  _const_spec((wa, D)), _const_spec((wb, D)), _const_spec((wc, D)),
                  row(D), _const_spec((1, D)), _const_spec((1, D))],
        out_specs=row(D),
        out_shape=jax.ShapeDtypeStruct((T, D), F32),
        compiler_params=_cparams(("parallel",)),
    )(oa, ob, oc, w_out_bf16[:wa], w_out_bf16[wa:wa + wb], w_out_bf16[wa + wb:], res,
      g.reshape(1, D), b.reshape(1, D))


def _res_ln_kernel(x_ref, y_ref, g_ref, b_ref, o_ref, *, alpha):
    o_ref[...] = _layer_norm(alpha * x_ref[...] + y_ref[...], g_ref[...], b_ref[...])


def res_ln(x, y, g, b, *, alpha, tm=1024):
    T, D = x.shape
    tm = min(tm, T)
    row = pl.BlockSpec((tm, D), lambda i: (i, 0))
    return pl.pallas_call(
        functools.partial(_res_ln_kernel, alpha=alpha),
        grid=(T // tm,),
        in_specs=[row, row, _const_spec((1, D)), _const_spec((1, D))],
        out_specs=row,
        out_shape=jax.ShapeDtypeStruct((T, D), F32),
        compiler_params=_cparams(("parallel",)),
    )(x, y, g.reshape(1, D), b.reshape(1, D))


def _mem_kernel(x_ref, wq_ref, mk_ref, mv_ref, wo_ref, g_ref, b_ref, o_ref, att_ref,
                *, alpha, nb, rows):
    x = x_ref[...]
    q = _bdot(x, wq_ref[...])
    D = q.shape[1]
    dh = D // N_MEM_HEADS
    scale = dh ** -0.5
    for bi in range(nb):
        r0 = bi * rows
        for h in range(N_MEM_HEADS):
            c0 = h * dh
            qh = q[r0:r0 + rows, c0:c0 + dh]
            kh = mk_ref[bi, :, c0:c0 + dh]
            vh = mv_ref[bi, :, c0:c0 + dh]
            s = _bdot_t(qh, kh) * scale
            s = s - jnp.max(s, -1, keepdims=True)
            p = jnp.exp(s)
            p = p / jnp.sum(p, -1, keepdims=True)
            att_ref[r0:r0 + rows, c0:c0 + dh] = _bdot(p, vh)
    y = _bdot(att_ref[...], wo_ref[...])
    o_ref[...] = _layer_norm(alpha * x + y, g_ref[...], b_ref[...])


def mem_block(x, wq_bf16, mk, mv, wo_bf16, g, b, *, alpha, seq_len, tm=512):
    T, D = x.shape
    B, n_mem, _ = mk.shape
    if seq_len >= tm:
        nb, rows = 1, tm
        mem_map = lambda i: (i // (seq_len // tm), 0, 0)
    else:
        nb, rows = min(B, 8), seq_len
        tm = nb * rows
        mem_map = lambda i: (i, 0, 0)
    row = pl.BlockSpec((tm, D), lambda i: (i, 0))
    return pl.pallas_call(
        functools.partial(_mem_kernel, alpha=alpha, nb=nb, rows=rows),
        grid=(T // tm,),
        in_specs=[row, _const_spec((D, D)),
                  pl.BlockSpec((nb, n_mem, D), mem_map), pl.BlockSpec((nb, n_mem, D), mem_map),
                  _const_spec((D, D)), _const_spec((1, D)), _const_spec((1, D))],
        out_specs=row,
        out_shape=jax.ShapeDtypeStruct((T, D), F32),
        scratch_shapes=[pltpu.VMEM((tm, D), F32)],
        compiler_params=_cparams(("parallel",)),
    )(x, wq_bf16, mk, mv, wo_bf16, g.reshape(1, D), b.reshape(1, D))


def _ffn_kernel(x_ref, wg_ref, wu_ref, wd_ref, g_ref, b_ref, o_ref, *, alpha, tf):
    x = x_ref[...]
    xb = x.astype(BF16)
    F = wg_ref.shape[1]
    y = jnp.zeros(x.shape, F32)
    for f0 in range(0, F, tf):
        gate = jnp.dot(xb, wg_ref[:, f0:f0 + tf], preferred_element_type=F32)
        up = jnp.dot(xb, wu_ref[:, f0:f0 + tf], preferred_element_type=F32)
        y = y + _bdot(_silu(gate) * up, wd_ref[f0:f0 + tf, :])
    o_ref[...] = _layer_norm(alpha * x + y, g_ref[...], b_ref[...])


def ffn_block(x, wg_bf16, wu_bf16, wd_bf16, g, b, *, alpha, tm=512, tf=256):
    T, D = x.shape
    F = wg_bf16.shape[1]
    tm = min(tm, T)
    row = pl.BlockSpec((tm, D), lambda i: (i, 0))
    return pl.pallas_call(
        functools.partial(_ffn_kernel, alpha=alpha, tf=tf),
        grid=(T // tm,),
        in_specs=[row, _const_spec((D, F)), _const_spec((D, F)), _const_spec((F, D)),
                  _const_spec((1, D)), _const_spec((1, D))],
        out_specs=row,
        out_shape=jax.ShapeDtypeStruct((T, D), F32),
        compiler_params=_cparams(("parallel",)),
    )(x, wg_bf16, wu_bf16, wd_bf16, g.reshape(1, D), b.reshape(1, D))


def _sb_prompt_kernel(bias_ref, q_ref, k_ref, v_ref, o_ref, acc_ref, c_ref, *, tq, tk, scale):
    hp = pl.program_id(1)
    qi = pl.program_id(2)
    n_sub = tq // tk
    q = q_ref[0] * scale
    half = lax.broadcasted_iota(jnp.int32, (tq, LANES), 1) // HEAD_DIM
    qm = [jnp.where(half == h2, q, 0.0).astype(BF16) for h2 in range(2)]
    bias = [bias_ref[2 * hp + h2] for h2 in range(2)]
    rr = lax.broadcasted_iota(jnp.int32, (2 * tk, 2 * tk), 0) % tk
    cc = lax.broadcasted_iota(jnp.int32, (2 * tk, 2 * tk), 1)
    mw = jnp.where((cc >= tk) | (rr > cc), 1.0, 0.0).astype(BF16)
    causal = (lax.broadcasted_iota(jnp.int32, (tq, tq), 1)
              < lax.broadcasted_iota(jnp.int32, (tq, tq), 0))
    acc_ref[...] = jnp.zeros(acc_ref.shape, F32)
    c_ref[...] = jnp.zeros(c_ref.shape, F32)

    def block(j, masked):
        start = pl.multiple_of(j * tq, tq)
        kb = k_ref[0, pl.ds(start, tq), :].astype(BF16)
        vb = v_ref[0, pl.ds(start, tq), :].astype(BF16)
        for h2 in range(2):
            z = _bdot_t(qm[h2], kb) + bias[h2]
            sp = _softplus(z)
            spm = jnp.where(causal, sp, 0.0) if masked else sp
            hi, lo = _split2(spm)
            c = c_ref[h2]
            parts = [None] * n_sub
            for s in reversed(range(n_sub)):
                sl = slice(s * tk, (s + 1) * tk)
                r = jnp.dot(jnp.concatenate([hi[:, sl], lo[:, sl]], axis=1), mw,
                            preferred_element_type=F32)
                parts[s] = z[:, sl] - sp[:, sl] - r[:, :tk] - c
                c = c + r[:, tk:]
            c_ref[h2] = c
            a = jnp.exp(parts[0] if n_sub == 1 else jnp.concatenate(parts, axis=1))
            if masked:
                a = jnp.where(causal, a, 0.0)
            acc_ref[h2] += jnp.dot(a.astype(BF16), vb, preferred_element_type=F32)

    block(qi, True)

    def body(i, carry):
        block(qi - 1 - i, False)
        return carry

    lax.fori_loop(0, qi, body, 0)
    o_ref[0] = jnp.where(half == 0, acc_ref[0], acc_ref[1])


def sb_prompt(q, k, v, bias, *, n_heads, tq=512, tk=128):
    B, L, _ = q.shape
    tq = min(tq, L)
    tk = min(tk, tq)
    grid_spec = pltpu.PrefetchScalarGridSpec(
        num_scalar_prefetch=1,
        grid=(B, n_heads // 2, L // tq),
        in_specs=[pl.BlockSpec((1, tq, LANES), lambda b, hp, qi, bias: (b, qi, hp)),
                  pl.BlockSpec((1, L, LANES), lambda b, hp, qi, bias: (b, 0, hp)),
                  pl.BlockSpec((1, L, LANES), lambda b, hp, qi, bias: (b, 0, hp))],
        out_specs=pl.BlockSpec((1, tq, LANES), lambda b, hp, qi, bias: (b, qi, hp)),
        scratch_shapes=[pltpu.VMEM((2, tq, LANES), F32), pltpu.VMEM((2, tq, tk), F32)],
    )
    return pl.pallas_call(
        functools.partial(_sb_prompt_kernel, tq=tq, tk=tk, scale=HEAD_DIM ** -0.5),
        grid_spec=grid_spec,
        out_shape=jax.ShapeDtypeStruct((B, L, n_heads * HEAD_DIM), F32),
        compiler_params=_cparams(("parallel", "parallel", "arbitrary")),
    )(bias.astype(F32), q, k, v)


def _sb_sample_kernel(pt_ref, lay_ref, qbd_ref, bias_ref, kn_ref, vn_ref, *rest,
                      pp, n_heads, n_q, scale):
    k_refs, v_refs = rest[:pp], rest[pp:2 * pp]
    o_ref, acc_ref, c_ref = rest[2 * pp:]
    j = pl.program_id(1)
    bias = bias_ref[...]

    def weights(zt, mask, blk):
        sp = _softplus(zt)
        spm = sp if mask is None else jnp.where(mask, sp, 0.0)
        hi, lo = _split2(spm)
        rr = lax.broadcasted_iota(jnp.int32, (blk, 2 * blk), 0)
        cc = lax.broadcasted_iota(jnp.int32, (blk, 2 * blk), 1) % blk
        later = jnp.where(cc > rr, 1.0, 0.0).astype(BF16)
        c = c_ref[...]
        parts = []
        for i in range(zt.shape[0] // blk):
            sl = slice(i * blk, (i + 1) * blk)
            r = jnp.dot(later, jnp.concatenate([hi[sl], lo[sl]], axis=0),
                        preferred_element_type=F32)
            parts.append(zt[sl] - sp[sl] - r - c)
            c = c + r[0:1] + spm[i * blk:i * blk + 1]
        c_ref[...] = c
        at = jnp.exp(parts[0] if len(parts) == 1 else jnp.concatenate(parts, axis=0))
        return at if mask is None else jnp.where(mask, at, 0.0)

    @pl.when(j == 0)
    def _():
        c_ref[...] = jnp.zeros(c_ref.shape, F32)
        rows = 16
        pad = jnp.zeros((rows - n_q, kn_ref.shape[2]), F32)
        kn = jnp.concatenate([kn_ref[0], pad], axis=0)
        vn = jnp.concatenate([vn_ref[0], pad], axis=0)
        s = lax.broadcasted_iota(jnp.int32, (rows, LANES), 0)
        col = lax.broadcasted_iota(jnp.int32, (rows, LANES), 1)
        valid = (s < col % n_q) & (col < n_heads * n_q)
        zt = jnp.dot(kn.astype(BF16), qbd_ref[0], preferred_element_type=F32) * scale + bias
        at = weights(zt, valid, rows)
        res = lax.dot_general(at.astype(BF16), vn.astype(BF16), (((0,), (0,)), ((), ())),
                              preferred_element_type=F32)
        for h in range(n_heads):
            acc_ref[h] = res[h * n_q:(h + 1) * n_q, h * HEAD_DIM:(h + 1) * HEAD_DIM]

    def head_rows(ref, h):
        return ref[pl.ds(h, PAGE_SIZE, stride=n_heads), :]

    zt = None
    for h in range(n_heads):
        kh = jnp.concatenate([head_rows(kr, h) for kr in k_refs], axis=0).astype(BF16)
        part = jnp.dot(kh, qbd_ref[0, h * HEAD_DIM:(h + 1) * HEAD_DIM, :],
                       preferred_element_type=F32)
        zt = part if zt is None else zt + part
    a = weights(zt * scale + bias, None, PAGE_SIZE).T
    for h in range(n_heads):
        vh = jnp.concatenate([head_rows(vr, h) for vr in v_refs], axis=0).astype(BF16)
        acc_ref[h] += jnp.dot(a[h * n_q:(h + 1) * n_q, :].astype(BF16), vh,
                              preferred_element_type=F32)

    @pl.when(j == pl.num_programs(1) - 1)
    def _():
        o_ref[0] = acc_ref[...]


def sb_sample(q, k_new, v_new, bias, cache_k, cache_v, page_table, layer, *, n_heads, pp=8):
    Bs, n_q, HD = q.shape
    n_pages = page_table.shape[1]
    pp = min(pp, n_pages)
    eye = jnp.eye(n_heads, dtype=F32)
    qbd = jnp.einsum('bqhd,hg->bhdgq', q.reshape(Bs, n_q, n_heads, HEAD_DIM), eye)
    qbd = qbd.reshape(Bs, HD, n_heads * n_q)
    qbd = jnp.pad(qbd, ((0, 0), (0, 0), (0, LANES - n_heads * n_q))).astype(BF16)
    bias_row = jnp.pad(jnp.repeat(bias.astype(F32), n_q), (0, LANES - n_heads * n_q)).reshape(1, LANES)

    pool_shape = cache_k.shape[:2] + (PAGE_SIZE * n_heads, HEAD_DIM)
    cache_k = cache_k.reshape(pool_shape)
    cache_v = cache_v.reshape(pool_shape)

    def page_spec(i):
        return pl.BlockSpec(
            (None, None, PAGE_SIZE * n_heads, HEAD_DIM),
            lambda b, j, pt, lay: (lay[0], pt[b, n_pages - 1 - (j * pp + i)], 0, 0))

    new_spec = pl.BlockSpec((1, n_q, HD), lambda b, j, pt, lay: (b, 0, 0))
    grid_spec = pltpu.PrefetchScalarGridSpec(
        num_scalar_prefetch=2,
        grid=(Bs, n_pages // pp),
        in_specs=[pl.BlockSpec((1, HD, LANES), lambda b, j, pt, lay: (b, 0, 0)),
                  pl.BlockSpec((1, LANES), lambda b, j, pt, lay: (0, 0)),
                  new_spec, new_spec]
                 + [page_spec(i) for i in range(pp)] + [page_spec(i) for i in range(pp)],
        out_specs=pl.BlockSpec((1, n_heads, n_q, HEAD_DIM), lambda b, j, pt, lay: (b, 0, 0, 0)),
        scratch_shapes=[pltpu.VMEM((n_heads, n_q, HEAD_DIM), F32), pltpu.VMEM((1, LANES), F32)],
    )
    out = pl.pallas_call(
        functools.partial(_sb_sample_kernel, pp=pp, n_heads=n_heads, n_q=n_q, scale=HEAD_DIM ** -0.5),
        grid_spec=grid_spec,
        out_shape=jax.ShapeDtypeStruct((Bs, n_heads, n_q, HEAD_DIM), F32),
        compiler_params=_cparams(("parallel", "arbitrary")),
    )(page_table, jnp.full((1,), layer, jnp.int32), qbd, bias_row, k_new, v_new,
      *([cache_k] * pp), *([cache_v] * pp))
    return out.transpose(0, 2, 1, 3).reshape(Bs, n_q, HD)


def _cumsum_rows(x, tril_bf16):
    C = x.shape[0]
    if C < 16:
        rows = [x[0:1]]
        for i in range(1, C):
            rows.append(rows[-1] + x[i:i + 1])
        return jnp.concatenate(rows, axis=0)
    h1 = x.astype(BF16)
    r1 = x - h1.astype(F32)
    h2 = r1.astype(BF16)
    h3 = (r1 - h2.astype(F32)).astype(BF16)
    dot = lambda h: jnp.dot(tril_bf16, h, preferred_element_type=F32)
    return dot(h1) + dot(h2) + dot(h3)


def _hgrn_kernel(z_ref, lb_ref, ng_ref, s0_ref, o_ref, sT_ref, st_ref, *, C, SB, n_chunks):
    ci = pl.program_id(1)
    W = lb_ref.shape[1]
    n_heads = W // HEAD_DIM
    n_sub = C // SB

    @pl.when(ci == 0)
    def _():
        st_ref[...] = s0_ref[0]

    ones_blk = _head_block_ones(W)
    blockmask = (lax.broadcasted_iota(jnp.int32, (W, W), 0) // HEAD_DIM
                 == lax.broadcasted_iota(jnp.int32, (W, W), 1) // HEAD_DIM)
    lane_head = lax.broadcasted_iota(jnp.int32, (SB, W), 1) // HEAD_DIM
    t_sub = lax.broadcasted_iota(jnp.int32, (SB, W), 0)
    tril = jnp.where(lax.broadcasted_iota(jnp.int32, (C, C), 0)
                     >= lax.broadcasted_iota(jnp.int32, (C, C), 1), 1.0, 0.0).astype(BF16)
    s_col = lax.broadcasted_iota(jnp.int32, (n_heads * SB, C), 1)
    lb = lb_ref[...]
    ng = ng_ref[...]

    def chunk(cc, carry):
        r0 = pl.multiple_of(cc * C, C)
        zc = z_ref[0, pl.ds(r0, C), :]
        q, fz, v, gate = zc[:, :W], zc[:, W:2 * W], zc[:, 2 * W:3 * W], zc[:, 3 * W:]
        f = lb + (1.0 - lb) * _sigmoid(fz)
        k = 1.0 - f
        b = _cumsum_rows(jnp.log(f), tril)
        st = st_ref[...]
        o_inter = _bdot_t(q * jnp.exp(b), st)
        outs = []
        for I in range(n_sub):
            lo_, hi_ = I * SB, (I + 1) * SB
            bI, qI, kI, vI = b[lo_:hi_], q[lo_:hi_], k[lo_:hi_], v[lo_:hi_]
            ds = []
            for s in range(SB):
                e = jnp.exp(jnp.minimum(bI - bI[s:s + 1], 0.0))
                ds.append(jnp.where(t_sub >= s, e * qI * kI[s:s + 1], 0.0))
            G = jnp.dot(jnp.concatenate(ds, axis=0).astype(BF16), ones_blk,
                        preferred_element_type=F32)
            od = G[0:SB] * vI[0:1]
            for s in range(1, SB):
                od = od + G[s * SB:(s + 1) * SB] * vI[s:s + 1]
            if I > 0:
                rho = b[lo_ - 1:lo_]
                qs = qI * jnp.exp(bI - rho)
                kt = k * jnp.exp(jnp.minimum(rho - b, 0.0))
                qst = jnp.concatenate([jnp.where(lane_head == h, qs, 0.0) for h in range(n_heads)],
                                      axis=0)
                att = jnp.where(s_col < lo_, _bdot_t(qst, kt), 0.0)
                R = _bdot(att, v)
                for h in range(n_heads):
                    od = od + jnp.where(lane_head == h, R[h * SB:(h + 1) * SB], 0.0)
            outs.append(od)
        o = o_inter + (outs[0] if n_sub == 1 else jnp.concatenate(outs, axis=0))
        blast = b[C - 1:C]
        kd = k * jnp.exp(blast - b)
        upd = lax.dot_general(v.astype(BF16), kd.astype(BF16), (((0,), (0,)), ((), ())),
                              preferred_element_type=F32)
        st_ref[...] = st * jnp.exp(blast) + jnp.where(blockmask, upd, 0.0)
        ms = _dot2(o * o, ones_blk) * (1.0 / HEAD_DIM)
        o_ref[0, pl.ds(r0, C), :] = o * lax.rsqrt(ms + RMS_EPS) * ng * _silu(gate)
        return carry

    lax.fori_loop(0, n_chunks, chunk, 0)

    @pl.when(ci == pl.num_programs(1) - 1)
    def _():
        sT_ref[0] = st_ref[...]


def hgrn2(zb, lb, norm_g, s0, *, rows_per_step=256):
    B, L, W4 = zb.shape
    W = W4 // 4
    H = W // HEAD_DIM
    C = math.gcd(L, 64)
    SB = min(16, C)
    rows = min(rows_per_step, L)
    eye = jnp.eye(H, dtype=F32)
    st0 = jnp.einsum('bhdv,hg->bhvgd', s0, eye).reshape(B, W, W)
    o, sT = pl.pallas_call(
        functools.partial(_hgrn_kernel, C=C, SB=SB, n_chunks=rows // C),
        grid=(B, L // rows),
        in_specs=[pl.BlockSpec((1, rows, W4), lambda b, i: (b, i, 0)),
                  _const_spec((1, W)), _const_spec((1, W)),
                  pl.BlockSpec((1, W, W), lambda b, i: (b, 0, 0))],
        out_specs=[pl.BlockSpec((1, rows, W), lambda b, i: (b, i, 0)),
                   pl.BlockSpec((1, W, W), lambda b, i: (b, 0, 0))],
        out_shape=[jax.ShapeDtypeStruct((B, L, W), F32), jax.ShapeDtypeStruct((B, W, W), F32)],
        scratch_shapes=[pltpu.VMEM((W, W), F32)],
        compiler_params=_cparams(("parallel", "arbitrary")),
    )(zb, lb.reshape(1, W), norm_g.reshape(1, W), st0)
    s5 = sT.reshape(B, H, HEAD_DIM, H, HEAD_DIM)
    s_new = jnp.einsum('bhvgd,hg->bhdv', s5, eye)
    return o, s_new


def _rwkv_pre_kernel(z_ref, sh0_ref, mu_ref, w0_ref, wup_ref, a0_ref, aup_ref, gup_ref,
                     kkw_ref, kaw_ref, rk_ref, *refs, scan_layout):
    if scan_layout:
        src_o, v_o, bonus_o, g_o, shift_o, prev_ref = refs
    else:
        kk_o, w_o, ka_o, k2_o, r_o, v_o, bonus_o, g_o, shift_o, prev_ref = refs
    ti = pl.program_id(1)
    z = z_ref[0]
    tm, P = z.shape
    W = w0_ref.shape[1]

    @pl.when(ti == 0)
    def _():
        prev_ref[...] = sh0_ref[0]

    row = lax.broadcasted_iota(jnp.int32, (tm, P), 0)
    prev = jnp.where(row == 0, prev_ref[...], pltpu.roll(z, 1, axis=0))
    last = z[tm - 1:tm]
    prev_ref[...] = last
    shift_o[0] = last
    zs = z + (prev - z) * mu_ref[...]
    r, k, v, x4 = zs[:, :W], zs[:, W:2 * W], zs[:, 2 * W:3 * W], zs[:, 3 * W:]
    ones_blk = _head_block_ones(W)
    u = w0_ref[...] + _bdot(jnp.tanh(x4), wup_ref[...])
    w = jnp.exp(-jnp.exp(-_softplus(-u) - 0.5))
    a = _sigmoid(a0_ref[...] + _bdot(x4, aup_ref[...]))
    kkr = k * kkw_ref[...]
    kk = kkr / jnp.maximum(jnp.sqrt(_dot2(kkr * kkr, ones_blk)), 1e-12)
    k2 = k * (1.0 + (a - 1.0) * kaw_ref[...])
    scan_ops = (kk, w, kk * a, k2, r)
    if scan_layout:
        low = lax.broadcasted_iota(jnp.int32, (HEAD_DIM, LANES), 1) < HEAD_DIM
        for qi, xq in enumerate(scan_ops):
            for hp in range(W // LANES):
                for c2 in range(tm // LANES):
                    t_ = xq[c2 * LANES:(c2 + 1) * LANES, hp * LANES:(hp + 1) * LANES].T
                    top, bot = t_[:HEAD_DIM], t_[HEAD_DIM:]
                    src_o[hp, 2 * c2, qi] = jnp.where(low, top, pltpu.roll(bot, HEAD_DIM, axis=1))
                    src_o[hp, 2 * c2 + 1, qi] = jnp.where(low, pltpu.roll(top, HEAD_DIM, axis=1), bot)
    else:
        for o_ref, xq in zip((kk_o, w_o, ka_o, k2_o, r_o), scan_ops):
            o_ref[0] = xq
    for hp in range(W // LANES):
        v_o[0, hp] = v[:, hp * LANES:(hp + 1) * LANES]
    bonus_o[0] = _dot2(r * k2 * rk_ref[...], ones_blk) * v
    g_o[0] = _bdot(_sigmoid(x4), gup_ref[...])


def _rwkv_scan_kernel(src_ref, v_ref, s0_ref, o_ref, sT_ref, st_ref, lhs_ref, *, P, steps):
    c = pl.program_id(1)

    @pl.when(c == 0)
    def _():
        st_ref[...] = s0_ref[...]

    n_op = src_ref.shape[2]
    for p in range(P):
        x = src_ref[p, 0].reshape(n_op * HEAD_DIM, LANES)
        hi, lo = _split2(x)
        lhs_ref[p] = jnp.concatenate([hi, lo], axis=1)

    rr = lax.broadcasted_iota(jnp.int32, (2 * LANES, 2 * LANES), 0)
    cc = lax.broadcasted_iota(jnp.int32, (2 * LANES, 2 * LANES), 1)
    t_of_row = jnp.where((rr // HEAD_DIM) % 2 == (cc // HEAD_DIM) % 2,
                         rr % HEAD_DIM - cc // LANES, -1)

    def step2(i, carry):
        t0 = 2 * i
        sel = jnp.where(t_of_row == t0, 1.0, 0.0).astype(BF16)
        for p in range(P):
            cb = jnp.dot(lhs_ref[p], sel, preferred_element_type=F32)
            s = st_ref[p]
            for u in range(2):
                kk, w, ka, k2, r = (cb[i_ * HEAD_DIM:(i_ + 1) * HEAD_DIM, u * LANES:(u + 1) * LANES]
                                    for i_ in range(5))
                skk = jnp.sum(s * kk, axis=0, keepdims=True)
                s = s * w - ka * skk + k2 * v_ref[p, pl.ds(t0 + u, 1), :]
                o_ref[p, pl.ds(t0 + u, 1), :] = jnp.sum(s * r, axis=0, keepdims=True)
            st_ref[p] = s
        return carry

    lax.fori_loop(0, steps // 2, step2, 0)

    @pl.when(c == pl.num_programs(1) - 1)
    def _():
        sT_ref[...] = st_ref[...]


def _rwkv_post_kernel(o_ref, bonus_ref, g_ref, gng_ref, gnb_ref, out_ref):
    o = jnp.concatenate([o_ref[0, hp] for hp in range(o_ref.shape[1])], axis=1)
    ones_blk = _head_block_ones(o.shape[1])
    inv = 1.0 / HEAD_DIM
    d = o - _dot2(o, ones_blk) * inv
    var = _dot2(d * d, ones_blk) * inv
    out_ref[0] = (d * lax.rsqrt(var + GN_EPS) * gng_ref[...] + gnb_ref[...] + bonus_ref[0]) * g_ref[0]


def rwkv7(za, shift0, s0, p, *, tm=512, pairs_per_step=16):
    B, L, P = za.shape
    W = p['rwkv_w0'].shape[0]
    H = W // HEAD_DIM
    HP = W // LANES
    tm = min(tm, L)
    n_low = P - 3 * W
    rank_w, rank_a = p['rwkv_w_up'].shape[0], p['rwkv_a_up'].shape[0]
    pad_rows = lambda m, r0: jnp.zeros((n_low, W), F32).at[r0:r0 + m.shape[0]].set(m).astype(BF16)
    wup = pad_rows(p['rwkv_w_up'], 0)
    aup = pad_rows(p['rwkv_a_up'], rank_w)
    gup = pad_rows(p['rwkv_g_up'], rank_w + rank_a)
    vec = lambda a: a.reshape(1, -1)
    tok = pl.BlockSpec((1, tm, W), lambda b, i: (b, i, 0))
    tok_sd = jax.ShapeDtypeStruct((B, L, W), F32)
    pair_spec = pl.BlockSpec((1, HP, tm, LANES), lambda b, i: (b, 0, i, 0))
    pair_sd = jax.ShapeDtypeStruct((B, HP, L, LANES), F32)
    steps = min(HEAD_DIM, L)
    assert steps % 2 == 0
    NC = L // steps
    NP = B * HP
    scan_layout = tm % LANES == 0
    if scan_layout:
        ops_specs = [pl.BlockSpec((HP, tm // HEAD_DIM, 5, HEAD_DIM, LANES),
                                  lambda b, i: (b, i, 0, 0, 0))]
        ops_sds = [jax.ShapeDtypeStruct((NP, NC, 5, HEAD_DIM, LANES), F32)]
    else:
        ops_specs, ops_sds = [tok] * 5, [tok_sd] * 5
    *ops, v, bonus, g, shift = pl.pallas_call(
        functools.partial(_rwkv_pre_kernel, scan_layout=scan_layout),
        grid=(B, L // tm),
        in_specs=[pl.BlockSpec((1, tm, P), lambda b, i: (b, i, 0)),
                  pl.BlockSpec((1, 1, P), lambda b, i: (b, 0, 0)),
                  _const_spec((1, P)), _const_spec((1, W)), _const_spec((n_low, W)),
                  _const_spec((1, W)), _const_spec((n_low, W)), _const_spec((n_low, W)),
                  _const_spec((1, W)), _const_spec((1, W)), _const_spec((1, W))],
        out_specs=ops_specs + [pair_spec, tok, tok, pl.BlockSpec((1, 1, P), lambda b, i: (b, 0, 0))],
        out_shape=ops_sds + [pair_sd, tok_sd, tok_sd, jax.ShapeDtypeStruct((B, 1, P), F32)],
        scratch_shapes=[pltpu.VMEM((1, P), F32)],
        compiler_params=_cparams(("parallel", "arbitrary")),
    )(za, shift0.reshape(B, 1, P), vec(p['rwkv_mu']), vec(p['rwkv_w0']), wup, vec(p['rwkv_a0']),
      aup, gup, vec(p['rwkv_k_k']), vec(p['rwkv_k_a']), vec(p['rwkv_r_k']))

    if scan_layout:
        src, = ops
    else:
        src = jnp.stack(ops)
        src = src.reshape(5, B, NC, steps, HP, 2, HEAD_DIM).transpose(1, 4, 2, 0, 6, 5, 3)
        src = jnp.pad(src, ((0, 0),) * 6 + ((0, HEAD_DIM - steps),))
        src = src.reshape(NP, NC, 5, HEAD_DIM, LANES)
    st0 = s0.reshape(B, HP, 2, HEAD_DIM, HEAD_DIM).transpose(0, 1, 4, 2, 3).reshape(NP, HEAD_DIM, LANES)
    PP = min(pairs_per_step, NP)
    o, sT = pl.pallas_call(
        functools.partial(_rwkv_scan_kernel, P=PP, steps=steps),
        grid=(NP // PP, NC),
        in_specs=[pl.BlockSpec((PP, 1, 5, HEAD_DIM, LANES), lambda g_, c: (g_, c, 0, 0, 0)),
                  pl.BlockSpec((PP, steps, LANES), lambda g_, c: (g_, c, 0)),
                  pl.BlockSpec((PP, HEAD_DIM, LANES), lambda g_, c: (g_, 0, 0))],
        out_specs=[pl.BlockSpec((PP, steps, LANES), lambda g_, c: (g_, c, 0)),
                   pl.BlockSpec((PP, HEAD_DIM, LANES), lambda g_, c: (g_, 0, 0))],
        out_shape=[jax.ShapeDtypeStruct((NP, L, LANES), F32),
                   jax.ShapeDtypeStruct((NP, HEAD_DIM, LANES), F32)],
        scratch_shapes=[pltpu.VMEM((PP, HEAD_DIM, LANES), F32),
                        pltpu.VMEM((PP, 5 * HEAD_DIM, 2 * LANES), BF16)],
        compiler_params=_cparams(("parallel", "arbitrary")),
    )(src, v.reshape(NP, L, LANES), st0)
    s_new = sT.reshape(B, HP, HEAD_DIM, 2, HEAD_DIM).transpose(0, 1, 3, 4, 2).reshape(B, H, HEAD_DIM, HEAD_DIM)

    out = pl.pallas_call(
        _rwkv_post_kernel,
        grid=(B, L // tm),
        in_specs=[pair_spec, tok, tok, _const_spec((1, W)), _const_spec((1, W))],
        out_specs=tok,
        out_shape=tok_sd,
        compiler_params=_cparams(("parallel", "parallel")),
    )(o.reshape(B, HP, L, LANES), bonus, g, vec(p['rwkv_gn_g']), vec(p['rwkv_gn_b']))
    return out, s_new, shift.reshape(B, P)


def _route_kernel(x_ref, rt_ref, g_ref, s_ref):
    logits = lax.dot_general(rt_ref[...], x_ref[...], (((1,), (1,)), ((), ())),
                             precision=lax.Precision.HIGHEST, preferred_element_type=F32)
    n_e = logits.shape[0]
    e_id = lax.broadcasted_iota(jnp.int32, logits.shape, 0)
    m1 = jnp.max(logits, axis=0, keepdims=True)
    i1 = jnp.min(jnp.where(logits == m1, e_id, n_e), axis=0, keepdims=True)
    rest = jnp.where(e_id == i1, -jnp.inf, logits)
    m2 = jnp.max(rest, axis=0, keepdims=True)
    i2 = jnp.min(jnp.where(rest == m2, e_id, n_e), axis=0, keepdims=True)
    t = jnp.exp(m2 - m1)
    g1 = 1.0 / (1.0 + t)
    g2 = t / (1.0 + t)
    g_ref[...] = jnp.where(e_id == i1, g1, jnp.where(e_id == i2, g2, 0.0))
    s_ref[...] = jnp.where(e_id == i1, 1.0, jnp.where(e_id == i2, 1.0, 0.0))


def _moe_kernel(xb_ref, g_ref, s_ref, wg_ref, wu_ref, wd_ref, o_ref, rank_ref, xg_ref, y_ref, *, R):
    e = pl.program_id(1)
    f = pl.program_id(2)
    n_f = pl.num_programs(2)
    tm = xb_ref.shape[0]

    @pl.when((e == 0) & (f == 0))
    def _():
        before = jnp.where(lax.broadcasted_iota(jnp.int32, (tm, tm), 0)
                           < lax.broadcasted_iota(jnp.int32, (tm, tm), 1), 1.0, 0.0).astype(BF16)
        rank_ref[...] = jnp.dot(s_ref[...].astype(BF16), before, preferred_element_type=F32)
        o_ref[...] = jnp.zeros(o_ref.shape, F32)

    sel = s_ref[pl.ds(e, 1), :]
    key = jnp.where(sel > 0.0, rank_ref[pl.ds(e, 1), :], -1.0)
    n_chunks = (jnp.sum(sel).astype(jnp.int32) + (R - 1)) // R
    r_id = lax.broadcasted_iota(jnp.int32, (R, tm), 0)

    def one_hot(c):
        return jnp.where(key == (r_id + c * R).astype(F32), 1.0, 0.0).astype(BF16)

    def rows_of(c):
        return pl.ds(pl.multiple_of(c * R, math.gcd(R, 256)), R)

    @pl.when(f == 0)
    def _():
        def gather(c, carry):
            xg_ref[rows_of(c), :] = jnp.dot(one_hot(c), xb_ref[...],
                                            preferred_element_type=F32).astype(BF16)
            return carry
        lax.fori_loop(0, n_chunks, gather, 0)

    def expert(c, carry):
        xg = xg_ref[rows_of(c), :]
        gate = jnp.dot(xg, wg_ref[...], preferred_element_type=F32)
        up = jnp.dot(xg, wu_ref[...], preferred_element_type=F32)
        y = _bdot(_silu(gate) * up, wd_ref[...])

        @pl.when(f == 0)
        def _():
            y_ref[rows_of(c), :] = y

        @pl.when(f > 0)
        def _():
            y_ref[rows_of(c), :] += y
        return carry

    lax.fori_loop(0, n_chunks, expert, 0)

    @pl.when(f == n_f - 1)
    def _():
        gt = g_ref[pl.ds(e, 1), :]
        h1 = gt.astype(BF16)
        r1 = gt - h1.astype(F32)
        h2 = r1.astype(BF16)
        h3 = (r1 - h2.astype(F32)).astype(BF16)
        g3 = jnp.concatenate([h1, h2, h3, jnp.zeros((5, tm), BF16)], axis=0)

        def scatter(c, carry):
            p = one_hot(c)
            g_row = jnp.sum(lax.dot_general(p, g3, (((1,), (1,)), ((), ())),
                                            preferred_element_type=F32), axis=1, keepdims=True)
            yw = (y_ref[rows_of(c), :] * g_row).astype(BF16)
            o_ref[...] += lax.dot_general(p, yw, (((0,), (0,)), ((), ())),
                                          preferred_element_type=F32)
            return carry
        lax.fori_loop(0, n_chunks, scatter, 0)


def moe_ffn(x, router, wg_bf16, wu_bf16, wd_bf16, *, tm=1024, tf=896, R=288):
    T, D = x.shape
    n_e, _, F = wg_bf16.shape
    tm = min(tm, T)
    R = min(R, tm)
    gates, sel = pl.pallas_call(
        _route_kernel,
        grid=(T // tm,),
        in_specs=[pl.BlockSpec((tm, D), lambda i: (i, 0)), _const_spec((n_e, D))],
        out_specs=[pl.BlockSpec((n_e, tm), lambda i: (0, i))] * 2,
        out_shape=[jax.ShapeDtypeStruct((n_e, T), F32)] * 2,
        compiler_params=_cparams(("parallel",)),
    )(x, router.T)
    return pl.pallas_call(
        functools.partial(_moe_kernel, R=R),
        grid=(T // tm, n_e, F // tf),
        in_specs=[pl.BlockSpec((tm, D), lambda i, e, f: (i, 0)),
                  pl.BlockSpec((n_e, tm), lambda i, e, f: (0, i)),
                  pl.BlockSpec((n_e, tm), lambda i, e, f: (0, i)),
                  pl.BlockSpec((None, D, tf), lambda i, e, f: (e, 0, f)),
                  pl.BlockSpec((None, D, tf), lambda i, e, f: (e, 0, f)),
                  pl.BlockSpec((None, tf, D), lambda i, e, f: (e, f, 0))],
        out_specs=pl.BlockSpec((tm, D), lambda i, e, f: (i, 0)),
        out_shape=jax.ShapeDtypeStruct((T, D), F32),
        scratch_shapes=[pltpu.VMEM((n_e, tm), F32), pltpu.VMEM((pl.cdiv(tm, R) * R, D), BF16),
                        pltpu.VMEM((pl.cdiv(tm, R) * R, D), F32)],
        compiler_params=_cparams(("parallel", "arbitrary", "arbitrary")),
    )(x.astype(BF16), gates, sel, wg_bf16, wu_bf16, wd_bf16)


def _layer(x, seq_len, p, lb, states, sb_past, mem_k, mem_v, ffn, alpha, layer, depth, kv_stack):
    T, D = x.shape
    B = T // seq_len
    W = p['rwkv_w0'].shape[0]
    pa = p['rwkv_mu'].shape[0]
    wc = p['sb_bias'].shape[0] * HEAD_DIM
    x, za, zb, q, k, v, k4, v4 = in_proj(x, p['ln_in_g'], p['ln_in_b'], p['w_in'], kv_stack, layer,
                                         depth, apply_ln=layer == 0, widths=(pa, 4 * W, wc, wc, wc))
    to3 = lambda a: a.reshape(B, seq_len, a.shape[-1])
    o_a, rwkv_s, shift = rwkv7(to3(za), states[1], states[0], p)
    o_b, hgrn_s = hgrn2(to3(zb), lb, p['hgrn_norm_g'], states[2])
    n_heads = p['sb_bias'].shape[0]
    if sb_past is None:
        o_c = sb_prompt(to3(q), to3(k), to3(v), p['sb_bias'], n_heads=n_heads)
    else:
        o_c = sb_sample(to3(q), to3(k), to3(v), p['sb_bias'], *sb_past, n_heads=n_heads)
    x = mix_out(o_a.reshape(T, W), o_b.reshape(T, W), o_c.reshape(T, wc), p['w_out'], x,
                p['ln_mix_g'], p['ln_mix_b'], alpha=alpha)
    x = mem_block(x, p['mem_wq'], mem_k, mem_v, p['mem_wo'], p['ln_mem_g'], p['ln_mem_b'],
                  alpha=alpha, seq_len=seq_len)
    x = ffn(x)
    return x, (rwkv_s, shift, hgrn_s, (k4, v4))


def kernel(x_prompt, x_sample, cache_sb_k, cache_sb_v, state_rwkv, state_rwkv_shift, state_hgrn,
           cache_mem_k, cache_mem_v, page_table, mem_prompt, ln_in_g, ln_in_b, w_in, rwkv_mu, rwkv_w0,
           rwkv_w_up, rwkv_a0, rwkv_a_up, rwkv_g_up, rwkv_k_k, rwkv_k_a, rwkv_r_k, rwkv_gn_g, rwkv_gn_b,
           hgrn_lb, hgrn_norm_g, sb_bias, w_out, ln_mix_g, ln_mix_b, mem_wq, mem_wk, mem_wv, mem_wo,
           ln_mem_g, ln_mem_b, ffn_w_gate, ffn_w_up, ffn_w_down, moe_router, moe_w_gate, moe_w_up,
           moe_w_down, ln_ffn_g, ln_ffn_b):
    B, L, D = x_prompt.shape
    Bs, Ls, _ = x_sample.shape
    depth = w_in.shape[0]
    H = state_rwkv.shape[2]
    n_heads_c = sb_bias.shape[1]
    n_mem = mem_prompt.shape[1]
    alpha = (2 * depth) ** 0.25
    bf = lambda a: a.astype(BF16)

    lb_sm = jax.nn.softmax(hgrn_lb.astype(F32), axis=0)
    lb_all = jnp.cumsum(lb_sm, axis=0) - lb_sm[0]

    mem_flat = mem_prompt.reshape(B * n_mem, D)
    zeros_p = (jnp.zeros((B, H, HEAD_DIM, HEAD_DIM), F32), jnp.zeros((B, rwkv_mu.shape[1]), F32),
               jnp.zeros((B, H, HEAD_DIM, HEAD_DIM), F32))

    xp = x_prompt.reshape(B * L, D)
    xs = x_sample.reshape(Bs * Ls, D)
    outs_p = [[] for _ in range(5)]
    outs_s = [[] for _ in range(3)]
    kv_p = kv_s = None
    for l in range(depth):
        p = dict(ln_in_g=ln_in_g, ln_in_b=ln_in_b, w_in=bf(w_in[l]), w_out=bf(w_out[l]),
                 rwkv_mu=rwkv_mu[l], rwkv_w0=rwkv_w0[l], rwkv_w_up=rwkv_w_up[l], rwkv_a0=rwkv_a0[l],
                 rwkv_a_up=rwkv_a_up[l], rwkv_g_up=rwkv_g_up[l], rwkv_k_k=rwkv_k_k[l],
                 rwkv_k_a=rwkv_k_a[l], rwkv_r_k=rwkv_r_k[l], rwkv_gn_g=rwkv_gn_g[l],
                 rwkv_gn_b=rwkv_gn_b[l], hgrn_norm_g=hgrn_norm_g[l], sb_bias=sb_bias[l],
                 ln_mix_g=ln_mix_g[l], ln_mix_b=ln_mix_b[l], mem_wq=bf(mem_wq[l]),
                 mem_wo=bf(mem_wo[l]), ln_mem_g=ln_mem_g[l], ln_mem_b=ln_mem_b[l])
        j = l // 2
        if l % 2 == 0:
            wg, wu, wd = bf(ffn_w_gate[j]), bf(ffn_w_up[j]), bf(ffn_w_down[j])
            ffn = lambda x, wg=wg, wu=wu, wd=wd, l=l: ffn_block(
                x, wg, wu, wd, ln_ffn_g[l], ln_ffn_b[l], alpha=alpha)
        else:
            wg, wu, wd = bf(moe_w_gate[j]), bf(moe_w_up[j]), bf(moe_w_down[j])
            ffn = lambda x, wg=wg, wu=wu, wd=wd, j=j, l=l: res_ln(
                x, moe_ffn(x, moe_router[j], wg, wu, wd), ln_ffn_g[l], ln_ffn_b[l], alpha=alpha)
        mk, mv = matmul2(mem_flat, bf(mem_wk[l]), bf(mem_wv[l]))
        mk = mk.reshape(B, n_mem, D)
        mv = mv.reshape(B, n_mem, D)
        xp, (s_a, sh, s_b, kv_p) = _layer(xp, L, p, lb_all[l], zeros_p, None, mk, mv, ffn,
                                          alpha, l, depth, kv_p)
        for lst, val in zip(outs_p, (s_a, sh, s_b,
                                     mk.reshape(B, n_mem, N_MEM_HEADS, D // N_MEM_HEADS),
                                     mv.reshape(B, n_mem, N_MEM_HEADS, D // N_MEM_HEADS))):
            lst.append(val)
        states = (state_rwkv[l], state_rwkv_shift[l], state_hgrn[l])
        xs, (s_a, sh, s_b, kv_s) = _layer(
            xs, Ls, p, lb_all[l], states, (cache_sb_k, cache_sb_v, page_table, l),
            cache_mem_k[l].reshape(Bs, n_mem, D), cache_mem_v[l].reshape(Bs, n_mem, D), ffn,
            alpha, l, depth, kv_s)
        for lst, val in zip(outs_s, (s_a, sh, s_b)):
            lst.append(val)
    kv5 = lambda a, nb, sl: a.reshape(depth, nb, sl, n_heads_c, HEAD_DIM)
    return (xp.reshape(B, L, D), xs.reshape(Bs, Ls, D),
            kv5(kv_p[0], B, L), kv5(kv_p[1], B, L), *(jnp.stack(o) for o in outs_p),
            kv5(kv_s[0], Bs, Ls), kv5(kv_s[1], Bs, Ls), *(jnp.stack(o) for o in outs_s))
```

```python
import functools
import math

import jax
import jax.numpy as jnp
from jax import lax
from jax.experimental import pallas as pl
from jax.experimental.pallas import tpu as pltpu

F32 = jnp.float32
BF16 = jnp.bfloat16

HEAD_DIM = 64
LANES = 128
PAGE_SIZE = 128
N_MEM_HEADS = 4
TOP_K = 2
LN_EPS = 1e-5
GN_EPS = 64e-5
RMS_EPS = 1e-6
VMEM_LIMIT = 56 * 1024 * 1024


def _cparams(sem):
    return pltpu.CompilerParams(dimension_semantics=sem, vmem_limit_bytes=VMEM_LIMIT)


def _const_spec(shape):
    nd = len(shape)
    return pl.BlockSpec(shape, lambda *_: (0,) * nd, pipeline_mode=pl.Buffered(1))


def _bdot(a, b):
    return jnp.dot(a.astype(BF16), b.astype(BF16), preferred_element_type=F32)


def _bdot_t(a, b):
    return lax.dot_general(a.astype(BF16), b.astype(BF16), (((1,), (1,)), ((), ())),
                           preferred_element_type=F32)


def _split2(x):
    hi = x.astype(BF16)
    lo = (x - hi.astype(F32)).astype(BF16)
    return hi, lo


def _dot2(x, w_bf16):
    hi, lo = _split2(x)
    return (jnp.dot(hi, w_bf16, preferred_element_type=F32)
            + jnp.dot(lo, w_bf16, preferred_element_type=F32))


def _layer_norm(x, g, b):
    mu = jnp.mean(x, -1, keepdims=True)
    xc = x - mu
    var = jnp.mean(xc * xc, -1, keepdims=True)
    return xc * lax.rsqrt(var + LN_EPS) * g + b


def _sigmoid(x):
    return 1.0 / (1.0 + jnp.exp(-x))


def _silu(x):
    return x * _sigmoid(x)


def _softplus(x):
    return jnp.maximum(x, 0.0) + jnp.log(1.0 + jnp.exp(-jnp.abs(x)))


def _head_block_ones(width):
    r = lax.broadcasted_iota(jnp.int32, (width, width), 0) // HEAD_DIM
    c = lax.broadcasted_iota(jnp.int32, (width, width), 1) // HEAD_DIM
    return jnp.where(r == c, 1.0, 0.0).astype(BF16)


def _in_proj_kernel(x_ref, g_ref, b_ref, w_ref, *refs, apply_ln, n_alias, n_heads):
    xn_ref, za_ref, zb_ref, q_ref, k_ref, v_ref, k4_ref, v4_ref = refs[n_alias:]
    x = x_ref[...]
    if apply_ln:
        x = _layer_norm(x, g_ref[...], b_ref[...])
    xn_ref[...] = x
    z = _bdot(x, w_ref[...])
    c0 = 0
    for z_ref in (za_ref, zb_ref, q_ref, k_ref, v_ref):
        z_ref[...] = z[:, c0:c0 + z_ref.shape[1]]
        c0 += z_ref.shape[1]
    tm = x.shape[0]
    hd = k_ref.shape[1]
    for src0, dst in ((c0 - 2 * hd, k4_ref), (c0 - hd, v4_ref)):
        for h in range(n_heads):
            dst[pl.ds(h, tm, stride=n_heads), :] = z[:, src0 + h * HEAD_DIM:src0 + (h + 1) * HEAD_DIM]


def in_proj(x, g, b, w_bf16, kv_stack, layer, depth, *, apply_ln, widths, tm=512):
    T, D = x.shape
    N = w_bf16.shape[1]
    assert sum(widths) == N and all(wd % LANES == 0 for wd in widths)
    n_heads = widths[-1] // HEAD_DIM
    tm = min(tm, T)
    nt = T // tm
    row = lambda width: pl.BlockSpec((tm, width), lambda i: (i, 0))
    stack_spec = pl.BlockSpec((tm * n_heads, HEAD_DIM), lambda i: (layer * nt + i, 0))
    stack_sd = jax.ShapeDtypeStruct((depth * T * n_heads, HEAD_DIM), F32)
    n_alias = 0 if kv_stack is None else 2
    n_out = 6
    return pl.pallas_call(
        functools.partial(_in_proj_kernel, apply_ln=apply_ln, n_alias=n_alias, n_heads=n_heads),
        grid=(nt,),
        in_specs=[row(D), _const_spec((1, D)), _const_spec((1, D)), _const_spec((D, N))]
                 + [pl.BlockSpec(memory_space=pl.ANY)] * n_alias,
        out_specs=[row(D)] + [row(wd) for wd in widths] + [stack_spec, stack_spec],
        out_shape=[jax.ShapeDtypeStruct((T, D), F32)]
                  + [jax.ShapeDtypeStruct((T, wd), F32) for wd in widths] + [stack_sd, stack_sd],
        input_output_aliases={4 + a: n_out + a for a in range(n_alias)},
        compiler_params=_cparams(("parallel",)),
    )(x, g.reshape(1, D), b.reshape(1, D), w_bf16, *(kv_stack or ()))


def _matmul2_kernel(x_ref, w1_ref, w2_ref, o1_ref, o2_ref):
    xb = x_ref[...].astype(BF16)
    o1_ref[...] = jnp.dot(xb, w1_ref[...], preferred_element_type=F32)
    o2_ref[...] = jnp.dot(xb, w2_ref[...], preferred_element_type=F32)


def matmul2(x, w1_bf16, w2_bf16, tm=512):
    T, K = x.shape
    N = w1_bf16.shape[1]
    tm = min(tm, T)
    out = pl.BlockSpec((tm, N), lambda i: (i, 0))
    return pl.pallas_call(
        _matmul2_kernel,
        grid=(T // tm,),
        in_specs=[pl.BlockSpec((tm, K), lambda i: (i, 0)), _const_spec((K, N)), _const_spec((K, N))],
        out_specs=[out, out],
        out_shape=[jax.ShapeDtypeStruct((T, N), F32)] * 2,
        compiler_params=_cparams(("parallel",)),
    )(x, w1_bf16, w2_bf16)


def _mix_out_kernel(oa_ref, ob_ref, oc_ref, wa_ref, wb_ref, wc_ref, res_ref, g_ref, b_ref, o_ref,
                    *, alpha):
    h = (_bdot(oa_ref[...], wa_ref[...]) + _bdot(ob_ref[...], wb_ref[...])
         + _bdot(oc_ref[...], wc_ref[...]))
    o_ref[...] = _layer_norm(alpha * res_ref[...] + h, g_ref[...], b_ref[...])


def mix_out(oa, ob, oc, w_out_bf16, res, g, b, *, alpha, tm=512):
    T, D = res.shape
    wa, wb, wc = oa.shape[1], ob.shape[1], oc.shape[1]
    tm = min(tm, T)
    row = lambda width: pl.BlockSpec((tm, width), lambda i: (i, 0))
    return pl.pallas_call(
        functools.partial(_mix_out_kernel, alpha=alpha),
        grid=(T // tm,),
        in_specs=[row(wa), row(wb), row(wc),
                  _const_spec((wa, D)), _const_spec((wb, D)), _const_spec((wc, D)),
                  row(D), _const_spec((1, D)), _const_spec((1, D))],
        out_specs=row(D),
        out_shape=jax.ShapeDtypeStruct((T, D), F32),
        compiler_params=_cparams(("parallel",)),
    )(oa, ob, oc, w_out_bf16[:wa], w_out_bf16[wa:wa + wb], w_out_bf16[wa + wb:], res,
      g.reshape(1, D), b.reshape(1, D))


def _res_ln_kernel(x_ref, y_ref, g_ref, b_ref, o_ref, *, alpha):
    o_ref[...] = _layer_norm(alpha * x_ref[...] + y_ref[...], g_ref[...], b_ref[...])


def res_ln(x, y, g, b, *, alpha, tm=1024):
    T, D = x.shape
    tm = min(tm, T)
    row = pl.BlockSpec((tm, D), lambda i: (i, 0))
    return pl.pallas_call(
        functools.partial(_res_ln_kernel, alpha=alpha),
        grid=(T // tm,),
        in_specs=[row, row, _const_spec((1, D)), _const_spec((1, D))],
        out_specs=row,
        out_shape=jax.ShapeDtypeStruct((T, D), F32),
        compiler_params=_cparams(("parallel",)),
    )(x, y, g.reshape(1, D), b.reshape(1, D))


def _mem_kernel(x_ref, wq_ref, mk_ref, mv_ref, wo_ref, g_ref, b_ref, o_ref, att_ref,
                *, alpha, nb, rows):
    x = x_ref[...]
    q = _bdot(x, wq_ref[...])
    D = q.shape[1]
    dh = D // N_MEM_HEADS
    scale = dh ** -0.5
    for bi in range(nb):
        r0 = bi * rows
        for h in range(N_MEM_HEADS):
            c0 = h * dh
            qh = q[r0:r0 + rows, c0:c0 + dh]
            kh = mk_ref[bi, :, c0:c0 + dh]
            vh = mv_ref[bi, :, c0:c0 + dh]
            s = _bdot_t(qh, kh) * scale
            s = s - jnp.max(s, -1, keepdims=True)
            p = jnp.exp(s)
            p = p / jnp.sum(p, -1, keepdims=True)
            att_ref[r0:r0 + rows, c0:c0 + dh] = _bdot(p, vh)
    y = _bdot(att_ref[...], wo_ref[...])
    o_ref[...] = _layer_norm(alpha * x + y, g_ref[...], b_ref[...])


def mem_block(x, wq_bf16, mk, mv, wo_bf16, g, b, *, alpha, seq_len, tm=512):
    T, D = x.shape
    B, n_mem, _ = mk.shape
    if seq_len >= tm:
        nb, rows = 1, tm
        mem_map = lambda i: (i // (seq_len // tm), 0, 0)
    else:
        nb, rows = min(B, 8), seq_len
        tm = nb * rows
        mem_map = lambda i: (i, 0, 0)
    row = pl.BlockSpec((tm, D), lambda i: (i, 0))
    return pl.pallas_call(
        functools.partial(_mem_kernel, alpha=alpha, nb=nb, rows=rows),
        grid=(T // tm,),
        in_specs=[row, _const_spec((D, D)),
                  pl.BlockSpec((nb, n_mem, D), mem_map), pl.BlockSpec((nb, n_mem, D), mem_map),
                  _const_spec((D, D)), _const_spec((1, D)), _const_spec((1, D))],
        out_specs=row,
        out_shape=jax.ShapeDtypeStruct((T, D), F32),
        scratch_shapes=[pltpu.VMEM((tm, D), F32)],
        compiler_params=_cparams(("parallel",)),
    )(x, wq_bf16, mk, mv, wo_bf16, g.reshape(1, D), b.reshape(1, D))


def _ffn_kernel(x_ref, wg_ref, wu_ref, wd_ref, g_ref, b_ref, o_ref, *, alpha, tf):
    x = x_ref[...]
    xb = x.astype(BF16)
    F = wg_ref.shape[1]
    y = jnp.zeros(x.shape, F32)
    for f0 in range(0, F, tf):
        gate = jnp.dot(xb, wg_ref[:, f0:f0 + tf], preferred_element_type=F32)
        up = jnp.dot(xb, wu_ref[:, f0:f0 + tf], preferred_element_type=F32)
        y = y + _bdot(_silu(gate) * up, wd_ref[f0:f0 + tf, :])
    o_ref[...] = _layer_norm(alpha * x + y, g_ref[...], b_ref[...])


def ffn_block(x, wg_bf16, wu_bf16, wd_bf16, g, b, *, alpha, tm=512, tf=256):
    T, D = x.shape
    F = wg_bf16.shape[1]
    tm = min(tm, T)
    row = pl.BlockSpec((tm, D), lambda i: (i, 0))
    return pl.pallas_call(
        functools.partial(_ffn_kernel, alpha=alpha, tf=tf),
        grid=(T // tm,),
        in_specs=[row, _const_spec((D, F)), _const_spec((D, F)), _const_spec((F, D)),
                  _const_spec((1, D)), _const_spec((1, D))],
        out_specs=row,
        out_shape=jax.ShapeDtypeStruct((T, D), F32),
        compiler_params=_cparams(("parallel",)),
    )(x, wg_bf16, wu_bf16, wd_bf16, g.reshape(1, D), b.reshape(1, D))


def _sb_prompt_kernel(bias_ref, q_ref, k_ref, v_ref, o_ref, acc_ref, c_ref, *, tq, tk, scale):
    hp = pl.program_id(1)
    qi = pl.program_id(2)
    n_sub = tq // tk
    q = q_ref[0] * scale
    half = lax.broadcasted_iota(jnp.int32, (tq, LANES), 1) // HEAD_DIM
    qm = [jnp.where(half == h2, q, 0.0).astype(BF16) for h2 in range(2)]
    bias = [bias_ref[2 * hp + h2] for h2 in range(2)]
    rr = lax.broadcasted_iota(jnp.int32, (tk, 2 * tk), 0)
    cc = lax.broadcasted_iota(jnp.int32, (tk, 2 * tk), 1)
    mw = jnp.where((cc >= tk) | (rr > cc), 1.0, 0.0).astype(BF16)
    causal = (lax.broadcasted_iota(jnp.int32, (tq, tq), 1)
              < lax.broadcasted_iota(jnp.int32, (tq, tq), 0))
    acc_ref[...] = jnp.zeros(acc_ref.shape, F32)
    c_ref[...] = jnp.zeros(c_ref.shape, F32)

    def block(j, masked):
        start = pl.multiple_of(j * tq, tq)
        kb = k_ref[0, pl.ds(start, tq), :].astype(BF16)
        vb = v_ref[0, pl.ds(start, tq), :].astype(BF16)
        for h2 in range(2):
            z = _bdot_t(qm[h2], kb) + bias[h2]
            sp = _softplus(z)
            spm = (jnp.where(causal, sp, 0.0) if masked else sp).astype(BF16)
            c = c_ref[h2]
            parts = [None] * n_sub
            for s in reversed(range(n_sub)):
                sl = slice(s * tk, (s + 1) * tk)
                r = jnp.dot(spm[:, sl], mw, preferred_element_type=F32)
                parts[s] = z[:, sl] - sp[:, sl] - r[:, :tk] - c
                c = c + r[:, tk:]
            c_ref[h2] = c
            a = jnp.exp(parts[0] if n_sub == 1 else jnp.concatenate(parts, axis=1))
            if masked:
                a = jnp.where(causal, a, 0.0)
            acc_ref[h2] += jnp.dot(a.astype(BF16), vb, preferred_element_type=F32)

    block(qi, True)

    def body(i, carry):
        block(qi - 1 - i, False)
        return carry

    lax.fori_loop(0, qi, body, 0)
    o_ref[0] = jnp.where(half == 0, acc_ref[0], acc_ref[1])


def sb_prompt(q, k, v, bias, *, n_heads, tq=512, tk=128):
    B, L, _ = q.shape
    tq = min(tq, L)
    tk = min(tk, tq)
    grid_spec = pltpu.PrefetchScalarGridSpec(
        num_scalar_prefetch=1,
        grid=(B, n_heads // 2, L // tq),
        in_specs=[pl.BlockSpec((1, tq, LANES), lambda b, hp, qi, bias: (b, qi, hp)),
                  pl.BlockSpec((1, L, LANES), lambda b, hp, qi, bias: (b, 0, hp)),
                  pl.BlockSpec((1, L, LANES), lambda b, hp, qi, bias: (b, 0, hp))],
        out_specs=pl.BlockSpec((1, tq, LANES), lambda b, hp, qi, bias: (b, qi, hp)),
        scratch_shapes=[pltpu.VMEM((2, tq, LANES), F32), pltpu.VMEM((2, tq, tk), F32)],
    )
    return pl.pallas_call(
        functools.partial(_sb_prompt_kernel, tq=tq, tk=tk, scale=HEAD_DIM ** -0.5),
        grid_spec=grid_spec,
        out_shape=jax.ShapeDtypeStruct((B, L, n_heads * HEAD_DIM), F32),
        compiler_params=_cparams(("parallel", "parallel", "arbitrary")),
    )(bias.astype(F32), q, k, v)


def _page_head_rows(page_ref, h, *, n_heads):
    rows = page_ref.reshape(PAGE_SIZE * n_heads, HEAD_DIM)
    return rows[pl.ds(h, PAGE_SIZE, stride=n_heads), :]


def _sb_sample_kernel(pt_ref, lay_ref, qbd_ref, bias_ref, kn_ref, vn_ref, *rest,
                      pp, n_heads, n_q, scale):
    k_refs, v_refs = rest[:pp], rest[pp:2 * pp]
    o_ref, acc_ref, c_ref = rest[2 * pp:]
    j = pl.program_id(1)
    bias = bias_ref[...]

    def weights(zt, mask, blk):
        sp = _softplus(zt)
        spm = sp if mask is None else jnp.where(mask, sp, 0.0)
        hi, lo = _split2(spm)
        rr = lax.broadcasted_iota(jnp.int32, (blk, 2 * blk), 0)
        cc = lax.broadcasted_iota(jnp.int32, (blk, 2 * blk), 1) % blk
        later = jnp.where(cc > rr, 1.0, 0.0).astype(BF16)
        c = c_ref[...]
        parts = []
        for i in range(zt.shape[0] // blk):
            sl = slice(i * blk, (i + 1) * blk)
            r = jnp.dot(later, jnp.concatenate([hi[sl], lo[sl]], axis=0),
                        preferred_element_type=F32)
            parts.append(zt[sl] - sp[sl] - r - c)
            c = c + r[0:1] + spm[i * blk:i * blk + 1]
        c_ref[...] = c
        at = jnp.exp(parts[0] if len(parts) == 1 else jnp.concatenate(parts, axis=0))
        return at if mask is None else jnp.where(mask, at, 0.0)

    @pl.when(j == 0)
    def _():
        c_ref[...] = jnp.zeros(c_ref.shape, F32)
        rows = 16
        pad = jnp.zeros((rows - n_q, kn_ref.shape[2]), F32)
        kn = jnp.concatenate([kn_ref[0], pad], axis=0)
        vn = jnp.concatenate([vn_ref[0], pad], axis=0)
        s = lax.broadcasted_iota(jnp.int32, (rows, LANES), 0)
        col = lax.broadcasted_iota(jnp.int32, (rows, LANES), 1)
        valid = (s < col % n_q) & (col < n_heads * n_q)
        zt = jnp.dot(kn.astype(BF16), qbd_ref[0], preferred_element_type=F32) * scale + bias
        at = weights(zt, valid, rows)
        res = lax.dot_general(at.astype(BF16), vn.astype(BF16), (((0,), (0,)), ((), ())),
                              preferred_element_type=F32)
        for h in range(n_heads):
            acc_ref[h] = res[h * n_q:(h + 1) * n_q, h * HEAD_DIM:(h + 1) * HEAD_DIM]

    head_rows = functools.partial(_page_head_rows, n_heads=n_heads)

    zt = None
    for h in range(n_heads):
        kh = jnp.concatenate([head_rows(kr, h) for kr in k_refs], axis=0).astype(BF16)
        part = jnp.dot(kh, qbd_ref[0, h * HEAD_DIM:(h + 1) * HEAD_DIM, :],
                       preferred_element_type=F32)
        zt = part if zt is None else zt + part
    a = weights(zt * scale + bias, None, PAGE_SIZE).T
    for h in range(n_heads):
        vh = jnp.concatenate([head_rows(vr, h) for vr in v_refs], axis=0).astype(BF16)
        acc_ref[h] += jnp.dot(a[h * n_q:(h + 1) * n_q, :].astype(BF16), vh,
                              preferred_element_type=F32)

    @pl.when(j == pl.num_programs(1) - 1)
    def _():
        o_ref[0] = acc_ref[...]


def sb_sample(q, k_new, v_new, bias, cache_k, cache_v, page_table, layer, *, n_heads, pp=16):
    Bs, n_q, HD = q.shape
    n_pages = page_table.shape[1]
    pp = min(pp, n_pages)
    eye = jnp.eye(n_heads, dtype=F32)
    qbd = jnp.einsum('bqhd,hg->bhdgq', q.reshape(Bs, n_q, n_heads, HEAD_DIM), eye)
    qbd = qbd.reshape(Bs, HD, n_heads * n_q)
    qbd = jnp.pad(qbd, ((0, 0), (0, 0), (0, LANES - n_heads * n_q))).astype(BF16)
    bias_row = jnp.pad(jnp.repeat(bias.astype(F32), n_q), (0, LANES - n_heads * n_q)).reshape(1, LANES)

    def page_spec(i):
        return pl.BlockSpec(
            (None, None, PAGE_SIZE, n_heads, HEAD_DIM),
            lambda b, j, pt, lay: (lay[0], pt[b, n_pages - 1 - (j * pp + i)], 0, 0, 0))

    new_spec = pl.BlockSpec((1, n_q, HD), lambda b, j, pt, lay: (b, 0, 0))
    grid_spec = pltpu.PrefetchScalarGridSpec(
        num_scalar_prefetch=2,
        grid=(Bs, n_pages // pp),
        in_specs=[pl.BlockSpec((1, HD, LANES), lambda b, j, pt, lay: (b, 0, 0)),
                  pl.BlockSpec((1, LANES), lambda b, j, pt, lay: (0, 0)),
                  new_spec, new_spec]
                 + [page_spec(i) for i in range(pp)] + [page_spec(i) for i in range(pp)],
        out_specs=pl.BlockSpec((1, n_heads, n_q, HEAD_DIM), lambda b, j, pt, lay: (b, 0, 0, 0)),
        scratch_shapes=[pltpu.VMEM((n_heads, n_q, HEAD_DIM), F32), pltpu.VMEM((1, LANES), F32)],
    )
    out = pl.pallas_call(
        functools.partial(_sb_sample_kernel, pp=pp, n_heads=n_heads, n_q=n_q, scale=HEAD_DIM ** -0.5),
        grid_spec=grid_spec,
        out_shape=jax.ShapeDtypeStruct((Bs, n_heads, n_q, HEAD_DIM), F32),
        compiler_params=_cparams(("parallel", "arbitrary")),
    )(page_table, jnp.full((1,), layer, jnp.int32), qbd, bias_row, k_new, v_new,
      *([cache_k] * pp), *([cache_v] * pp))
    return out.transpose(0, 2, 1, 3).reshape(Bs, n_q, HD)


def _cumsum_rows(x, tril_bf16):
    C = x.shape[0]
    if C < 16:
        rows = [x[0:1]]
        for i in range(1, C):
            rows.append(rows[-1] + x[i:i + 1])
        return jnp.concatenate(rows, axis=0)
    h1 = x.astype(BF16)
    r1 = x - h1.astype(F32)
    h2 = r1.astype(BF16)
    h3 = (r1 - h2.astype(F32)).astype(BF16)
    dot = lambda h: jnp.dot(tril_bf16, h, preferred_element_type=F32)
    return dot(h1) + dot(h2) + dot(h3)


def _hgrn_kernel(z_ref, lb_ref, ng_ref, s0_ref, o_ref, sT_ref, st_ref, *, C, SB, n_chunks):
    ci = pl.program_id(1)
    W = lb_ref.shape[1]
    n_heads = W // HEAD_DIM
    n_sub = C // SB

    @pl.when(ci == 0)
    def _():
        st_ref[...] = s0_ref[0]

    ones_blk = _head_block_ones(W)
    blockmask = (lax.broadcasted_iota(jnp.int32, (W, W), 0) // HEAD_DIM
                 == lax.broadcasted_iota(jnp.int32, (W, W), 1) // HEAD_DIM)
    lane_head = lax.broadcasted_iota(jnp.int32, (SB, W), 1) // HEAD_DIM
    t_sub = lax.broadcasted_iota(jnp.int32, (SB, W), 0)
    tril = jnp.where(lax.broadcasted_iota(jnp.int32, (C, C), 0)
                     >= lax.broadcasted_iota(jnp.int32, (C, C), 1), 1.0, 0.0).astype(BF16)
    s_col = lax.broadcasted_iota(jnp.int32, (n_heads * SB, C), 1)
    lb = lb_ref[...]
    ng = ng_ref[...]

    def chunk(cc, carry):
        r0 = pl.multiple_of(cc * C, C)
        zc = z_ref[0, pl.ds(r0, C), :]
        q, fz, v, gate = zc[:, :W], zc[:, W:2 * W], zc[:, 2 * W:3 * W], zc[:, 3 * W:]
        f = lb + (1.0 - lb) * _sigmoid(fz)
        k = 1.0 - f
        b = _cumsum_rows(jnp.log(f), tril)
        st = st_ref[...]
        o_inter = _bdot_t(q * jnp.exp(b), st)
        outs = []
        for I in range(n_sub):
            lo_, hi_ = I * SB, (I + 1) * SB
            bI, qI, kI, vI = b[lo_:hi_], q[lo_:hi_], k[lo_:hi_], v[lo_:hi_]
            ds = []
            for s in range(SB):
                e = jnp.exp(jnp.minimum(bI - bI[s:s + 1], 0.0))
                ds.append(jnp.where(t_sub >= s, e * qI * kI[s:s + 1], 0.0))
            G = jnp.dot(jnp.concatenate(ds, axis=0).astype(BF16), ones_blk,
                        preferred_element_type=F32)
            od = G[0:SB] * vI[0:1]
            for s in range(1, SB):
                od = od + G[s * SB:(s + 1) * SB] * vI[s:s + 1]
            if I > 0:
                rho = b[lo_ - 1:lo_]
                qs = qI * jnp.exp(bI - rho)
                kt = k * jnp.exp(jnp.minimum(rho - b, 0.0))
                qst = jnp.concatenate([jnp.where(lane_head == h, qs, 0.0) for h in range(n_heads)],
                                      axis=0)
                att = jnp.where(s_col < lo_, _bdot_t(qst, kt), 0.0)
                R = _bdot(att, v)
                for h in range(n_heads):
                    od = od + jnp.where(lane_head == h, R[h * SB:(h + 1) * SB], 0.0)
            outs.append(od)
        o = o_inter + (outs[0] if n_sub == 1 else jnp.concatenate(outs, axis=0))
        blast = b[C - 1:C]
        kd = k * jnp.exp(blast - b)
        upd = lax.dot_general(v.astype(BF16), kd.astype(BF16), (((0,), (0,)), ((), ())),
                              preferred_element_type=F32)
        st_ref[...] = st * jnp.exp(blast) + jnp.where(blockmask, upd, 0.0)
        ms = _dot2(o * o, ones_blk) * (1.0 / HEAD_DIM)
        o_ref[0, pl.ds(r0, C), :] = o * lax.rsqrt(ms + RMS_EPS) * ng * _silu(gate)
        return carry

    lax.fori_loop(0, n_chunks, chunk, 0)

    @pl.when(ci == pl.num_programs(1) - 1)
    def _():
        sT_ref[0] = st_ref[...]


def hgrn2(zb, lb, norm_g, s0, *, rows_per_step=256):
    B, L, W4 = zb.shape
    W = W4 // 4
    H = W // HEAD_DIM
    C = math.gcd(L, 64)
    SB = min(16, C)
    rows = min(rows_per_step, L)
    eye = jnp.eye(H, dtype=F32)
    st0 = jnp.einsum('bhdv,hg->bhvgd', s0, eye).reshape(B, W, W)
    o, sT = pl.pallas_call(
        functools.partial(_hgrn_kernel, C=C, SB=SB, n_chunks=rows // C),
        grid=(B, L // rows),
        in_specs=[pl.BlockSpec((1, rows, W4), lambda b, i: (b, i, 0)),
                  _const_spec((1, W)), _const_spec((1, W)),
                  pl.BlockSpec((1, W, W), lambda b, i: (b, 0, 0))],
        out_specs=[pl.BlockSpec((1, rows, W), lambda b, i: (b, i, 0)),
                   pl.BlockSpec((1, W, W), lambda b, i: (b, 0, 0))],
        out_shape=[jax.ShapeDtypeStruct((B, L, W), F32), jax.ShapeDtypeStruct((B, W, W), F32)],
        scratch_shapes=[pltpu.VMEM((W, W), F32)],
        compiler_params=_cparams(("parallel", "arbitrary")),
    )(zb, lb.reshape(1, W), norm_g.reshape(1, W), st0)
    s5 = sT.reshape(B, H, HEAD_DIM, H, HEAD_DIM)
    s_new = jnp.einsum('bhvgd,hg->bhdv', s5, eye)
    return o, s_new


def _rwkv_pre_kernel(z_ref, sh0_ref, mu_ref, w0_ref, wup_ref, a0_ref, aup_ref, gup_ref,
                     kkw_ref, kaw_ref, rk_ref, *refs, scan_layout):
    if scan_layout:
        src_o, v_o, bonus_o, g_o, shift_o, prev_ref = refs
    else:
        kk_o, w_o, ka_o, k2_o, r_o, v_o, bonus_o, g_o, shift_o, prev_ref = refs
    ti = pl.program_id(1)
    z = z_ref[0]
    tm, P = z.shape
    W = w0_ref.shape[1]

    @pl.when(ti == 0)
    def _():
        prev_ref[...] = sh0_ref[0]

    row = lax.broadcasted_iota(jnp.int32, (tm, P), 0)
    prev = jnp.where(row == 0, prev_ref[...], pltpu.roll(z, 1, axis=0))
    last = z[tm - 1:tm]
    prev_ref[...] = last
    shift_o[0] = last
    zs = z + (prev - z) * mu_ref[...]
    r, k, v, x4 = zs[:, :W], zs[:, W:2 * W], zs[:, 2 * W:3 * W], zs[:, 3 * W:]
    ones_blk = _head_block_ones(W)
    u = w0_ref[...] + _bdot(jnp.tanh(x4), wup_ref[...])
    w = jnp.exp(-jnp.exp(-_softplus(-u) - 0.5))
    a = _sigmoid(a0_ref[...] + _bdot(x4, aup_ref[...]))
    kkr = k * kkw_ref[...]
    kk = kkr / jnp.maximum(jnp.sqrt(_dot2(kkr * kkr, ones_blk)), 1e-12)
    k2 = k * (1.0 + (a - 1.0) * kaw_ref[...])
    scan_ops = (kk, w, kk * a, k2, r)
    if scan_layout:
        low = lax.broadcasted_iota(jnp.int32, (HEAD_DIM, LANES), 1) < HEAD_DIM
        for qi, xq in enumerate(scan_ops):
            for hp in range(W // LANES):
                for c2 in range(tm // LANES):
                    t_ = xq[c2 * LANES:(c2 + 1) * LANES, hp * LANES:(hp + 1) * LANES].T
                    top, bot = t_[:HEAD_DIM], t_[HEAD_DIM:]
                    src_o[hp, 2 * c2, qi] = jnp.where(low, top, pltpu.roll(bot, HEAD_DIM, axis=1))
                    src_o[hp, 2 * c2 + 1, qi] = jnp.where(low, pltpu.roll(top, HEAD_DIM, axis=1), bot)
    else:
        for o_ref, xq in zip((kk_o, w_o, ka_o, k2_o, r_o), scan_ops):
            o_ref[0] = xq
    for hp in range(W // LANES):
        v_o[0, hp] = v[:, hp * LANES:(hp + 1) * LANES]
    bonus_o[0] = _dot2(r * k2 * rk_ref[...], ones_blk) * v
    g_o[0] = _bdot(_sigmoid(x4), gup_ref[...])


def _rwkv_scan_kernel(src_ref, v_ref, s0_ref, o_ref, sT_ref, st_ref, lhs_ref, *, P, steps):
    c = pl.program_id(1)

    @pl.when(c == 0)
    def _():
        st_ref[...] = s0_ref[...]

    n_op = src_ref.shape[2]
    for p in range(P):
        x = src_ref[p, 0].reshape(n_op * HEAD_DIM, LANES)
        hi, lo = _split2(x)
        lhs_ref[p] = jnp.concatenate([hi, lo], axis=1)

    rr = lax.broadcasted_iota(jnp.int32, (2 * LANES, 2 * LANES), 0)
    cc = lax.broadcasted_iota(jnp.int32, (2 * LANES, 2 * LANES), 1)
    t_of_row = jnp.where((rr // HEAD_DIM) % 2 == (cc // HEAD_DIM) % 2,
                         rr % HEAD_DIM - cc // LANES, -1)

    def step2(i, carry):
        t0 = 2 * i
        sel = jnp.where(t_of_row == t0, 1.0, 0.0).astype(BF16)
        for p in range(P):
            cb = jnp.dot(lhs_ref[p], sel, preferred_element_type=F32)
            s = st_ref[p]
            for u in range(2):
                kk, w, ka, k2, r = (cb[i_ * HEAD_DIM:(i_ + 1) * HEAD_DIM, u * LANES:(u + 1) * LANES]
                                    for i_ in range(5))
                skk = jnp.sum(s * kk, axis=0, keepdims=True)
                s = s * w - ka * skk + k2 * v_ref[p, pl.ds(t0 + u, 1), :]
                o_ref[p, pl.ds(t0 + u, 1), :] = jnp.sum(s * r, axis=0, keepdims=True)
            st_ref[p] = s
        return carry

    lax.fori_loop(0, steps // 2, step2, 0)

    @pl.when(c == pl.num_programs(1) - 1)
    def _():
        sT_ref[...] = st_ref[...]


def _rwkv_post_kernel(o_ref, bonus_ref, g_ref, gng_ref, gnb_ref, out_ref):
    o = jnp.concatenate([o_ref[0, hp] for hp in range(o_ref.shape[1])], axis=1)
    ones_blk = _head_block_ones(o.shape[1])
    inv = 1.0 / HEAD_DIM
    d = o - _dot2(o, ones_blk) * inv
    var = _dot2(d * d, ones_blk) * inv
    out_ref[0] = (d * lax.rsqrt(var + GN_EPS) * gng_ref[...] + gnb_ref[...] + bonus_ref[0]) * g_ref[0]


def rwkv7(za, shift0, s0, p, *, tm=512, pairs_per_step=16):
    B, L, P = za.shape
    W = p['rwkv_w0'].shape[0]
    H = W // HEAD_DIM
    HP = W // LANES
    tm = min(tm, L)
    n_low = P - 3 * W
    rank_w, rank_a = p['rwkv_w_up'].shape[0], p['rwkv_a_up'].shape[0]
    pad_rows = lambda m, r0: jnp.zeros((n_low, W), F32).at[r0:r0 + m.shape[0]].set(m).astype(BF16)
    wup = pad_rows(p['rwkv_w_up'], 0)
    aup = pad_rows(p['rwkv_a_up'], rank_w)
    gup = pad_rows(p['rwkv_g_up'], rank_w + rank_a)
    vec = lambda a: a.reshape(1, -1)
    tok = pl.BlockSpec((1, tm, W), lambda b, i: (b, i, 0))
    tok_sd = jax.ShapeDtypeStruct((B, L, W), F32)
    pair_spec = pl.BlockSpec((1, HP, tm, LANES), lambda b, i: (b, 0, i, 0))
    pair_sd = jax.ShapeDtypeStruct((B, HP, L, LANES), F32)
    steps = min(HEAD_DIM, L)
    assert steps % 2 == 0
    NC = L // steps
    NP = B * HP
    scan_layout = tm % LANES == 0
    if scan_layout:
        ops_specs = [pl.BlockSpec((HP, tm // HEAD_DIM, 5, HEAD_DIM, LANES),
                                  lambda b, i: (b, i, 0, 0, 0))]
        ops_sds = [jax.ShapeDtypeStruct((NP, NC, 5, HEAD_DIM, LANES), F32)]
    else:
        ops_specs, ops_sds = [tok] * 5, [tok_sd] * 5
    *ops, v, bonus, g, shift = pl.pallas_call(
        functools.partial(_rwkv_pre_kernel, scan_layout=scan_layout),
        grid=(B, L // tm),
        in_specs=[pl.BlockSpec((1, tm, P), lambda b, i: (b, i, 0)),
                  pl.BlockSpec((1, 1, P), lambda b, i: (b, 0, 0)),
                  _const_spec((1, P)), _const_spec((1, W)), _const_spec((n_low, W)),
                  _const_spec((1, W)), _const_spec((n_low, W)), _const_spec((n_low, W)),
                  _const_spec((1, W)), _const_spec((1, W)), _const_spec((1, W))],
        out_specs=ops_specs + [pair_spec, tok, tok, pl.BlockSpec((1, 1, P), lambda b, i: (b, 0, 0))],
        out_shape=ops_sds + [pair_sd, tok_sd, tok_sd, jax.ShapeDtypeStruct((B, 1, P), F32)],
        scratch_shapes=[pltpu.VMEM((1, P), F32)],
        compiler_params=_cparams(("parallel", "arbitrary")),
    )(za, shift0.reshape(B, 1, P), vec(p['rwkv_mu']), vec(p['rwkv_w0']), wup, vec(p['rwkv_a0']),
      aup, gup, vec(p['rwkv_k_k']), vec(p['rwkv_k_a']), vec(p['rwkv_r_k']))

    if scan_layout:
        src, = ops
    else:
        src = jnp.stack(ops)
        src = src.reshape(5, B, NC, steps, HP, 2, HEAD_DIM).transpose(1, 4, 2, 0, 6, 5, 3)
        src = jnp.pad(src, ((0, 0),) * 6 + ((0, HEAD_DIM - steps),))
        src = src.reshape(NP, NC, 5, HEAD_DIM, LANES)
    st0 = s0.reshape(B, HP, 2, HEAD_DIM, HEAD_DIM).transpose(0, 1, 4, 2, 3).reshape(NP, HEAD_DIM, LANES)
    PP = min(pairs_per_step, NP)
    o, sT = pl.pallas_call(
        functools.partial(_rwkv_scan_kernel, P=PP, steps=steps),
        grid=(NP // PP, NC),
        in_specs=[pl.BlockSpec((PP, 1, 5, HEAD_DIM, LANES), lambda g_, c: (g_, c, 0, 0, 0)),
                  pl.BlockSpec((PP, steps, LANES), lambda g_, c: (g_, c, 0)),
                  pl.BlockSpec((PP, HEAD_DIM, LANES), lambda g_, c: (g_, 0, 0))],
        out_specs=[pl.BlockSpec((PP, steps, LANES), lambda g_, c: (g_, c, 0)),
                   pl.BlockSpec((PP, HEAD_DIM, LANES), lambda g_, c: (g_, 0, 0))],
        out_shape=[jax.ShapeDtypeStruct((NP, L, LANES), F32),
                   jax.ShapeDtypeStruct((NP, HEAD_DIM, LANES), F32)],
        scratch_shapes=[pltpu.VMEM((PP, HEAD_DIM, LANES), F32),
                        pltpu.VMEM((PP, 5 * HEAD_DIM, 2 * LANES), BF16)],
        compiler_params=_cparams(("parallel", "arbitrary")),
    )(src, v.reshape(NP, L, LANES), st0)
    s_new = sT.reshape(B, HP, HEAD_DIM, 2, HEAD_DIM).transpose(0, 1, 3, 4, 2).reshape(B, H, HEAD_DIM, HEAD_DIM)

    out = pl.pallas_call(
        _rwkv_post_kernel,
        grid=(B, L // tm),
        in_specs=[pair_spec, tok, tok, _const_spec((1, W)), _const_spec((1, W))],
        out_specs=tok,
        out_shape=tok_sd,
        compiler_params=_cparams(("parallel", "parallel")),
    )(o.reshape(B, HP, L, LANES), bonus, g, vec(p['rwkv_gn_g']), vec(p['rwkv_gn_b']))
    return out, s_new, shift.reshape(B, P)


def _route_kernel(x_ref, rt_ref, g_ref, s_ref):
    logits = lax.dot_general(rt_ref[...], x_ref[...], (((1,), (1,)), ((), ())),
                             precision=lax.Precision.HIGHEST, preferred_element_type=F32)
    n_e = logits.shape[0]
    e_id = lax.broadcasted_iota(jnp.int32, logits.shape, 0)
    m1 = jnp.max(logits, axis=0, keepdims=True)
    i1 = jnp.min(jnp.where(logits == m1, e_id, n_e), axis=0, keepdims=True)
    rest = jnp.where(e_id == i1, -jnp.inf, logits)
    m2 = jnp.max(rest, axis=0, keepdims=True)
    i2 = jnp.min(jnp.where(rest == m2, e_id, n_e), axis=0, keepdims=True)
    t = jnp.exp(m2 - m1)
    g1 = 1.0 / (1.0 + t)
    g2 = t / (1.0 + t)
    g_ref[...] = jnp.where(e_id == i1, g1, jnp.where(e_id == i2, g2, 0.0))
    s_ref[...] = jnp.where(e_id == i1, 1.0, jnp.where(e_id == i2, 1.0, 0.0))


def _moe_kernel(xb_ref, g_ref, s_ref, wg_ref, wu_ref, wd_ref, o_ref, rank_ref, xg_ref, y_ref, *, R):
    e = pl.program_id(1)
    f = pl.program_id(2)
    n_f = pl.num_programs(2)
    tm = xb_ref.shape[0]

    @pl.when((e == 0) & (f == 0))
    def _():
        before = jnp.where(lax.broadcasted_iota(jnp.int32, (tm, tm), 0)
                           < lax.broadcasted_iota(jnp.int32, (tm, tm), 1), 1.0, 0.0).astype(BF16)
        rank_ref[...] = jnp.dot(s_ref[...].astype(BF16), before, preferred_element_type=F32)
        o_ref[...] = jnp.zeros(o_ref.shape, F32)

    sel = s_ref[pl.ds(e, 1), :]
    key = jnp.where(sel > 0.0, rank_ref[pl.ds(e, 1), :], -1.0)
    n_chunks = (jnp.sum(sel).astype(jnp.int32) + (R - 1)) // R
    r_id = lax.broadcasted_iota(jnp.int32, (R, tm), 0)

    def one_hot(c):
        return jnp.where(key == (r_id + c * R).astype(F32), 1.0, 0.0).astype(BF16)

    def rows_of(c):
        return pl.ds(pl.multiple_of(c * R, math.gcd(R, 256)), R)

    @pl.when(f == 0)
    def _():
        def gather(c, carry):
            xg_ref[rows_of(c), :] = jnp.dot(one_hot(c), xb_ref[...],
                                            preferred_element_type=F32).astype(BF16)
            return carry
        lax.fori_loop(0, n_chunks, gather, 0)

    def expert(c, carry):
        xg = xg_ref[rows_of(c), :]
        gate = jnp.dot(xg, wg_ref[...], preferred_element_type=F32)
        up = jnp.dot(xg, wu_ref[...], preferred_element_type=F32)
        y = _bdot(_silu(gate) * up, wd_ref[...])

        @pl.when(f == 0)
        def _():
            y_ref[rows_of(c), :] = y

        @pl.when(f > 0)
        def _():
            y_ref[rows_of(c), :] += y
        return carry

    lax.fori_loop(0, n_chunks, expert, 0)

    @pl.when(f == n_f - 1)
    def _():
        gt = g_ref[pl.ds(e, 1), :]
        h1 = gt.astype(BF16)
        r1 = gt - h1.astype(F32)
        h2 = r1.astype(BF16)
        h3 = (r1 - h2.astype(F32)).astype(BF16)
        g3 = jnp.concatenate([h1, h2, h3, jnp.zeros((5, tm), BF16)], axis=0)

        def scatter(c, carry):
            p = one_hot(c)
            g_row = jnp.sum(lax.dot_general(p, g3, (((1,), (1,)), ((), ())),
                                            preferred_element_type=F32), axis=1, keepdims=True)
            yw = (y_ref[rows_of(c), :] * g_row).astype(BF16)
            o_ref[...] += lax.dot_general(p, yw, (((0,), (0,)), ((), ())),
                                          preferred_element_type=F32)
            return carry
        lax.fori_loop(0, n_chunks, scatter, 0)


def moe_ffn(x, router, wg_bf16, wu_bf16, wd_bf16, *, tm=1024, tf=896, R=288):
    T, D = x.shape
    n_e, _, F = wg_bf16.shape
    tm = min(tm, T)
    R = min(R, tm)
    gates, sel = pl.pallas_call(
        _route_kernel,
        grid=(T // tm,),
        in_specs=[pl.BlockSpec((tm, D), lambda i: (i, 0)), _const_spec((n_e, D))],
        out_specs=[pl.BlockSpec((n_e, tm), lambda i: (0, i))] * 2,
        out_shape=[jax.ShapeDtypeStruct((n_e, T), F32)] * 2,
        compiler_params=_cparams(("parallel",)),
    )(x, router.T)
    return pl.pallas_call(
        functools.partial(_moe_kernel, R=R),
        grid=(T // tm, n_e, F // tf),
        in_specs=[pl.BlockSpec((tm, D), lambda i, e, f: (i, 0)),
                  pl.BlockSpec((n_e, tm), lambda i, e, f: (0, i)),
                  pl.BlockSpec((n_e, tm), lambda i, e, f: (0, i)),
                  pl.BlockSpec((None, D, tf), lambda i, e, f: (e, 0, f)),
                  pl.BlockSpec((None, D, tf), lambda i, e, f: (e, 0, f)),
                  pl.BlockSpec((None, tf, D), lambda i, e, f: (e, f, 0))],
        out_specs=pl.BlockSpec((tm, D), lambda i, e, f: (i, 0)),
        out_shape=jax.ShapeDtypeStruct((T, D), F32),
        scratch_shapes=[pltpu.VMEM((n_e, tm), F32), pltpu.VMEM((pl.cdiv(tm, R) * R, D), BF16),
                        pltpu.VMEM((pl.cdiv(tm, R) * R, D), F32)],
        compiler_params=_cparams(("parallel", "arbitrary", "arbitrary")),
    )(x.astype(BF16), gates, sel, wg_bf16, wu_bf16, wd_bf16)


def _layer(x, seq_len, p, lb, states, sb_past, mem_k, mem_v, ffn, alpha, layer, depth, kv_stack):
    T, D = x.shape
    B = T // seq_len
    W = p['rwkv_w0'].shape[0]
    pa = p['rwkv_mu'].shape[0]
    wc = p['sb_bias'].shape[0] * HEAD_DIM
    x, za, zb, q, k, v, k4, v4 = in_proj(x, p['ln_in_g'], p['ln_in_b'], p['w_in'], kv_stack, layer,
                                         depth, apply_ln=layer == 0, widths=(pa, 4 * W, wc, wc, wc))
    to3 = lambda a: a.reshape(B, seq_len, a.shape[-1])
    o_a, rwkv_s, shift = rwkv7(to3(za), states[1], states[0], p)
    o_b, hgrn_s = hgrn2(to3(zb), lb, p['hgrn_norm_g'], states[2])
    n_heads = p['sb_bias'].shape[0]
    if sb_past is None:
        o_c = sb_prompt(to3(q), to3(k), to3(v), p['sb_bias'], n_heads=n_heads)
    else:
        o_c = sb_sample(to3(q), to3(k), to3(v), p['sb_bias'], *sb_past, n_heads=n_heads)
    x = mix_out(o_a.reshape(T, W), o_b.reshape(T, W), o_c.reshape(T, wc), p['w_out'], x,
                p['ln_mix_g'], p['ln_mix_b'], alpha=alpha)
    x = mem_block(x, p['mem_wq'], mem_k, mem_v, p['mem_wo'], p['ln_mem_g'], p['ln_mem_b'],
                  alpha=alpha, seq_len=seq_len)
    x = ffn(x)
    return x, (rwkv_s, shift, hgrn_s, (k4, v4))


def kernel(x_prompt, x_sample, cache_sb_k, cache_sb_v, state_rwkv, state_rwkv_shift, state_hgrn,
           cache_mem_k, cache_mem_v, page_table, mem_prompt, ln_in_g, ln_in_b, w_in, rwkv_mu, rwkv_w0,
           rwkv_w_up, rwkv_a0, rwkv_a_up, rwkv_g_up, rwkv_k_k, rwkv_k_a, rwkv_r_k, rwkv_gn_g, rwkv_gn_b,
           hgrn_lb, hgrn_norm_g, sb_bias, w_out, ln_mix_g, ln_mix_b, mem_wq, mem_wk, mem_wv, mem_wo,
           ln_mem_g, ln_mem_b, ffn_w_gate, ffn_w_up, ffn_w_down, moe_router, moe_w_gate, moe_w_up,
           moe_w_down, ln_ffn_g, ln_ffn_b):
    B, L, D = x_prompt.shape
    Bs, Ls, _ = x_sample.shape
    depth = w_in.shape[0]
    H = state_rwkv.shape[2]
    n_heads_c = sb_bias.shape[1]
    n_mem = mem_prompt.shape[1]
    alpha = (2 * depth) ** 0.25
    bf = lambda a: a.astype(BF16)

    lb_sm = jax.nn.softmax(hgrn_lb.astype(F32), axis=0)
    lb_all = jnp.cumsum(lb_sm, axis=0) - lb_sm[0]

    mem_flat = mem_prompt.reshape(B * n_mem, D)
    zeros_p = (jnp.zeros((B, H, HEAD_DIM, HEAD_DIM), F32), jnp.zeros((B, rwkv_mu.shape[1]), F32),
               jnp.zeros((B, H, HEAD_DIM, HEAD_DIM), F32))

    xp = x_prompt.reshape(B * L, D)
    xs = x_sample.reshape(Bs * Ls, D)
    outs_p = [[] for _ in range(5)]
    outs_s = [[] for _ in range(3)]
    kv_p = kv_s = None
    for l in range(depth):
        p = dict(ln_in_g=ln_in_g, ln_in_b=ln_in_b, w_in=bf(w_in[l]), w_out=bf(w_out[l]),
                 rwkv_mu=rwkv_mu[l], rwkv_w0=rwkv_w0[l], rwkv_w_up=rwkv_w_up[l], rwkv_a0=rwkv_a0[l],
                 rwkv_a_up=rwkv_a_up[l], rwkv_g_up=rwkv_g_up[l], rwkv_k_k=rwkv_k_k[l],
                 rwkv_k_a=rwkv_k_a[l], rwkv_r_k=rwkv_r_k[l], rwkv_gn_g=rwkv_gn_g[l],
                 rwkv_gn_b=rwkv_gn_b[l], hgrn_norm_g=hgrn_norm_g[l], sb_bias=sb_bias[l],
                 ln_mix_g=ln_mix_g[l], ln_mix_b=ln_mix_b[l], mem_wq=bf(mem_wq[l]),
                 mem_wo=bf(mem_wo[l]), ln_mem_g=ln_mem_g[l], ln_mem_b=ln_mem_b[l])
        j = l // 2
        if l % 2 == 0:
            wg, wu, wd = bf(ffn_w_gate[j]), bf(ffn_w_up[j]), bf(ffn_w_down[j])
            ffn = lambda x, wg=wg, wu=wu, wd=wd, l=l: ffn_block(
                x, wg, wu, wd, ln_ffn_g[l], ln_ffn_b[l], alpha=alpha)
        else:
            wg, wu, wd = bf(moe_w_gate[j]), bf(moe_w_up[j]), bf(moe_w_down[j])
            ffn = lambda x, wg=wg, wu=wu, wd=wd, j=j, l=l: res_ln(
                x, moe_ffn(x, moe_router[j], wg, wu, wd), ln_ffn_g[l], ln_ffn_b[l], alpha=alpha)
        mk, mv = matmul2(mem_flat, bf(mem_wk[l]), bf(mem_wv[l]))
        mk = mk.reshape(B, n_mem, D)
        mv = mv.reshape(B, n_mem, D)
        xp, (s_a, sh, s_b, kv_p) = _layer(xp, L, p, lb_all[l], zeros_p, None, mk, mv, ffn,
                                          alpha, l, depth, kv_p)
        for lst, val in zip(outs_p, (s_a, sh, s_b,
                                     mk.reshape(B, n_mem, N_MEM_HEADS, D // N_MEM_HEADS),
                                     mv.reshape(B, n_mem, N_MEM_HEADS, D // N_MEM_HEADS))):
            lst.append(val)
        states = (state_rwkv[l], state_rwkv_shift[l], state_hgrn[l])
        xs, (s_a, sh, s_b, kv_s) = _layer(
            xs, Ls, p, lb_all[l], states, (cache_sb_k, cache_sb_v, page_table, l),
            cache_mem_k[l].reshape(Bs, n_mem, D), cache_mem_v[l].reshape(Bs, n_mem, D), ffn,
            alpha, l, depth, kv_s)
        for lst, val in zip(outs_s, (s_a, sh, s_b)):
            lst.append(val)
    kv5 = lambda a, nb, sl: a.reshape(depth, nb, sl, n_heads_c, HEAD_DIM)
    return (xp.reshape(B, L, D), xs.reshape(Bs, Ls, D),
            kv5(kv_p[0], B, L), kv5(kv_p[1], B, L), *(jnp.stack(o) for o in outs_p),
            kv5(kv_s[0], Bs, Ls), kv5(kv_s[1], Bs, Ls), *(jnp.stack(o) for o in outs_s))
```

```python
import functools
import math

import jax
import jax.numpy as jnp
from jax import lax
from jax.experimental import pallas as pl
from jax.experimental.pallas import tpu as pltpu

F32 = jnp.float32
BF16 = jnp.bfloat16

HEAD_DIM = 64
LANES = 128
PAGE_SIZE = 128
N_MEM_HEADS = 4
TOP_K = 2
LN_EPS = 1e-5
GN_EPS = 64e-5
RMS_EPS = 1e-6
VMEM_LIMIT = 56 * 1024 * 1024


def _cparams(sem):
    return pltpu.CompilerParams(dimension_semantics=sem, vmem_limit_bytes=VMEM_LIMIT)


def _const_spec(shape):
    nd = len(shape)
    return pl.BlockSpec(shape, lambda *_: (0,) * nd, pipeline_mode=pl.Buffered(1))


def _bdot(a, b):
    return jnp.dot(a.astype(BF16), b.astype(BF16), preferred_element_type=F32)


def _bdot_t(a, b):
    return lax.dot_general(a.astype(BF16), b.astype(BF16), (((1,), (1,)), ((), ())),
                           preferred_element_type=F32)


def _split2(x):
    hi = x.astype(BF16)
    lo = (x - hi.astype(F32)).astype(BF16)
    return hi, lo


def _dot2(x, w_bf16):
    hi, lo = _split2(x)
    return (jnp.dot(hi, w_bf16, preferred_element_type=F32)
            + jnp.dot(lo, w_bf16, preferred_element_type=F32))


def _layer_norm(x, g, b):
    mu = jnp.mean(x, -1, keepdims=True)
    xc = x - mu
    var = jnp.mean(xc * xc, -1, keepdims=True)
    return xc * lax.rsqrt(var + LN_EPS) * g + b


def _sigmoid(x):
    return 1.0 / (1.0 + jnp.exp(-x))


def _silu(x):
    return x * _sigmoid(x)


def _softplus(x):
    return jnp.maximum(x, 0.0) + jnp.log(1.0 + jnp.exp(-jnp.abs(x)))


def _head_block_ones(width):
    r = lax.broadcasted_iota(jnp.int32, (width, width), 0) // HEAD_DIM
    c = lax.broadcasted_iota(jnp.int32, (width, width), 1) // HEAD_DIM
    return jnp.where(r == c, 1.0, 0.0).astype(BF16)


def _in_proj_kernel(x_ref, g_ref, b_ref, w_ref, *refs, apply_ln, n_alias, n_heads):
    xn_ref, za_ref, zb_ref, q_ref, k_ref, v_ref, k4_ref, v4_ref = refs[n_alias:]
    x = x_ref[...]
    if apply_ln:
        x = _layer_norm(x, g_ref[...], b_ref[...])
    xn_ref[...] = x
    z = _bdot(x, w_ref[...])
    c0 = 0
    for z_ref in (za_ref, zb_ref, q_ref, k_ref, v_ref):
        z_ref[...] = z[:, c0:c0 + z_ref.shape[1]]
        c0 += z_ref.shape[1]
    tm = x.shape[0]
    hd = k_ref.shape[1]
    for src0, dst in ((c0 - 2 * hd, k4_ref), (c0 - hd, v4_ref)):
        for h in range(n_heads):
            dst[pl.ds(h, tm, stride=n_heads), :] = z[:, src0 + h * HEAD_DIM:src0 + (h + 1) * HEAD_DIM]


def in_proj(x, g, b, w_bf16, kv_stack, layer, depth, *, apply_ln, widths, tm=512):
    T, D = x.shape
    N = w_bf16.shape[1]
    assert sum(widths) == N and all(wd % LANES == 0 for wd in widths)
    n_heads = widths[-1] // HEAD_DIM
    tm = min(tm, T)
    nt = T // tm
    row = lambda width: pl.BlockSpec((tm, width), lambda i: (i, 0))
    stack_spec = pl.BlockSpec((tm * n_heads, HEAD_DIM), lambda i: (layer * nt + i, 0))
    stack_sd = jax.ShapeDtypeStruct((depth * T * n_heads, HEAD_DIM), F32)
    n_alias = 0 if kv_stack is None else 2
    n_out = 6
    return pl.pallas_call(
        functools.partial(_in_proj_kernel, apply_ln=apply_ln, n_alias=n_alias, n_heads=n_heads),
        grid=(nt,),
        in_specs=[row(D), _const_spec((1, D)), _const_spec((1, D)), _const_spec((D, N))]
                 + [pl.BlockSpec(memory_space=pl.ANY)] * n_alias,
        out_specs=[row(D)] + [row(wd) for wd in widths] + [stack_spec, stack_spec],
        out_shape=[jax.ShapeDtypeStruct((T, D), F32)]
                  + [jax.ShapeDtypeStruct((T, wd), F32) for wd in widths] + [stack_sd, stack_sd],
        input_output_aliases={4 + a: n_out + a for a in range(n_alias)},
        compiler_params=_cparams(("parallel",)),
    )(x, g.reshape(1, D), b.reshape(1, D), w_bf16, *(kv_stack or ()))


def _matmul2_kernel(x_ref, w1_ref, w2_ref, o1_ref, o2_ref):
    xb = x_ref[...].astype(BF16)
    o1_ref[...] = jnp.dot(xb, w1_ref[...], preferred_element_type=F32)
    o2_ref[...] = jnp.dot(xb, w2_ref[...], preferred_element_type=F32)


def matmul2(x, w1_bf16, w2_bf16, tm=512):
    T, K = x.shape
    N = w1_bf16.shape[1]
    tm = min(tm, T)
    out = pl.BlockSpec((tm, N), lambda i: (i, 0))
    return pl.pallas_call(
        _matmul2_kernel,
        grid=(T // tm,),
        in_specs=[pl.BlockSpec((tm, K), lambda i: (i, 0)), _const_spec((K, N)), _const_spec((K, N))],
        out_specs=[out, out],
        out_shape=[jax.ShapeDtypeStruct((T, N), F32)] * 2,
        compiler_params=_cparams(("parallel",)),
    )(x, w1_bf16, w2_bf16)


def _mix_out_kernel(oa_ref, ob_ref, oc_ref, wa_ref, wb_ref, wc_ref, res_ref, g_ref, b_ref, o_ref,
                    *, alpha):
    h = (_bdot(oa_ref[...], wa_ref[...]) + _bdot(ob_ref[...], wb_ref[...])
         + _bdot(oc_ref[...], wc_ref[...]))
    o_ref[...] = _layer_norm(alpha * res_ref[...] + h, g_ref[...], b_ref[...])


def mix_out(oa, ob, oc, w_out_bf16, res, g, b, *, alpha, tm=512):
    T, D = res.shape
    wa, wb, wc = oa.shape[1], ob.shape[1], oc.shape[1]
    tm = min(tm, T)
    row = lambda width: pl.BlockSpec((tm, width), lambda i: (i, 0))
    return pl.pallas_call(
        functools.partial(_mix_out_kernel, alpha=alpha),
        grid=(T // tm,),
        in_specs=[row(wa), row(wb), row(wc),
                  _const_spec((wa, D)), _const_spec((wb, D)), _const_spec((wc, D)),
                  row(D), _const_spec((1, D)), _const_spec((1, D))],
        out_specs=row(D),
        out_shape=jax.ShapeDtypeStruct((T, D), F32),
        compiler_params=_cparams(("parallel",)),
    )(oa, ob, oc, w_out_bf16[:wa], w_out_bf16[wa:wa + wb], w_out_bf16[wa + wb:], res,
      g.reshape(1, D), b.reshape(1, D))


def _res_ln_kernel(x_ref, y_ref, g_ref, b_ref, o_ref, *, alpha):
    o_ref[...] = _layer_norm(alpha * x_ref[...] + y_ref[...], g_ref[...], b_ref[...])


def res_ln(x, y, g, b, *, alpha, tm=1024):
    T, D = x.shape
    tm = min(tm, T)
    row = pl.BlockSpec((tm, D), lambda i: (i, 0))
    return pl.pallas_call(
        functools.partial(_res_ln_kernel, alpha=alpha),
        grid=(T // tm,),
        in_specs=[row, row, _const_spec((1, D)), _const_spec((1, D))],
        out_specs=row,
        out_shape=jax.ShapeDtypeStruct((T, D), F32),
        compiler_params=_cparams(("parallel",)),
    )(x, y, g.reshape(1, D), b.reshape(1, D))


def _mem_kernel(x_ref, wq_ref, mk_ref, mv_ref, wo_ref, g_ref, b_ref, o_ref, att_ref,
                *, alpha, nb, rows):
    x = x_ref[...]
    q = _bdot(x, wq_ref[...])
    D = q.shape[1]
    dh = D // N_MEM_HEADS
    scale = dh ** -0.5
    for bi in range(nb):
        r0 = bi * rows
        for h in range(N_MEM_HEADS):
            c0 = h * dh
            qh = q[r0:r0 + rows, c0:c0 + dh]
            kh = mk_ref[bi, :, c0:c0 + dh]
            vh = mv_ref[bi, :, c0:c0 + dh]
            s = _bdot_t(qh, kh) * scale
            s = s - jnp.max(s, -1, keepdims=True)
            p = jnp.exp(s)
            p = p / jnp.sum(p, -1, keepdims=True)
            att_ref[r0:r0 + rows, c0:c0 + dh] = _bdot(p, vh)
    y = _bdot(att_ref[...], wo_ref[...])
    o_ref[...] = _layer_norm(alpha * x + y, g_ref[...], b_ref[...])


def mem_block(x, wq_bf16, mk, mv, wo_bf16, g, b, *, alpha, seq_len, layer=0, tm=512):
    T, D = x.shape
    _, B, n_mem, _ = mk.shape
    if seq_len >= tm:
        nb, rows = 1, tm
        mem_map = lambda i: (layer, i // (seq_len // tm), 0, 0)
    else:
        nb, rows = min(B, 8), seq_len
        tm = nb * rows
        mem_map = lambda i: (layer, i, 0, 0)
    row = pl.BlockSpec((tm, D), lambda i: (i, 0))
    return pl.pallas_call(
        functools.partial(_mem_kernel, alpha=alpha, nb=nb, rows=rows),
        grid=(T // tm,),
        in_specs=[row, _const_spec((D, D)),
                  pl.BlockSpec((None, nb, n_mem, D), mem_map),
                  pl.BlockSpec((None, nb, n_mem, D), mem_map),
                  _const_spec((D, D)), _const_spec((1, D)), _const_spec((1, D))],
        out_specs=row,
        out_shape=jax.ShapeDtypeStruct((T, D), F32),
        scratch_shapes=[pltpu.VMEM((tm, D), F32)],
        compiler_params=_cparams(("parallel",)),
    )(x, wq_bf16, mk, mv, wo_bf16, g.reshape(1, D), b.reshape(1, D))


def _ffn_kernel(x_ref, wg_ref, wu_ref, wd_ref, g_ref, b_ref, o_ref, *, alpha, tf):
    x = x_ref[...]
    xb = x.astype(BF16)
    F = wg_ref.shape[1]
    y = jnp.zeros(x.shape, F32)
    for f0 in range(0, F, tf):
        gate = jnp.dot(xb, wg_ref[:, f0:f0 + tf], preferred_element_type=F32)
        up = jnp.dot(xb, wu_ref[:, f0:f0 + tf], preferred_element_type=F32)
        y = y + _bdot(_silu(gate) * up, wd_ref[f0:f0 + tf, :])
    o_ref[...] = _layer_norm(alpha * x + y, g_ref[...], b_ref[...])


def ffn_block(x, wg_bf16, wu_bf16, wd_bf16, g, b, *, alpha, tm=512, tf=256):
    T, D = x.shape
    F = wg_bf16.shape[1]
    tm = min(tm, T)
    row = pl.BlockSpec((tm, D), lambda i: (i, 0))
    return pl.pallas_call(
        functools.partial(_ffn_kernel, alpha=alpha, tf=tf),
        grid=(T // tm,),
        in_specs=[row, _const_spec((D, F)), _const_spec((D, F)), _const_spec((F, D)),
                  _const_spec((1, D)), _const_spec((1, D))],
        out_specs=row,
        out_shape=jax.ShapeDtypeStruct((T, D), F32),
        compiler_params=_cparams(("parallel",)),
    )(x, wg_bf16, wu_bf16, wd_bf16, g.reshape(1, D), b.reshape(1, D))


def _sb_prompt_kernel(bias_ref, q_ref, k_ref, v_ref, o_ref, acc_ref, c_ref, *, tq, tk, scale):
    hp = pl.program_id(1)
    qi = pl.program_id(2)
    n_sub = tq // tk
    q = q_ref[0] * scale
    half = lax.broadcasted_iota(jnp.int32, (tq, LANES), 1) // HEAD_DIM
    qm = [jnp.where(half == h2, q, 0.0).astype(BF16) for h2 in range(2)]
    bias = [bias_ref[2 * hp + h2] for h2 in range(2)]
    rr = lax.broadcasted_iota(jnp.int32, (tk, 2 * tk), 0)
    cc = lax.broadcasted_iota(jnp.int32, (tk, 2 * tk), 1)
    mw = jnp.where((cc >= tk) | (rr > cc), 1.0, 0.0).astype(BF16)
    causal = (lax.broadcasted_iota(jnp.int32, (tq, tq), 1)
              < lax.broadcasted_iota(jnp.int32, (tq, tq), 0))
    acc_ref[...] = jnp.zeros(acc_ref.shape, F32)
    c_ref[...] = jnp.zeros(c_ref.shape, F32)

    def block(j, masked):
        start = pl.multiple_of(j * tq, tq)
        kb = k_ref[0, pl.ds(start, tq), :].astype(BF16)
        vb = v_ref[0, pl.ds(start, tq), :].astype(BF16)
        for h2 in range(2):
            z = _bdot_t(qm[h2], kb) + bias[h2]
            sp = _softplus(z)
            spm = (jnp.where(causal, sp, 0.0) if masked else sp).astype(BF16)
            c = c_ref[h2]
            parts = [None] * n_sub
            for s in reversed(range(n_sub)):
                sl = slice(s * tk, (s + 1) * tk)
                r = jnp.dot(spm[:, sl], mw, preferred_element_type=F32)
                parts[s] = z[:, sl] - sp[:, sl] - r[:, :tk] - c
                c = c + r[:, tk:]
            c_ref[h2] = c
            a = jnp.exp(parts[0] if n_sub == 1 else jnp.concatenate(parts, axis=1))
            if masked:
                a = jnp.where(causal, a, 0.0)
            acc_ref[h2] += jnp.dot(a.astype(BF16), vb, preferred_element_type=F32)

    block(qi, True)

    def body(i, carry):
        block(qi - 1 - i, False)
        return carry

    lax.fori_loop(0, qi, body, 0)
    o_ref[0] = jnp.where(half == 0, acc_ref[0], acc_ref[1])


def sb_prompt(q, k, v, bias, *, n_heads, tq=512, tk=128):
    B, L, _ = q.shape
    tq = min(tq, L)
    tk = min(tk, tq)
    grid_spec = pltpu.PrefetchScalarGridSpec(
        num_scalar_prefetch=1,
        grid=(B, n_heads // 2, L // tq),
        in_specs=[pl.BlockSpec((1, tq, LANES), lambda b, hp, qi, bias: (b, qi, hp)),
                  pl.BlockSpec((1, L, LANES), lambda b, hp, qi, bias: (b, 0, hp)),
                  pl.BlockSpec((1, L, LANES), lambda b, hp, qi, bias: (b, 0, hp))],
        out_specs=pl.BlockSpec((1, tq, LANES), lambda b, hp, qi, bias: (b, qi, hp)),
        scratch_shapes=[pltpu.VMEM((2, tq, LANES), F32), pltpu.VMEM((2, tq, tk), F32)],
    )
    return pl.pallas_call(
        functools.partial(_sb_prompt_kernel, tq=tq, tk=tk, scale=HEAD_DIM ** -0.5),
        grid_spec=grid_spec,
        out_shape=jax.ShapeDtypeStruct((B, L, n_heads * HEAD_DIM), F32),
        compiler_params=_cparams(("parallel", "parallel", "arbitrary")),
    )(bias.astype(F32), q, k, v)


def _sb_sample_kernel(pt_ref, lay_ref, qbd_ref, bias_ref, kn_ref, vn_ref, *rest,
                      pp, n_heads, n_q, scale):
    k_refs, v_refs = rest[:pp], rest[pp:2 * pp]
    o_ref, acc_ref, c_ref = rest[2 * pp:]
    j = pl.program_id(1)
    qbd = qbd_ref[0]
    bias = bias_ref[...]
    HQ = qbd.shape[0]
    HD = qbd.shape[1]
    rr = lax.broadcasted_iota(jnp.int32, (PAGE_SIZE, 2 * PAGE_SIZE), 0)
    cc = lax.broadcasted_iota(jnp.int32, (PAGE_SIZE, 2 * PAGE_SIZE), 1)
    mw = jnp.where((cc >= PAGE_SIZE) | (rr > cc), 1.0, 0.0).astype(BF16)

    def attend(kts, vts, mask):
        n = len(kts)
        kt = kts[0] if n == 1 else jnp.concatenate(kts, axis=1)
        z = jnp.dot(qbd, kt, preferred_element_type=F32) * scale
        z = z + (bias if n == 1 else jnp.concatenate([bias] * n, axis=1))
        sp = _softplus(z)
        spm = sp if mask is None else jnp.where(mask, sp, 0.0)
        hi, lo = _split2(spm)
        rows = lambda x: jnp.concatenate([x[:, i * PAGE_SIZE:(i + 1) * PAGE_SIZE] for i in range(n)],
                                         axis=0) if n > 1 else x
        r = (jnp.dot(rows(hi), mw, preferred_element_type=F32)
             + jnp.dot(rows(lo), mw, preferred_element_type=F32))
        t = z - sp
        c = c_ref[...]
        parts = []
        for i in range(n):
            ri = r[i * HQ:(i + 1) * HQ]
            parts.append(t[:, i * PAGE_SIZE:(i + 1) * PAGE_SIZE] - ri[:, :PAGE_SIZE] - c)
            c = c + ri[:, PAGE_SIZE:]
        c_ref[...] = c
        a = jnp.exp(parts[0] if n == 1 else jnp.concatenate(parts, axis=1))
        if mask is not None:
            a = jnp.where(mask, a, 0.0)
        vt = vts[0] if n == 1 else jnp.concatenate(vts, axis=1)
        return lax.dot_general(vt, a.astype(BF16), (((1,), (1,)), ((), ())),
                               preferred_element_type=F32)

    @pl.when(j == 0)
    def _():
        c_ref[...] = jnp.zeros(c_ref.shape, F32)
        row_q = lax.broadcasted_iota(jnp.int32, (HQ, PAGE_SIZE), 0) % n_q
        key = lax.broadcasted_iota(jnp.int32, (HQ, PAGE_SIZE), 1)
        valid = (key < row_q) & (key < n_q)
        acc_ref[...] = attend([kn_ref[0].astype(BF16)], [vn_ref[0].astype(BF16)], valid)

    acc_ref[...] += attend([kr[...].reshape(HD, PAGE_SIZE).astype(BF16) for kr in k_refs],
                           [vr[...].reshape(HD, PAGE_SIZE).astype(BF16) for vr in v_refs], None)

    @pl.when(j == pl.num_programs(1) - 1)
    def _():
        o_ref[0] = acc_ref[...]


def sb_sample(q, k_new, v_new, bias, cache_k, cache_v, page_table, layer, *, n_heads, pp=16):
    Bs, n_q, HD = q.shape
    n_pages = page_table.shape[1]
    pp = min(pp, n_pages)
    HQ = n_heads * n_q
    eye = jnp.eye(n_heads, dtype=F32)
    qbd = jnp.einsum('bqhd,hg->bhqgd', q.reshape(Bs, n_q, n_heads, HEAD_DIM), eye)
    qbd = qbd.reshape(Bs, HQ, HD).astype(BF16)
    bias_rows = jnp.broadcast_to(jnp.repeat(bias.astype(F32), n_q)[:, None], (HQ, PAGE_SIZE))
    new_t = lambda a: jnp.pad(a.transpose(0, 2, 1), ((0, 0), (0, 0), (0, PAGE_SIZE - n_q)))
    pool_t = lambda c: c.transpose(0, 1, 3, 4, 2)

    def page_spec(i):
        return pl.BlockSpec(
            (None, None, n_heads, HEAD_DIM, PAGE_SIZE),
            lambda b, j, pt, lay: (lay[0], pt[b, n_pages - 1 - (j * pp + i)], 0, 0, 0))

    new_spec = pl.BlockSpec((1, HD, PAGE_SIZE), lambda b, j, pt, lay: (b, 0, 0))
    grid_spec = pltpu.PrefetchScalarGridSpec(
        num_scalar_prefetch=2,
        grid=(Bs, n_pages // pp),
        in_specs=[pl.BlockSpec((1, HQ, HD), lambda b, j, pt, lay: (b, 0, 0)),
                  pl.BlockSpec((HQ, PAGE_SIZE), lambda b, j, pt, lay: (0, 0)),
                  new_spec, new_spec]
                 + [page_spec(i) for i in range(pp)] + [page_spec(i) for i in range(pp)],
        out_specs=pl.BlockSpec((1, HD, HQ), lambda b, j, pt, lay: (b, 0, 0)),
        scratch_shapes=[pltpu.VMEM((HD, HQ), F32), pltpu.VMEM((HQ, PAGE_SIZE), F32)],
    )
    acc = pl.pallas_call(
        functools.partial(_sb_sample_kernel, pp=pp, n_heads=n_heads, n_q=n_q, scale=HEAD_DIM ** -0.5),
        grid_spec=grid_spec,
        out_shape=jax.ShapeDtypeStruct((Bs, HD, HQ), F32),
        compiler_params=_cparams(("parallel", "arbitrary")),
    )(page_table, jnp.full((1,), layer, jnp.int32), qbd, bias_rows, new_t(k_new), new_t(v_new),
      *([pool_t(cache_k)] * pp), *([pool_t(cache_v)] * pp))
    out = jnp.einsum('bhdgq,hg->bqhd', acc.reshape(Bs, n_heads, HEAD_DIM, n_heads, n_q), eye)
    return out.reshape(Bs, n_q, HD)


def _cumsum_rows(x, tril_bf16):
    C = x.shape[0]
    if C < 16:
        rows = [x[0:1]]
        for i in range(1, C):
            rows.append(rows[-1] + x[i:i + 1])
        return jnp.concatenate(rows, axis=0)
    h1 = x.astype(BF16)
    r1 = x - h1.astype(F32)
    h2 = r1.astype(BF16)
    h3 = (r1 - h2.astype(F32)).astype(BF16)
    dot = lambda h: jnp.dot(tril_bf16, h, preferred_element_type=F32)
    return dot(h1) + dot(h2) + dot(h3)


def _hgrn_kernel(z_ref, lb_ref, ng_ref, s0_ref, o_ref, sT_ref, st_ref, *, C, SB, n_chunks):
    ci = pl.program_id(1)
    W = lb_ref.shape[1]
    n_heads = W // HEAD_DIM
    n_sub = C // SB

    @pl.when(ci == 0)
    def _():
        st_ref[...] = s0_ref[0]

    ones_blk = _head_block_ones(W)
    blockmask = (lax.broadcasted_iota(jnp.int32, (W, W), 0) // HEAD_DIM
                 == lax.broadcasted_iota(jnp.int32, (W, W), 1) // HEAD_DIM)
    lane_head = lax.broadcasted_iota(jnp.int32, (SB, W), 1) // HEAD_DIM
    t_sub = lax.broadcasted_iota(jnp.int32, (SB, W), 0)
    tril = jnp.where(lax.broadcasted_iota(jnp.int32, (C, C), 0)
                     >= lax.broadcasted_iota(jnp.int32, (C, C), 1), 1.0, 0.0).astype(BF16)
    s_col = lax.broadcasted_iota(jnp.int32, (n_heads * SB, C), 1)
    lb = lb_ref[...]
    ng = ng_ref[...]

    def chunk(cc, carry):
        r0 = pl.multiple_of(cc * C, C)
        zc = z_ref[0, pl.ds(r0, C), :]
        q, fz, v, gate = zc[:, :W], zc[:, W:2 * W], zc[:, 2 * W:3 * W], zc[:, 3 * W:]
        f = lb + (1.0 - lb) * _sigmoid(fz)
        k = 1.0 - f
        b = _cumsum_rows(jnp.log(f), tril)
        st = st_ref[...]
        o_inter = _bdot_t(q * jnp.exp(b), st)
        outs = []
        for I in range(n_sub):
            lo_, hi_ = I * SB, (I + 1) * SB
            bI, qI, kI, vI = b[lo_:hi_], q[lo_:hi_], k[lo_:hi_], v[lo_:hi_]
            ds = []
            for s in range(SB):
                e = jnp.exp(jnp.minimum(bI - bI[s:s + 1], 0.0))
                ds.append(jnp.where(t_sub >= s, e * qI * kI[s:s + 1], 0.0))
            G = jnp.dot(jnp.concatenate(ds, axis=0).astype(BF16), ones_blk,
                        preferred_element_type=F32)
            od = G[0:SB] * vI[0:1]
            for s in range(1, SB):
                od = od + G[s * SB:(s + 1) * SB] * vI[s:s + 1]
            if I > 0:
                rho = b[lo_ - 1:lo_]
                qs = qI * jnp.exp(bI - rho)
                kt = k * jnp.exp(jnp.minimum(rho - b, 0.0))
                qst = jnp.concatenate([jnp.where(lane_head == h, qs, 0.0) for h in range(n_heads)],
                                      axis=0)
                att = jnp.where(s_col < lo_, _bdot_t(qst, kt), 0.0)
                R = _bdot(att, v)
                for h in range(n_heads):
                    od = od + jnp.where(lane_head == h, R[h * SB:(h + 1) * SB], 0.0)
            outs.append(od)
        o = o_inter + (outs[0] if n_sub == 1 else jnp.concatenate(outs, axis=0))
        blast = b[C - 1:C]
        kd = k * jnp.exp(blast - b)
        upd = lax.dot_general(v.astype(BF16), kd.astype(BF16), (((0,), (0,)), ((), ())),
                              preferred_element_type=F32)
        st_ref[...] = st * jnp.exp(blast) + jnp.where(blockmask, upd, 0.0)
        ms = _dot2(o * o, ones_blk) * (1.0 / HEAD_DIM)
        o_ref[0, pl.ds(r0, C), :] = o * lax.rsqrt(ms + RMS_EPS) * ng * _silu(gate)
        return carry

    lax.fori_loop(0, n_chunks, chunk, 0)

    @pl.when(ci == pl.num_programs(1) - 1)
    def _():
        sT_ref[0] = st_ref[...]


def hgrn2(zb, lb, norm_g, s0, *, rows_per_step=256):
    B, L, W4 = zb.shape
    W = W4 // 4
    H = W // HEAD_DIM
    C = math.gcd(L, 64)
    SB = min(16, C)
    rows = min(rows_per_step, L)
    eye = jnp.eye(H, dtype=F32)
    st0 = jnp.einsum('bhdv,hg->bhvgd', s0, eye).reshape(B, W, W)
    o, sT = pl.pallas_call(
        functools.partial(_hgrn_kernel, C=C, SB=SB, n_chunks=rows // C),
        grid=(B, L // rows),
        in_specs=[pl.BlockSpec((1, rows, W4), lambda b, i: (b, i, 0)),
                  _const_spec((1, W)), _const_spec((1, W)),
                  pl.BlockSpec((1, W, W), lambda b, i: (b, 0, 0))],
        out_specs=[pl.BlockSpec((1, rows, W), lambda b, i: (b, i, 0)),
                   pl.BlockSpec((1, W, W), lambda b, i: (b, 0, 0))],
        out_shape=[jax.ShapeDtypeStruct((B, L, W), F32), jax.ShapeDtypeStruct((B, W, W), F32)],
        scratch_shapes=[pltpu.VMEM((W, W), F32)],
        compiler_params=_cparams(("parallel", "arbitrary")),
    )(zb, lb.reshape(1, W), norm_g.reshape(1, W), st0)
    s5 = sT.reshape(B, H, HEAD_DIM, H, HEAD_DIM)
    s_new = jnp.einsum('bhvgd,hg->bhdv', s5, eye)
    return o, s_new


def _rwkv_pre_kernel(z_ref, sh0_ref, mu_ref, w0_ref, wup_ref, a0_ref, aup_ref, gup_ref,
                     kkw_ref, kaw_ref, rk_ref, *refs, scan_layout):
    if scan_layout:
        src_o, v_o, bonus_o, g_o, shift_o, prev_ref = refs
    else:
        kk_o, w_o, ka_o, k2_o, r_o, v_o, bonus_o, g_o, shift_o, prev_ref = refs
    ti = pl.program_id(1)
    z = z_ref[0]
    tm, P = z.shape
    W = w0_ref.shape[1]

    @pl.when(ti == 0)
    def _():
        prev_ref[...] = sh0_ref[0]

    row = lax.broadcasted_iota(jnp.int32, (tm, P), 0)
    prev = jnp.where(row == 0, prev_ref[...], pltpu.roll(z, 1, axis=0))
    last = z[tm - 1:tm]
    prev_ref[...] = last
    shift_o[0] = last
    zs = z + (prev - z) * mu_ref[...]
    r, k, v, x4 = zs[:, :W], zs[:, W:2 * W], zs[:, 2 * W:3 * W], zs[:, 3 * W:]
    ones_blk = _head_block_ones(W)
    u = w0_ref[...] + _bdot(jnp.tanh(x4), wup_ref[...])
    w = jnp.exp(-jnp.exp(-_softplus(-u) - 0.5))
    a = _sigmoid(a0_ref[...] + _bdot(x4, aup_ref[...]))
    kkr = k * kkw_ref[...]
    kk = kkr / jnp.maximum(jnp.sqrt(_dot2(kkr * kkr, ones_blk)), 1e-12)
    k2 = k * (1.0 + (a - 1.0) * kaw_ref[...])
    scan_ops = (kk, w, kk * a, k2, r)
    if scan_layout:
        low = lax.broadcasted_iota(jnp.int32, (HEAD_DIM, LANES), 1) < HEAD_DIM
        for qi, xq in enumerate(scan_ops):
            for hp in range(W // LANES):
                for c2 in range(tm // LANES):
                    t_ = xq[c2 * LANES:(c2 + 1) * LANES, hp * LANES:(hp + 1) * LANES].T
                    top, bot = t_[:HEAD_DIM], t_[HEAD_DIM:]
                    src_o[hp, 2 * c2, qi] = jnp.where(low, top, pltpu.roll(bot, HEAD_DIM, axis=1))
                    src_o[hp, 2 * c2 + 1, qi] = jnp.where(low, pltpu.roll(top, HEAD_DIM, axis=1), bot)
    else:
        for o_ref, xq in zip((kk_o, w_o, ka_o, k2_o, r_o), scan_ops):
            o_ref[0] = xq
    for hp in range(W // LANES):
        v_o[0, hp] = v[:, hp * LANES:(hp + 1) * LANES]
    bonus_o[0] = _dot2(r * k2 * rk_ref[...], ones_blk) * v
    g_o[0] = _bdot(_sigmoid(x4), gup_ref[...])


def _rwkv_scan_kernel(src_ref, v_ref, s0_ref, o_ref, sT_ref, st_ref, lhs_ref, *, P, steps):
    c = pl.program_id(1)

    @pl.when(c == 0)
    def _():
        st_ref[...] = s0_ref[...]

    n_op = src_ref.shape[2]
    for p in range(P):
        x = src_ref[p, 0].reshape(n_op * HEAD_DIM, LANES)
        hi, lo = _split2(x)
        lhs_ref[p] = jnp.concatenate([hi, lo], axis=1)

    rr = lax.broadcasted_iota(jnp.int32, (2 * LANES, 2 * LANES), 0)
    cc = lax.broadcasted_iota(jnp.int32, (2 * LANES, 2 * LANES), 1)
    t_of_row = jnp.where((rr // HEAD_DIM) % 2 == (cc // HEAD_DIM) % 2,
                         rr % HEAD_DIM - cc // LANES, -1)

    def step2(i, carry):
        t0 = 2 * i
        sel = jnp.where(t_of_row == t0, 1.0, 0.0).astype(BF16)
        for p in range(P):
            cb = jnp.dot(lhs_ref[p], sel, preferred_element_type=F32)
            s = st_ref[p]
            for u in range(2):
                kk, w, ka, k2, r = (cb[i_ * HEAD_DIM:(i_ + 1) * HEAD_DIM, u * LANES:(u + 1) * LANES]
                                    for i_ in range(5))
                skk = jnp.sum(s * kk, axis=0, keepdims=True)
                s = s * w - ka * skk + k2 * v_ref[p, pl.ds(t0 + u, 1), :]
                o_ref[p, pl.ds(t0 + u, 1), :] = jnp.sum(s * r, axis=0, keepdims=True)
            st_ref[p] = s
        return carry

    lax.fori_loop(0, steps // 2, step2, 0)

    @pl.when(c == pl.num_programs(1) - 1)
    def _():
        sT_ref[...] = st_ref[...]


def _rwkv_post_kernel(o_ref, bonus_ref, g_ref, gng_ref, gnb_ref, out_ref):
    o = jnp.concatenate([o_ref[0, hp] for hp in range(o_ref.shape[1])], axis=1)
    ones_blk = _head_block_ones(o.shape[1])
    inv = 1.0 / HEAD_DIM
    d = o - _dot2(o, ones_blk) * inv
    var = _dot2(d * d, ones_blk) * inv
    out_ref[0] = (d * lax.rsqrt(var + GN_EPS) * gng_ref[...] + gnb_ref[...] + bonus_ref[0]) * g_ref[0]


def rwkv7(za, shift0, s0, p, *, tm=512, pairs_per_step=16):
    B, L, P = za.shape
    W = p['rwkv_w0'].shape[0]
    H = W // HEAD_DIM
    HP = W // LANES
    tm = min(tm, L)
    n_low = P - 3 * W
    rank_w, rank_a = p['rwkv_w_up'].shape[0], p['rwkv_a_up'].shape[0]
    pad_rows = lambda m, r0: jnp.zeros((n_low, W), F32).at[r0:r0 + m.shape[0]].set(m).astype(BF16)
    wup = pad_rows(p['rwkv_w_up'], 0)
    aup = pad_rows(p['rwkv_a_up'], rank_w)
    gup = pad_rows(p['rwkv_g_up'], rank_w + rank_a)
    vec = lambda a: a.reshape(1, -1)
    tok = pl.BlockSpec((1, tm, W), lambda b, i: (b, i, 0))
    tok_sd = jax.ShapeDtypeStruct((B, L, W), F32)
    pair_spec = pl.BlockSpec((1, HP, tm, LANES), lambda b, i: (b, 0, i, 0))
    pair_sd = jax.ShapeDtypeStruct((B, HP, L, LANES), F32)
    steps = min(HEAD_DIM, L)
    assert steps % 2 == 0
    NC = L // steps
    NP = B * HP
    scan_layout = tm % LANES == 0
    if scan_layout:
        ops_specs = [pl.BlockSpec((HP, tm // HEAD_DIM, 5, HEAD_DIM, LANES),
                                  lambda b, i: (b, i, 0, 0, 0))]
        ops_sds = [jax.ShapeDtypeStruct((NP, NC, 5, HEAD_DIM, LANES), F32)]
    else:
        ops_specs, ops_sds = [tok] * 5, [tok_sd] * 5
    *ops, v, bonus, g, shift = pl.pallas_call(
        functools.partial(_rwkv_pre_kernel, scan_layout=scan_layout),
        grid=(B, L // tm),
        in_specs=[pl.BlockSpec((1, tm, P), lambda b, i: (b, i, 0)),
                  pl.BlockSpec((1, 1, P), lambda b, i: (b, 0, 0)),
                  _const_spec((1, P)), _const_spec((1, W)), _const_spec((n_low, W)),
                  _const_spec((1, W)), _const_spec((n_low, W)), _const_spec((n_low, W)),
                  _const_spec((1, W)), _const_spec((1, W)), _const_spec((1, W))],
        out_specs=ops_specs + [pair_spec, tok, tok, pl.BlockSpec((1, 1, P), lambda b, i: (b, 0, 0))],
        out_shape=ops_sds + [pair_sd, tok_sd, tok_sd, jax.ShapeDtypeStruct((B, 1, P), F32)],
        scratch_shapes=[pltpu.VMEM((1, P), F32)],
        compiler_params=_cparams(("parallel", "arbitrary")),
    )(za, shift0.reshape(B, 1, P), vec(p['rwkv_mu']), vec(p['rwkv_w0']), wup, vec(p['rwkv_a0']),
      aup, gup, vec(p['rwkv_k_k']), vec(p['rwkv_k_a']), vec(p['rwkv_r_k']))

    if scan_layout:
        src, = ops
    else:
        src = jnp.stack(ops)
        src = src.reshape(5, B, NC, steps, HP, 2, HEAD_DIM).transpose(1, 4, 2, 0, 6, 5, 3)
        src = jnp.pad(src, ((0, 0),) * 6 + ((0, HEAD_DIM - steps),))
        src = src.reshape(NP, NC, 5, HEAD_DIM, LANES)
    st0 = s0.reshape(B, HP, 2, HEAD_DIM, HEAD_DIM).transpose(0, 1, 4, 2, 3).reshape(NP, HEAD_DIM, LANES)
    PP = min(pairs_per_step, NP)
    o, sT = pl.pallas_call(
        functools.partial(_rwkv_scan_kernel, P=PP, steps=steps),
        grid=(NP // PP, NC),
        in_specs=[pl.BlockSpec((PP, 1, 5, HEAD_DIM, LANES), lambda g_, c: (g_, c, 0, 0, 0)),
                  pl.BlockSpec((PP, steps, LANES), lambda g_, c: (g_, c, 0)),
                  pl.BlockSpec((PP, HEAD_DIM, LANES), lambda g_, c: (g_, 0, 0))],
        out_specs=[pl.BlockSpec((PP, steps, LANES), lambda g_, c: (g_, c, 0)),
                   pl.BlockSpec((PP, HEAD_DIM, LANES), lambda g_, c: (g_, 0, 0))],
        out_shape=[jax.ShapeDtypeStruct((NP, L, LANES), F32),
                   jax.ShapeDtypeStruct((NP, HEAD_DIM, LANES), F32)],
        scratch_shapes=[pltpu.VMEM((PP, HEAD_DIM, LANES), F32),
                        pltpu.VMEM((PP, 5 * HEAD_DIM, 2 * LANES), BF16)],
        compiler_params=_cparams(("parallel", "arbitrary")),
    )(src, v.reshape(NP, L, LANES), st0)
    s_new = sT.reshape(B, HP, HEAD_DIM, 2, HEAD_DIM).transpose(0, 1, 3, 4, 2).reshape(B, H, HEAD_DIM, HEAD_DIM)

    out = pl.pallas_call(
        _rwkv_post_kernel,
        grid=(B, L // tm),
        in_specs=[pair_spec, tok, tok, _const_spec((1, W)), _const_spec((1, W))],
        out_specs=tok,
        out_shape=tok_sd,
        compiler_params=_cparams(("parallel", "parallel")),
    )(o.reshape(B, HP, L, LANES), bonus, g, vec(p['rwkv_gn_g']), vec(p['rwkv_gn_b']))
    return out, s_new, shift.reshape(B, P)


def _route_kernel(x_ref, rt_ref, g_ref, s_ref):
    logits = lax.dot_general(rt_ref[...], x_ref[...], (((1,), (1,)), ((), ())),
                             precision=lax.Precision.HIGHEST, preferred_element_type=F32)
    n_e = logits.shape[0]
    e_id = lax.broadcasted_iota(jnp.int32, logits.shape, 0)
    m1 = jnp.max(logits, axis=0, keepdims=True)
    i1 = jnp.min(jnp.where(logits == m1, e_id, n_e), axis=0, keepdims=True)
    rest = jnp.where(e_id == i1, -jnp.inf, logits)
    m2 = jnp.max(rest, axis=0, keepdims=True)
    i2 = jnp.min(jnp.where(rest == m2, e_id, n_e), axis=0, keepdims=True)
    t = jnp.exp(m2 - m1)
    g1 = 1.0 / (1.0 + t)
    g2 = t / (1.0 + t)
    g_ref[...] = jnp.where(e_id == i1, g1, jnp.where(e_id == i2, g2, 0.0))
    s_ref[...] = jnp.where(e_id == i1, 1.0, jnp.where(e_id == i2, 1.0, 0.0))


def _moe_kernel(xb_ref, g_ref, s_ref, wg_ref, wu_ref, wd_ref, o_ref, rank_ref, xg_ref, y_ref, *, R):
    e = pl.program_id(1)
    f = pl.program_id(2)
    n_f = pl.num_programs(2)
    tm = xb_ref.shape[0]

    @pl.when((e == 0) & (f == 0))
    def _():
        before = jnp.where(lax.broadcasted_iota(jnp.int32, (tm, tm), 0)
                           < lax.broadcasted_iota(jnp.int32, (tm, tm), 1), 1.0, 0.0).astype(BF16)
        rank_ref[...] = jnp.dot(s_ref[...].astype(BF16), before, preferred_element_type=F32)
        o_ref[...] = jnp.zeros(o_ref.shape, F32)

    sel = s_ref[pl.ds(e, 1), :]
    key = jnp.where(sel > 0.0, rank_ref[pl.ds(e, 1), :], -1.0)
    n_chunks = (jnp.sum(sel).astype(jnp.int32) + (R - 1)) // R
    r_id = lax.broadcasted_iota(jnp.int32, (R, tm), 0)

    def one_hot(c):
        return jnp.where(key == (r_id + c * R).astype(F32), 1.0, 0.0).astype(BF16)

    def rows_of(c):
        return pl.ds(pl.multiple_of(c * R, math.gcd(R, 256)), R)

    @pl.when(f == 0)
    def _():
        def gather(c, carry):
            xg_ref[rows_of(c), :] = jnp.dot(one_hot(c), xb_ref[...],
                                            preferred_element_type=F32).astype(BF16)
            return carry
        lax.fori_loop(0, n_chunks, gather, 0)

    def expert(c, carry):
        xg = xg_ref[rows_of(c), :]
        gate = jnp.dot(xg, wg_ref[...], preferred_element_type=F32)
        up = jnp.dot(xg, wu_ref[...], preferred_element_type=F32)
        y = _bdot(_silu(gate) * up, wd_ref[...])

        @pl.when(f == 0)
        def _():
            y_ref[rows_of(c), :] = y

        @pl.when(f > 0)
        def _():
            y_ref[rows_of(c), :] += y
        return carry

    lax.fori_loop(0, n_chunks, expert, 0)

    @pl.when(f == n_f - 1)
    def _():
        gt = g_ref[pl.ds(e, 1), :]
        h1 = gt.astype(BF16)
        r1 = gt - h1.astype(F32)
        h2 = r1.astype(BF16)
        h3 = (r1 - h2.astype(F32)).astype(BF16)
        g3 = jnp.concatenate([h1, h2, h3, jnp.zeros((5, tm), BF16)], axis=0)

        def scatter(c, carry):
            p = one_hot(c)
            g_row = jnp.sum(lax.dot_general(p, g3, (((1,), (1,)), ((), ())),
                                            preferred_element_type=F32), axis=1, keepdims=True)
            yw = (y_ref[rows_of(c), :] * g_row).astype(BF16)
            o_ref[...] += lax.dot_general(p, yw, (((0,), (0,)), ((), ())),
                                          preferred_element_type=F32)
            return carry
        lax.fori_loop(0, n_chunks, scatter, 0)


def moe_ffn(x, router, wg_bf16, wu_bf16, wd_bf16, *, tm=1024, tf=896, R=288):
    T, D = x.shape
    n_e, _, F = wg_bf16.shape
    tm = min(tm, T)
    R = min(R, tm)
    gates, sel = pl.pallas_call(
        _route_kernel,
        grid=(T // tm,),
        in_specs=[pl.BlockSpec((tm, D), lambda i: (i, 0)), _const_spec((n_e, D))],
        out_specs=[pl.BlockSpec((n_e, tm), lambda i: (0, i))] * 2,
        out_shape=[jax.ShapeDtypeStruct((n_e, T), F32)] * 2,
        compiler_params=_cparams(("parallel",)),
    )(x, router.T)
    return pl.pallas_call(
        functools.partial(_moe_kernel, R=R),
        grid=(T // tm, n_e, F // tf),
        in_specs=[pl.BlockSpec((tm, D), lambda i, e, f: (i, 0)),
                  pl.BlockSpec((n_e, tm), lambda i, e, f: (0, i)),
                  pl.BlockSpec((n_e, tm), lambda i, e, f: (0, i)),
                  pl.BlockSpec((None, D, tf), lambda i, e, f: (e, 0, f)),
                  pl.BlockSpec((None, D, tf), lambda i, e, f: (e, 0, f)),
                  pl.BlockSpec((None, tf, D), lambda i, e, f: (e, f, 0))],
        out_specs=pl.BlockSpec((tm, D), lambda i, e, f: (i, 0)),
        out_shape=jax.ShapeDtypeStruct((T, D), F32),
        scratch_shapes=[pltpu.VMEM((n_e, tm), F32), pltpu.VMEM((pl.cdiv(tm, R) * R, D), BF16),
                        pltpu.VMEM((pl.cdiv(tm, R) * R, D), F32)],
        compiler_params=_cparams(("parallel", "arbitrary", "arbitrary")),
    )(x.astype(BF16), gates, sel, wg_bf16, wu_bf16, wd_bf16)


def _layer(x, seq_len, p, lb, states, sb_past, mem_k, mem_v, ffn, alpha, layer, depth, kv_stack):
    T, D = x.shape
    B = T // seq_len
    W = p['rwkv_w0'].shape[0]
    pa = p['rwkv_mu'].shape[0]
    wc = p['sb_bias'].shape[0] * HEAD_DIM
    x, za, zb, q, k, v, k4, v4 = in_proj(x, p['ln_in_g'], p['ln_in_b'], p['w_in'], kv_stack, layer,
                                         depth, apply_ln=layer == 0, widths=(pa, 4 * W, wc, wc, wc))
    to3 = lambda a: a.reshape(B, seq_len, a.shape[-1])
    o_a, rwkv_s, shift = rwkv7(to3(za), states[1], states[0], p)
    o_b, hgrn_s = hgrn2(to3(zb), lb, p['hgrn_norm_g'], states[2])
    n_heads = p['sb_bias'].shape[0]
    if sb_past is None:
        o_c = sb_prompt(to3(q), to3(k), to3(v), p['sb_bias'], n_heads=n_heads)
    else:
        o_c = sb_sample(to3(q), to3(k), to3(v), p['sb_bias'], *sb_past, n_heads=n_heads)
    x = mix_out(o_a.reshape(T, W), o_b.reshape(T, W), o_c.reshape(T, wc), p['w_out'], x,
                p['ln_mix_g'], p['ln_mix_b'], alpha=alpha)
    mem_layer = layer if mem_k.shape[0] == depth else 0
    x = mem_block(x, p['mem_wq'], mem_k, mem_v, p['mem_wo'], p['ln_mem_g'], p['ln_mem_b'],
                  alpha=alpha, seq_len=seq_len, layer=mem_layer)
    x = ffn(x)
    return x, (rwkv_s, shift, hgrn_s, (k4, v4))


def kernel(x_prompt, x_sample, cache_sb_k, cache_sb_v, state_rwkv, state_rwkv_shift, state_hgrn,
           cache_mem_k, cache_mem_v, page_table, mem_prompt, ln_in_g, ln_in_b, w_in, rwkv_mu, rwkv_w0,
           rwkv_w_up, rwkv_a0, rwkv_a_up, rwkv_g_up, rwkv_k_k, rwkv_k_a, rwkv_r_k, rwkv_gn_g, rwkv_gn_b,
           hgrn_lb, hgrn_norm_g, sb_bias, w_out, ln_mix_g, ln_mix_b, mem_wq, mem_wk, mem_wv, mem_wo,
           ln_mem_g, ln_mem_b, ffn_w_gate, ffn_w_up, ffn_w_down, moe_router, moe_w_gate, moe_w_up,
           moe_w_down, ln_ffn_g, ln_ffn_b):
    B, L, D = x_prompt.shape
    Bs, Ls, _ = x_sample.shape
    depth = w_in.shape[0]
    H = state_rwkv.shape[2]
    n_heads_c = sb_bias.shape[1]
    n_mem = mem_prompt.shape[1]
    alpha = (2 * depth) ** 0.25
    bf = lambda a: a.astype(BF16)

    lb_sm = jax.nn.softmax(hgrn_lb.astype(F32), axis=0)
    lb_all = jnp.cumsum(lb_sm, axis=0) - lb_sm[0]

    mem_flat = mem_prompt.reshape(B * n_mem, D)
    zeros_p = (jnp.zeros((B, H, HEAD_DIM, HEAD_DIM), F32), jnp.zeros((B, rwkv_mu.shape[1]), F32),
               jnp.zeros((B, H, HEAD_DIM, HEAD_DIM), F32))

    xp = x_prompt.reshape(B * L, D)
    xs = x_sample.reshape(Bs * Ls, D)
    outs_p = [[] for _ in range(5)]
    outs_s = [[] for _ in range(3)]
    kv_p = kv_s = None
    for l in range(depth):
        p = dict(ln_in_g=ln_in_g, ln_in_b=ln_in_b, w_in=bf(w_in[l]), w_out=bf(w_out[l]),
                 rwkv_mu=rwkv_mu[l], rwkv_w0=rwkv_w0[l], rwkv_w_up=rwkv_w_up[l], rwkv_a0=rwkv_a0[l],
                 rwkv_a_up=rwkv_a_up[l], rwkv_g_up=rwkv_g_up[l], rwkv_k_k=rwkv_k_k[l],
                 rwkv_k_a=rwkv_k_a[l], rwkv_r_k=rwkv_r_k[l], rwkv_gn_g=rwkv_gn_g[l],
                 rwkv_gn_b=rwkv_gn_b[l], hgrn_norm_g=hgrn_norm_g[l], sb_bias=sb_bias[l],
                 ln_mix_g=ln_mix_g[l], ln_mix_b=ln_mix_b[l], mem_wq=bf(mem_wq[l]),
                 mem_wo=bf(mem_wo[l]), ln_mem_g=ln_mem_g[l], ln_mem_b=ln_mem_b[l])
        j = l // 2
        if l % 2 == 0:
            wg, wu, wd = bf(ffn_w_gate[j]), bf(ffn_w_up[j]), bf(ffn_w_down[j])
            ffn = lambda x, wg=wg, wu=wu, wd=wd, l=l: ffn_block(
                x, wg, wu, wd, ln_ffn_g[l], ln_ffn_b[l], alpha=alpha)
        else:
            wg, wu, wd = bf(moe_w_gate[j]), bf(moe_w_up[j]), bf(moe_w_down[j])
            ffn = lambda x, wg=wg, wu=wu, wd=wd, j=j, l=l: res_ln(
                x, moe_ffn(x, moe_router[j], wg, wu, wd), ln_ffn_g[l], ln_ffn_b[l], alpha=alpha)
        mk, mv = matmul2(mem_flat, bf(mem_wk[l]), bf(mem_wv[l]))
        mk = mk.reshape(1, B, n_mem, D)
        mv = mv.reshape(1, B, n_mem, D)
        xp, (s_a, sh, s_b, kv_p) = _layer(xp, L, p, lb_all[l], zeros_p, None, mk, mv, ffn,
                                          alpha, l, depth, kv_p)
        for lst, val in zip(outs_p, (s_a, sh, s_b,
                                     mk.reshape(B, n_mem, N_MEM_HEADS, D // N_MEM_HEADS),
                                     mv.reshape(B, n_mem, N_MEM_HEADS, D // N_MEM_HEADS))):
            lst.append(val)
        states = (state_rwkv[l], state_rwkv_shift[l], state_hgrn[l])
        xs, (s_a, sh, s_b, kv_s) = _layer(
            xs, Ls, p, lb_all[l], states, (cache_sb_k, cache_sb_v, page_table, l),
            cache_mem_k.reshape(depth, Bs, n_mem, D), cache_mem_v.reshape(depth, Bs, n_mem, D), ffn,
            alpha, l, depth, kv_s)
        for lst, val in zip(outs_s, (s_a, sh, s_b)):
            lst.append(val)
    kv5 = lambda a, nb, sl: a.reshape(depth, nb, sl, n_heads_c, HEAD_DIM)
    return (xp.reshape(B, L, D), xs.reshape(Bs, Ls, D),
            kv5(kv_p[0], B, L), kv5(kv_p[1], B, L), *(jnp.stack(o) for o in outs_p),
            kv5(kv_s[0], Bs, Ls), kv5(kv_s[1], Bs, Ls), *(jnp.stack(o) for o in outs_s))
```

```python
import functools
import math

import jax
import jax.numpy as jnp
from jax import lax
from jax.experimental import pallas as pl
from jax.experimental.pallas import tpu as pltpu

F32 = jnp.float32
BF16 = jnp.bfloat16

HEAD_DIM = 64
LANES = 128
PAGE_SIZE = 128
N_MEM_HEADS = 4
TOP_K = 2
LN_EPS = 1e-5
GN_EPS = 64e-5
RMS_EPS = 1e-6
VMEM_LIMIT = 56 * 1024 * 1024


def _cparams(sem):
    return pltpu.CompilerParams(dimension_semantics=sem, vmem_limit_bytes=VMEM_LIMIT)


def _const_spec(shape):
    nd = len(shape)
    return pl.BlockSpec(shape, lambda *_: (0,) * nd, pipeline_mode=pl.Buffered(1))


def _bdot(a, b):
    return jnp.dot(a.astype(BF16), b.astype(BF16), preferred_element_type=F32)


def _bdot_t(a, b):
    return lax.dot_general(a.astype(BF16), b.astype(BF16), (((1,), (1,)), ((), ())),
                           preferred_element_type=F32)


def _split2(x):
    hi = x.astype(BF16)
    lo = (x - hi.astype(F32)).astype(BF16)
    return hi, lo


def _dot2(x, w_bf16):
    hi, lo = _split2(x)
    return (jnp.dot(hi, w_bf16, preferred_element_type=F32)
            + jnp.dot(lo, w_bf16, preferred_element_type=F32))


def _layer_norm(x, g, b):
    mu = jnp.mean(x, -1, keepdims=True)
    xc = x - mu
    var = jnp.mean(xc * xc, -1, keepdims=True)
    return xc * lax.rsqrt(var + LN_EPS) * g + b


def _sigmoid(x):
    return 1.0 / (1.0 + jnp.exp(-x))


def _silu(x):
    return x * _sigmoid(x)


def _softplus(x):
    return jnp.maximum(x, 0.0) + jnp.log(1.0 + jnp.exp(-jnp.abs(x)))


def _head_block_ones(width):
    r = lax.broadcasted_iota(jnp.int32, (width, width), 0) // HEAD_DIM
    c = lax.broadcasted_iota(jnp.int32, (width, width), 1) // HEAD_DIM
    return jnp.where(r == c, 1.0, 0.0).astype(BF16)


def _in_proj_kernel(x_ref, g_ref, b_ref, w_ref, k4_in, v4_in,
                    xn_ref, za_ref, zb_ref, q_ref, k_ref, v_ref, k4_ref, v4_ref, *, apply_ln, n_heads):
    del k4_in, v4_in
    x = x_ref[...]
    if apply_ln:
        x = _layer_norm(x, g_ref[...], b_ref[...])
    xn_ref[...] = x
    z = _bdot(x, w_ref[...])
    c0 = 0
    for z_ref in (za_ref, zb_ref, q_ref, k_ref, v_ref):
        z_ref[...] = z[:, c0:c0 + z_ref.shape[1]]
        c0 += z_ref.shape[1]
    tm = x.shape[0]
    hd = k_ref.shape[1]
    for src0, dst in ((c0 - 2 * hd, k4_ref), (c0 - hd, v4_ref)):
        for h in range(n_heads):
            dst[pl.ds(h, tm, stride=n_heads), :] = z[:, src0 + h * HEAD_DIM:src0 + (h + 1) * HEAD_DIM]


def in_proj(x, g, b, w_bf16, kv_stack, layer, depth, *, apply_ln, widths, tm=512):
    T, D = x.shape
    N = w_bf16.shape[1]
    assert sum(widths) == N and all(wd % LANES == 0 for wd in widths)
    n_heads = widths[-1] // HEAD_DIM
    tm = min(tm, T)
    nt = T // tm
    row = lambda width: pl.BlockSpec((tm, width), lambda i: (i, 0))
    stack_spec = pl.BlockSpec((tm * n_heads, HEAD_DIM), lambda i: (layer * nt + i, 0))
    stack_sd = jax.ShapeDtypeStruct((depth * T * n_heads, HEAD_DIM), F32)
    return pl.pallas_call(
        functools.partial(_in_proj_kernel, apply_ln=apply_ln, n_heads=n_heads),
        grid=(nt,),
        in_specs=[row(D), _const_spec((1, D)), _const_spec((1, D)), _const_spec((D, N))]
                 + [pl.BlockSpec(memory_space=pl.ANY)] * 2,
        out_specs=[row(D)] + [row(wd) for wd in widths] + [stack_spec, stack_spec],
        out_shape=[jax.ShapeDtypeStruct((T, D), F32)]
                  + [jax.ShapeDtypeStruct((T, wd), F32) for wd in widths] + [stack_sd, stack_sd],
        input_output_aliases={4: 6, 5: 7},
        compiler_params=_cparams(("parallel",)),
    )(x, g.reshape(1, D), b.reshape(1, D), w_bf16, *kv_stack)


def _matmul2_kernel(x_ref, w1_ref, w2_ref, o1_ref, o2_ref):
    xb = x_ref[...].astype(BF16)
    o1_ref[...] = jnp.dot(xb, w1_ref[...], preferred_element_type=F32)
    o2_ref[...] = jnp.dot(xb, w2_ref[...], preferred_element_type=F32)


def matmul2(x, w1_bf16, w2_bf16, tm=512):
    T, K = x.shape
    N = w1_bf16.shape[1]
    tm = min(tm, T)
    out = pl.BlockSpec((tm, N), lambda i: (i, 0))
    return pl.pallas_call(
        _matmul2_kernel,
        grid=(T // tm,),
        in_specs=[pl.BlockSpec((tm, K), lambda i: (i, 0)), _const_spec((K, N)), _const_spec((K, N))],
        out_specs=[out, out],
        out_shape=[jax.ShapeDtypeStruct((T, N), F32)] * 2,
        compiler_params=_cparams(("parallel",)),
    )(x, w1_bf16, w2_bf16)


def _mix_out_kernel(oa_ref, ob_ref, oc_ref, wa_ref, wb_ref, wc_ref, res_ref, g_ref, b_ref, o_ref,
                    *, alpha):
    h = (_bdot(oa_ref[...], wa_ref[...]) + _bdot(ob_ref[...], wb_ref[...])
         + _bdot(oc_ref[...], wc_ref[...]))
    o_ref[...] = _layer_norm(alpha * res_ref[...] + h, g_ref[...], b_ref[...])


def mix_out(oa, ob, oc, w_out_bf16, res, g, b, *, alpha, tm=512):
    T, D = res.shape
    wa, wb, wc = oa.shape[1], ob.shape[1], oc.shape[1]
    tm = min(tm, T)
    row = lambda width: pl.BlockSpec((tm, width), lambda i: (i, 0))
    return pl.pallas_call(
        functools.partial(_mix_out_kernel, alpha=alpha),
        grid=(T // tm,),
        in_specs=[row(wa), row(wb), row(wc),
                  _const_spec((wa, D)), _const_spec((wb, D)), _const_spec((wc, D)),
                  row(D), _const_spec((1, D)), _const_spec((1, D))],
        out_specs=row(D),
        out_shape=jax.ShapeDtypeStruct((T, D), F32),
        compiler_params=_cparams(("parallel",)),
    )(oa, ob, oc, w_out_bf16[:wa], w_out_bf16[wa:wa + wb], w_out_bf16[wa + wb:], res,
      g.reshape(1, D), b.reshape(1, D))


def _res_ln_kernel(x_ref, y_ref, g_ref, b_ref, o_ref, *, alpha):
    o_ref[...] = _layer_norm(alpha * x_ref[...] + y_ref[...], g_ref[...], b_ref[...])


def res_ln(x, y, g, b, *, alpha, tm=1024):
    T, D = x.shape
    tm = min(tm, T)
    row = pl.BlockSpec((tm, D), lambda i: (i, 0))
    return pl.pallas_call(
        functools.partial(_res_ln_kernel, alpha=alpha),
        grid=(T // tm,),
        in_specs=[row, row, _const_spec((1, D)), _const_spec((1, D))],
        out_specs=row,
        out_shape=jax.ShapeDtypeStruct((T, D), F32),
        compiler_params=_cparams(("parallel",)),
    )(x, y, g.reshape(1, D), b.reshape(1, D))


def _mem_kernel(x_ref, wq_ref, mk_ref, mv_ref, wo_ref, g_ref, b_ref, o_ref, att_ref,
                *, alpha, nb, rows):
    x = x_ref[...]
    q = _bdot(x, wq_ref[...])
    D = q.shape[1]
    dh = D // N_MEM_HEADS
    scale = dh ** -0.5
    for bi in range(nb):
        r0 = bi * rows
        for h in range(N_MEM_HEADS):
            c0 = h * dh
            qh = q[r0:r0 + rows, c0:c0 + dh]
            kh = mk_ref[bi, :, c0:c0 + dh]
            vh = mv_ref[bi, :, c0:c0 + dh]
            s = _bdot_t(qh, kh) * scale
            s = s - jnp.max(s, -1, keepdims=True)
            p = jnp.exp(s)
            p = p / jnp.sum(p, -1, keepdims=True)
            att_ref[r0:r0 + rows, c0:c0 + dh] = _bdot(p, vh)
    y = _bdot(att_ref[...], wo_ref[...])
    o_ref[...] = _layer_norm(alpha * x + y, g_ref[...], b_ref[...])


def mem_block(x, wq_bf16, mk, mv, wo_bf16, g, b, *, alpha, seq_len, layer=0, tm=512):
    T, D = x.shape
    _, B, n_mem, _ = mk.shape
    if seq_len >= tm:
        nb, rows = 1, tm
        mem_map = lambda i: (layer, i // (seq_len // tm), 0, 0)
    else:
        nb, rows = min(B, 8), seq_len
        tm = nb * rows
        mem_map = lambda i: (layer, i, 0, 0)
    row = pl.BlockSpec((tm, D), lambda i: (i, 0))
    return pl.pallas_call(
        functools.partial(_mem_kernel, alpha=alpha, nb=nb, rows=rows),
        grid=(T // tm,),
        in_specs=[row, _const_spec((D, D)),
                  pl.BlockSpec((None, nb, n_mem, D), mem_map),
                  pl.BlockSpec((None, nb, n_mem, D), mem_map),
                  _const_spec((D, D)), _const_spec((1, D)), _const_spec((1, D))],
        out_specs=row,
        out_shape=jax.ShapeDtypeStruct((T, D), F32),
        scratch_shapes=[pltpu.VMEM((tm, D), F32)],
        compiler_params=_cparams(("parallel",)),
    )(x, wq_bf16, mk, mv, wo_bf16, g.reshape(1, D), b.reshape(1, D))


def _ffn_kernel(x_ref, wg_ref, wu_ref, wd_ref, g_ref, b_ref, o_ref, *, alpha, tf):
    x = x_ref[...]
    xb = x.astype(BF16)
    F = wg_ref.shape[1]
    y = jnp.zeros(x.shape, F32)
    for f0 in range(0, F, tf):
        gate = jnp.dot(xb, wg_ref[:, f0:f0 + tf], preferred_element_type=F32)
        up = jnp.dot(xb, wu_ref[:, f0:f0 + tf], preferred_element_type=F32)
        y = y + _bdot(_silu(gate) * up, wd_ref[f0:f0 + tf, :])
    o_ref[...] = _layer_norm(alpha * x + y, g_ref[...], b_ref[...])


def ffn_block(x, wg_bf16, wu_bf16, wd_bf16, g, b, *, alpha, tm=512, tf=256):
    T, D = x.shape
    F = wg_bf16.shape[1]
    tm = min(tm, T)
    row = pl.BlockSpec((tm, D), lambda i: (i, 0))
    return pl.pallas_call(
        functools.partial(_ffn_kernel, alpha=alpha, tf=tf),
        grid=(T // tm,),
        in_specs=[row, _const_spec((D, F)), _const_spec((D, F)), _const_spec((F, D)),
                  _const_spec((1, D)), _const_spec((1, D))],
        out_specs=row,
        out_shape=jax.ShapeDtypeStruct((T, D), F32),
        compiler_params=_cparams(("parallel",)),
    )(x, wg_bf16, wu_bf16, wd_bf16, g.reshape(1, D), b.reshape(1, D))


def _sb_prompt_kernel(bias_ref, q_ref, k_ref, v_ref, o_ref, acc_ref, c_ref, *, tq, tk, scale):
    hp = pl.program_id(1)
    qi = pl.program_id(2)
    n_sub = tq // tk
    q = q_ref[0] * scale
    half = lax.broadcasted_iota(jnp.int32, (tq, LANES), 1) // HEAD_DIM
    qm = [jnp.where(half == h2, q, 0.0).astype(BF16) for h2 in range(2)]
    bias = [bias_ref[2 * hp + h2] for h2 in range(2)]
    rr = lax.broadcasted_iota(jnp.int32, (tk, 2 * tk), 0)
    cc = lax.broadcasted_iota(jnp.int32, (tk, 2 * tk), 1)
    mw = jnp.where((cc >= tk) | (rr > cc), 1.0, 0.0).astype(BF16)
    causal = (lax.broadcasted_iota(jnp.int32, (tq, tq), 1)
              < lax.broadcasted_iota(jnp.int32, (tq, tq), 0))
    acc_ref[...] = jnp.zeros(acc_ref.shape, F32)
    c_ref[...] = jnp.zeros(c_ref.shape, F32)

    def block(j, masked):
        start = pl.multiple_of(j * tq, tq)
        kb = k_ref[0, pl.ds(start, tq), :].astype(BF16)
        vb = v_ref[0, pl.ds(start, tq), :].astype(BF16)
        for h2 in range(2):
            z = _bdot_t(qm[h2], kb) + bias[h2]
            sp = _softplus(z)
            spm = (jnp.where(causal, sp, 0.0) if masked else sp).astype(BF16)
            c = c_ref[h2]
            parts = [None] * n_sub
            for s in reversed(range(n_sub)):
                sl = slice(s * tk, (s + 1) * tk)
                r = jnp.dot(spm[:, sl], mw, preferred_element_type=F32)
                parts[s] = z[:, sl] - sp[:, sl] - r[:, :tk] - c
                c = c + r[:, tk:]
            c_ref[h2] = c
            a = jnp.exp(parts[0] if n_sub == 1 else jnp.concatenate(parts, axis=1))
            if masked:
                a = jnp.where(causal, a, 0.0)
            acc_ref[h2] += jnp.dot(a.astype(BF16), vb, preferred_element_type=F32)

    block(qi, True)

    def body(i, carry):
        block(qi - 1 - i, False)
        return carry

    lax.fori_loop(0, qi, body, 0)
    o_ref[0] = jnp.where(half == 0, acc_ref[0], acc_ref[1])


def sb_prompt(q, k, v, bias, *, n_heads, tq=512, tk=128):
    B, L, _ = q.shape
    tq = min(tq, L)
    tk = min(tk, tq)
    grid_spec = pltpu.PrefetchScalarGridSpec(
        num_scalar_prefetch=1,
        grid=(B, n_heads // 2, L // tq),
        in_specs=[pl.BlockSpec((1, tq, LANES), lambda b, hp, qi, bias: (b, qi, hp)),
                  pl.BlockSpec((1, L, LANES), lambda b, hp, qi, bias: (b, 0, hp)),
                  pl.BlockSpec((1, L, LANES), lambda b, hp, qi, bias: (b, 0, hp))],
        out_specs=pl.BlockSpec((1, tq, LANES), lambda b, hp, qi, bias: (b, qi, hp)),
        scratch_shapes=[pltpu.VMEM((2, tq, LANES), F32), pltpu.VMEM((2, tq, tk), F32)],
    )
    return pl.pallas_call(
        functools.partial(_sb_prompt_kernel, tq=tq, tk=tk, scale=HEAD_DIM ** -0.5),
        grid_spec=grid_spec,
        out_shape=jax.ShapeDtypeStruct((B, L, n_heads * HEAD_DIM), F32),
        compiler_params=_cparams(("parallel", "parallel", "arbitrary")),
    )(bias.astype(F32), q, k, v)


def _sb_sample_kernel(pt_ref, lay_ref, qbd_ref, bias_ref, kn_ref, vn_ref, *rest,
                      pp, n_heads, n_q, scale):
    k_refs, v_refs = rest[:pp], rest[pp:2 * pp]
    o_ref, acc_ref, c_ref = rest[2 * pp:]
    j = pl.program_id(1)
    qbd = qbd_ref[0]
    bias = bias_ref[...]
    HQ = qbd.shape[0]
    HD = qbd.shape[1]
    rr = lax.broadcasted_iota(jnp.int32, (PAGE_SIZE, 2 * PAGE_SIZE), 0)
    cc = lax.broadcasted_iota(jnp.int32, (PAGE_SIZE, 2 * PAGE_SIZE), 1)
    mw = jnp.where((cc >= PAGE_SIZE) | (rr > cc), 1.0, 0.0).astype(BF16)

    def attend(kts, vts, mask):
        n = len(kts)
        kt = kts[0] if n == 1 else jnp.concatenate(kts, axis=1)
        z = jnp.dot(qbd, kt, preferred_element_type=F32) * scale
        z = z + (bias if n == 1 else jnp.concatenate([bias] * n, axis=1))
        sp = _softplus(z)
        spm = sp if mask is None else jnp.where(mask, sp, 0.0)
        hi, lo = _split2(spm)
        rows = lambda x: jnp.concatenate([x[:, i * PAGE_SIZE:(i + 1) * PAGE_SIZE] for i in range(n)],
                                         axis=0) if n > 1 else x
        r = (jnp.dot(rows(hi), mw, preferred_element_type=F32)
             + jnp.dot(rows(lo), mw, preferred_element_type=F32))
        t = z - sp
        c = c_ref[...]
        parts = []
        for i in range(n):
            ri = r[i * HQ:(i + 1) * HQ]
            parts.append(t[:, i * PAGE_SIZE:(i + 1) * PAGE_SIZE] - ri[:, :PAGE_SIZE] - c)
            c = c + ri[:, PAGE_SIZE:]
        c_ref[...] = c
        a = jnp.exp(parts[0] if n == 1 else jnp.concatenate(parts, axis=1))
        if mask is not None:
            a = jnp.where(mask, a, 0.0)
        vt = vts[0] if n == 1 else jnp.concatenate(vts, axis=1)
        return lax.dot_general(vt, a.astype(BF16), (((1,), (1,)), ((), ())),
                               preferred_element_type=F32)

    @pl.when(j == 0)
    def _():
        c_ref[...] = jnp.zeros(c_ref.shape, F32)
        row_q = lax.broadcasted_iota(jnp.int32, (HQ, PAGE_SIZE), 0) % n_q
        key = lax.broadcasted_iota(jnp.int32, (HQ, PAGE_SIZE), 1)
        valid = (key < row_q) & (key < n_q)
        acc_ref[...] = attend([kn_ref[0].astype(BF16)], [vn_ref[0].astype(BF16)], valid)

    acc_ref[...] += attend([kr[...].reshape(HD, PAGE_SIZE).astype(BF16) for kr in k_refs],
                           [vr[...].reshape(HD, PAGE_SIZE).astype(BF16) for vr in v_refs], None)

    @pl.when(j == pl.num_programs(1) - 1)
    def _():
        o_ref[0] = acc_ref[...]


def sb_sample(q, k_new, v_new, bias, cache_k, cache_v, page_table, layer, *, n_heads, pp=16):
    Bs, n_q, HD = q.shape
    n_pages = page_table.shape[1]
    pp = min(pp, n_pages)
    HQ = n_heads * n_q
    eye = jnp.eye(n_heads, dtype=F32)
    qbd = jnp.einsum('bqhd,hg->bhqgd', q.reshape(Bs, n_q, n_heads, HEAD_DIM), eye)
    qbd = qbd.reshape(Bs, HQ, HD).astype(BF16)
    bias_rows = jnp.broadcast_to(jnp.repeat(bias.astype(F32), n_q)[:, None], (HQ, PAGE_SIZE))
    new_t = lambda a: jnp.pad(a.transpose(0, 2, 1), ((0, 0), (0, 0), (0, PAGE_SIZE - n_q)))
    pool_t = lambda c: c.transpose(0, 1, 3, 4, 2)

    def page_spec(i):
        return pl.BlockSpec(
            (None, None, n_heads, HEAD_DIM, PAGE_SIZE),
            lambda b, j, pt, lay: (lay[0], pt[b, n_pages - 1 - (j * pp + i)], 0, 0, 0))

    new_spec = pl.BlockSpec((1, HD, PAGE_SIZE), lambda b, j, pt, lay: (b, 0, 0))
    grid_spec = pltpu.PrefetchScalarGridSpec(
        num_scalar_prefetch=2,
        grid=(Bs, n_pages // pp),
        in_specs=[pl.BlockSpec((1, HQ, HD), lambda b, j, pt, lay: (b, 0, 0)),
                  pl.BlockSpec((HQ, PAGE_SIZE), lambda b, j, pt, lay: (0, 0)),
                  new_spec, new_spec]
                 + [page_spec(i) for i in range(pp)] + [page_spec(i) for i in range(pp)],
        out_specs=pl.BlockSpec((1, HD, HQ), lambda b, j, pt, lay: (b, 0, 0)),
        scratch_shapes=[pltpu.VMEM((HD, HQ), F32), pltpu.VMEM((HQ, PAGE_SIZE), F32)],
    )
    acc = pl.pallas_call(
        functools.partial(_sb_sample_kernel, pp=pp, n_heads=n_heads, n_q=n_q, scale=HEAD_DIM ** -0.5),
        grid_spec=grid_spec,
        out_shape=jax.ShapeDtypeStruct((Bs, HD, HQ), F32),
        compiler_params=_cparams(("parallel", "arbitrary")),
    )(page_table, jnp.full((1,), layer, jnp.int32), qbd, bias_rows, new_t(k_new), new_t(v_new),
      *([pool_t(cache_k)] * pp), *([pool_t(cache_v)] * pp))
    out = jnp.einsum('bhdgq,hg->bqhd', acc.reshape(Bs, n_heads, HEAD_DIM, n_heads, n_q), eye)
    return out.reshape(Bs, n_q, HD)


def _cumsum_rows(x, tril_bf16):
    C = x.shape[0]
    if C < 16:
        rows = [x[0:1]]
        for i in range(1, C):
            rows.append(rows[-1] + x[i:i + 1])
        return jnp.concatenate(rows, axis=0)
    h1 = x.astype(BF16)
    r1 = x - h1.astype(F32)
    h2 = r1.astype(BF16)
    h3 = (r1 - h2.astype(F32)).astype(BF16)
    dot = lambda h: jnp.dot(tril_bf16, h, preferred_element_type=F32)
    return dot(h1) + dot(h2) + dot(h3)


def _hgrn_kernel(z_ref, lb_ref, ng_ref, s0_ref, o_ref, sT_ref, st_ref, *, C, SB, n_chunks):
    ci = pl.program_id(1)
    W = lb_ref.shape[1]
    n_heads = W // HEAD_DIM
    n_sub = C // SB

    @pl.when(ci == 0)
    def _():
        st_ref[...] = s0_ref[...]

    ones_blk = _head_block_ones(W)
    blockmask = (lax.broadcasted_iota(jnp.int32, (W, W), 0) // HEAD_DIM
                 == lax.broadcasted_iota(jnp.int32, (W, W), 1) // HEAD_DIM)
    lane_head = lax.broadcasted_iota(jnp.int32, (SB, W), 1) // HEAD_DIM
    t_sub = lax.broadcasted_iota(jnp.int32, (SB, W), 0)
    tril = jnp.where(lax.broadcasted_iota(jnp.int32, (C, C), 0)
                     >= lax.broadcasted_iota(jnp.int32, (C, C), 1), 1.0, 0.0).astype(BF16)
    s_col = lax.broadcasted_iota(jnp.int32, (n_heads * SB, C), 1)
    lb = lb_ref[...]
    ng = ng_ref[...]

    def chunk(cc, carry):
        for bb in range(z_ref.shape[0]):
            chunk_of(bb, cc)
        return carry

    def chunk_of(bb, cc):
        r0 = pl.multiple_of(cc * C, C)
        zc = z_ref[bb, pl.ds(r0, C), :]
        q, fz, v, gate = zc[:, :W], zc[:, W:2 * W], zc[:, 2 * W:3 * W], zc[:, 3 * W:]
        f = lb + (1.0 - lb) * _sigmoid(fz)
        k = 1.0 - f
        b = _cumsum_rows(jnp.log(f), tril)
        st = st_ref[bb]
        o_inter = _bdot_t(q * jnp.exp(b), st)
        outs = []
        for I in range(n_sub):
            lo_, hi_ = I * SB, (I + 1) * SB
            bI, qI, kI, vI = b[lo_:hi_], q[lo_:hi_], k[lo_:hi_], v[lo_:hi_]
            ds = []
            for s in range(SB):
                e = jnp.exp(jnp.minimum(bI - bI[s:s + 1], 0.0))
                ds.append(jnp.where(t_sub >= s, e * qI * kI[s:s + 1], 0.0))
            G = jnp.dot(jnp.concatenate(ds, axis=0).astype(BF16), ones_blk,
                        preferred_element_type=F32)
            od = G[0:SB] * vI[0:1]
            for s in range(1, SB):
                od = od + G[s * SB:(s + 1) * SB] * vI[s:s + 1]
            if I > 0:
                rho = b[lo_ - 1:lo_]
                qs = qI * jnp.exp(bI - rho)
                kt = k * jnp.exp(jnp.minimum(rho - b, 0.0))
                qst = jnp.concatenate([jnp.where(lane_head == h, qs, 0.0) for h in range(n_heads)],
                                      axis=0)
                att = jnp.where(s_col < lo_, _bdot_t(qst, kt), 0.0)
                R = _bdot(att, v)
                for h in range(n_heads):
                    od = od + jnp.where(lane_head == h, R[h * SB:(h + 1) * SB], 0.0)
            outs.append(od)
        o = o_inter + (outs[0] if n_sub == 1 else jnp.concatenate(outs, axis=0))
        blast = b[C - 1:C]
        kd = k * jnp.exp(blast - b)
        upd = lax.dot_general(v.astype(BF16), kd.astype(BF16), (((0,), (0,)), ((), ())),
                              preferred_element_type=F32)
        st_ref[bb] = st * jnp.exp(blast) + jnp.where(blockmask, upd, 0.0)
        ms = _dot2(o * o, ones_blk) * (1.0 / HEAD_DIM)
        o_ref[bb, pl.ds(r0, C), :] = o * lax.rsqrt(ms + RMS_EPS) * ng * _silu(gate)

    lax.fori_loop(0, n_chunks, chunk, 0)

    @pl.when(ci == pl.num_programs(1) - 1)
    def _():
        sT_ref[...] = st_ref[...]


def hgrn2(zb, lb, norm_g, s0, *, rows_per_step=256):
    B, L, W4 = zb.shape
    W = W4 // 4
    H = W // HEAD_DIM
    C = math.gcd(L, 64)
    SB = min(16, C)
    rows = min(rows_per_step, L)
    eye = jnp.eye(H, dtype=F32)
    st0 = jnp.einsum('bhdv,hg->bhvgd', s0, eye).reshape(B, W, W)
    nb = 2 if B % 2 == 0 else 1
    o, sT = pl.pallas_call(
        functools.partial(_hgrn_kernel, C=C, SB=SB, n_chunks=rows // C),
        grid=(B // nb, L // rows),
        in_specs=[pl.BlockSpec((nb, rows, W4), lambda b, i: (b, i, 0)),
                  _const_spec((1, W)), _const_spec((1, W)),
                  pl.BlockSpec((nb, W, W), lambda b, i: (b, 0, 0))],
        out_specs=[pl.BlockSpec((nb, rows, W), lambda b, i: (b, i, 0)),
                   pl.BlockSpec((nb, W, W), lambda b, i: (b, 0, 0))],
        out_shape=[jax.ShapeDtypeStruct((B, L, W), F32), jax.ShapeDtypeStruct((B, W, W), F32)],
        scratch_shapes=[pltpu.VMEM((nb, W, W), F32)],
        compiler_params=_cparams(("parallel", "arbitrary")),
    )(zb, lb.reshape(1, W), norm_g.reshape(1, W), st0)
    s5 = sT.reshape(B, H, HEAD_DIM, H, HEAD_DIM)
    s_new = jnp.einsum('bhvgd,hg->bhdv', s5, eye)
    return o, s_new


def _rwkv_pre_kernel(z_ref, sh0_ref, mu_ref, w0_ref, wup_ref, a0_ref, aup_ref, gup_ref,
                     kkw_ref, kaw_ref, rk_ref, *refs, scan_layout):
    if scan_layout:
        src_o, v_o, bonus_o, g_o, shift_o, prev_ref = refs
    else:
        kk_o, w_o, ka_o, k2_o, r_o, v_o, bonus_o, g_o, shift_o, prev_ref = refs
    ti = pl.program_id(1)
    z = z_ref[0]
    tm, P = z.shape
    W = w0_ref.shape[1]

    @pl.when(ti == 0)
    def _():
        prev_ref[...] = sh0_ref[0]

    row = lax.broadcasted_iota(jnp.int32, (tm, P), 0)
    prev = jnp.where(row == 0, prev_ref[...], pltpu.roll(z, 1, axis=0))
    last = z[tm - 1:tm]
    prev_ref[...] = last
    shift_o[0] = last
    zs = z + (prev - z) * mu_ref[...]
    r, k, v, x4 = zs[:, :W], zs[:, W:2 * W], zs[:, 2 * W:3 * W], zs[:, 3 * W:]
    ones_blk = _head_block_ones(W)
    u = w0_ref[...] + _bdot(jnp.tanh(x4), wup_ref[...])
    w = jnp.exp(-jnp.exp(-_softplus(-u) - 0.5))
    a = _sigmoid(a0_ref[...] + _bdot(x4, aup_ref[...]))
    kkr = k * kkw_ref[...]
    kk = kkr / jnp.maximum(jnp.sqrt(_dot2(kkr * kkr, ones_blk)), 1e-12)
    k2 = k * (1.0 + (a - 1.0) * kaw_ref[...])
    scan_ops = (kk, w, kk * a, k2, r)
    if scan_layout:
        low = lax.broadcasted_iota(jnp.int32, (HEAD_DIM, LANES), 1) < HEAD_DIM
        for qi, xq in enumerate(scan_ops):
            for hp in range(W // LANES):
                for c2 in range(tm // LANES):
                    t_ = xq[c2 * LANES:(c2 + 1) * LANES, hp * LANES:(hp + 1) * LANES].T
                    top, bot = t_[:HEAD_DIM], t_[HEAD_DIM:]
                    src_o[hp, 2 * c2, qi] = jnp.where(low, top, pltpu.roll(bot, HEAD_DIM, axis=1))
                    src_o[hp, 2 * c2 + 1, qi] = jnp.where(low, pltpu.roll(top, HEAD_DIM, axis=1), bot)
    else:
        for o_ref, xq in zip((kk_o, w_o, ka_o, k2_o, r_o), scan_ops):
            o_ref[0] = xq
    for hp in range(W // LANES):
        v_o[0, hp] = v[:, hp * LANES:(hp + 1) * LANES]
    bonus_o[0] = _dot2(r * k2 * rk_ref[...], ones_blk) * v
    g_o[0] = _bdot(_sigmoid(x4), gup_ref[...])


def _rwkv_scan_kernel(src_ref, v_ref, s0_ref, o_ref, sT_ref, st_ref, lhs_ref, *, P, steps):
    c = pl.program_id(1)

    @pl.when(c == 0)
    def _():
        st_ref[...] = s0_ref[...]

    n_op = src_ref.shape[2]
    for p in range(P):
        x = src_ref[p, 0].reshape(n_op * HEAD_DIM, LANES)
        hi, lo = _split2(x)
        lhs_ref[p] = jnp.concatenate([hi, lo], axis=1)

    rr = lax.broadcasted_iota(jnp.int32, (2 * LANES, 2 * LANES), 0)
    cc = lax.broadcasted_iota(jnp.int32, (2 * LANES, 2 * LANES), 1)
    t_of_row = jnp.where((rr // HEAD_DIM) % 2 == (cc // HEAD_DIM) % 2,
                         rr % HEAD_DIM - cc // LANES, -1)

    def step2(i, carry):
        t0 = 2 * i
        sel = jnp.where(t_of_row == t0, 1.0, 0.0).astype(BF16)
        for p in range(P):
            cb = jnp.dot(lhs_ref[p], sel, preferred_element_type=F32)
            s = st_ref[p]
            for u in range(2):
                kk, w, ka, k2, r = (cb[i_ * HEAD_DIM:(i_ + 1) * HEAD_DIM, u * LANES:(u + 1) * LANES]
                                    for i_ in range(5))
                skk = jnp.sum(s * kk, axis=0, keepdims=True)
                s = s * w - ka * skk + k2 * v_ref[p, pl.ds(t0 + u, 1), :]
                o_ref[p, pl.ds(t0 + u, 1), :] = jnp.sum(s * r, axis=0, keepdims=True)
            st_ref[p] = s
        return carry

    lax.fori_loop(0, steps // 2, step2, 0)

    @pl.when(c == pl.num_programs(1) - 1)
    def _():
        sT_ref[...] = st_ref[...]


def _rwkv_post_kernel(o_ref, bonus_ref, g_ref, gng_ref, gnb_ref, out_ref):
    o = jnp.concatenate([o_ref[0, hp] for hp in range(o_ref.shape[1])], axis=1)
    ones_blk = _head_block_ones(o.shape[1])
    inv = 1.0 / HEAD_DIM
    d = o - _dot2(o, ones_blk) * inv
    var = _dot2(d * d, ones_blk) * inv
    out_ref[0] = (d * lax.rsqrt(var + GN_EPS) * gng_ref[...] + gnb_ref[...] + bonus_ref[0]) * g_ref[0]


def rwkv7(za, shift0, s0, p, *, tm=512, pairs_per_step=16):
    B, L, P = za.shape
    W = p['rwkv_w0'].shape[0]
    H = W // HEAD_DIM
    HP = W // LANES
    tm = min(tm, L)
    n_low = P - 3 * W
    rank_w, rank_a = p['rwkv_w_up'].shape[0], p['rwkv_a_up'].shape[0]
    pad_rows = lambda m, r0: jnp.zeros((n_low, W), F32).at[r0:r0 + m.shape[0]].set(m).astype(BF16)
    wup = pad_rows(p['rwkv_w_up'], 0)
    aup = pad_rows(p['rwkv_a_up'], rank_w)
    gup = pad_rows(p['rwkv_g_up'], rank_w + rank_a)
    vec = lambda a: a.reshape(1, -1)
    tok = pl.BlockSpec((1, tm, W), lambda b, i: (b, i, 0))
    tok_sd = jax.ShapeDtypeStruct((B, L, W), F32)
    pair_spec = pl.BlockSpec((1, HP, tm, LANES), lambda b, i: (b, 0, i, 0))
    pair_sd = jax.ShapeDtypeStruct((B, HP, L, LANES), F32)
    steps = min(HEAD_DIM, L)
    assert steps % 2 == 0
    NC = L // steps
    NP = B * HP
    scan_layout = tm % LANES == 0
    if scan_layout:
        ops_specs = [pl.BlockSpec((HP, tm // HEAD_DIM, 5, HEAD_DIM, LANES),
                                  lambda b, i: (b, i, 0, 0, 0))]
        ops_sds = [jax.ShapeDtypeStruct((NP, NC, 5, HEAD_DIM, LANES), F32)]
    else:
        ops_specs, ops_sds = [tok] * 5, [tok_sd] * 5
    *ops, v, bonus, g, shift = pl.pallas_call(
        functools.partial(_rwkv_pre_kernel, scan_layout=scan_layout),
        grid=(B, L // tm),
        in_specs=[pl.BlockSpec((1, tm, P), lambda b, i: (b, i, 0)),
                  pl.BlockSpec((1, 1, P), lambda b, i: (b, 0, 0)),
                  _const_spec((1, P)), _const_spec((1, W)), _const_spec((n_low, W)),
                  _const_spec((1, W)), _const_spec((n_low, W)), _const_spec((n_low, W)),
                  _const_spec((1, W)), _const_spec((1, W)), _const_spec((1, W))],
        out_specs=ops_specs + [pair_spec, tok, tok, pl.BlockSpec((1, 1, P), lambda b, i: (b, 0, 0))],
        out_shape=ops_sds + [pair_sd, tok_sd, tok_sd, jax.ShapeDtypeStruct((B, 1, P), F32)],
        scratch_shapes=[pltpu.VMEM((1, P), F32)],
        compiler_params=_cparams(("parallel", "arbitrary")),
    )(za, shift0.reshape(B, 1, P), vec(p['rwkv_mu']), vec(p['rwkv_w0']), wup, vec(p['rwkv_a0']),
      aup, gup, vec(p['rwkv_k_k']), vec(p['rwkv_k_a']), vec(p['rwkv_r_k']))

    if scan_layout:
        src, = ops
    else:
        src = jnp.stack(ops)
        src = src.reshape(5, B, NC, steps, HP, 2, HEAD_DIM).transpose(1, 4, 2, 0, 6, 5, 3)
        src = jnp.pad(src, ((0, 0),) * 6 + ((0, HEAD_DIM - steps),))
        src = src.reshape(NP, NC, 5, HEAD_DIM, LANES)
    st0 = s0.reshape(B, HP, 2, HEAD_DIM, HEAD_DIM).transpose(0, 1, 4, 2, 3).reshape(NP, HEAD_DIM, LANES)
    PP = min(pairs_per_step, NP)
    o, sT = pl.pallas_call(
        functools.partial(_rwkv_scan_kernel, P=PP, steps=steps),
        grid=(NP // PP, NC),
        in_specs=[pl.BlockSpec((PP, 1, 5, HEAD_DIM, LANES), lambda g_, c: (g_, c, 0, 0, 0)),
                  pl.BlockSpec((PP, steps, LANES), lambda g_, c: (g_, c, 0)),
                  pl.BlockSpec((PP, HEAD_DIM, LANES), lambda g_, c: (g_, 0, 0))],
        out_specs=[pl.BlockSpec((PP, steps, LANES), lambda g_, c: (g_, c, 0)),
                   pl.BlockSpec((PP, HEAD_DIM, LANES), lambda g_, c: (g_, 0, 0))],
        out_shape=[jax.ShapeDtypeStruct((NP, L, LANES), F32),
                   jax.ShapeDtypeStruct((NP, HEAD_DIM, LANES), F32)],
        scratch_shapes=[pltpu.VMEM((PP, HEAD_DIM, LANES), F32),
                        pltpu.VMEM((PP, 5 * HEAD_DIM, 2 * LANES), BF16)],
        compiler_params=_cparams(("parallel", "arbitrary")),
    )(src, v.reshape(NP, L, LANES), st0)
    s_new = sT.reshape(B, HP, HEAD_DIM, 2, HEAD_DIM).transpose(0, 1, 3, 4, 2).reshape(B, H, HEAD_DIM, HEAD_DIM)

    out = pl.pallas_call(
        _rwkv_post_kernel,
        grid=(B, L // tm),
        in_specs=[pair_spec, tok, tok, _const_spec((1, W)), _const_spec((1, W))],
        out_specs=tok,
        out_shape=tok_sd,
        compiler_params=_cparams(("parallel", "parallel")),
    )(o.reshape(B, HP, L, LANES), bonus, g, vec(p['rwkv_gn_g']), vec(p['rwkv_gn_b']))
    return out, s_new, shift.reshape(B, P)


def _route_kernel(x_ref, rt_ref, g_ref, s_ref):
    logits = lax.dot_general(rt_ref[...], x_ref[...], (((1,), (1,)), ((), ())),
                             precision=lax.Precision.HIGHEST, preferred_element_type=F32)
    n_e = logits.shape[0]
    e_id = lax.broadcasted_iota(jnp.int32, logits.shape, 0)
    m1 = jnp.max(logits, axis=0, keepdims=True)
    i1 = jnp.min(jnp.where(logits == m1, e_id, n_e), axis=0, keepdims=True)
    rest = jnp.where(e_id == i1, -jnp.inf, logits)
    m2 = jnp.max(rest, axis=0, keepdims=True)
    i2 = jnp.min(jnp.where(rest == m2, e_id, n_e), axis=0, keepdims=True)
    t = jnp.exp(m2 - m1)
    g1 = 1.0 / (1.0 + t)
    g2 = t / (1.0 + t)
    g_ref[...] = jnp.where(e_id == i1, g1, jnp.where(e_id == i2, g2, 0.0))
    s_ref[...] = jnp.where(e_id == i1, 1.0, jnp.where(e_id == i2, 1.0, 0.0))


def _moe_kernel(xb_ref, g_ref, s_ref, wg_ref, wu_ref, wd_ref, o_ref, rank_ref, xg_ref, y_ref, *, R):
    e = pl.program_id(1)
    f = pl.program_id(2)
    n_f = pl.num_programs(2)
    tm = xb_ref.shape[0]

    @pl.when((e == 0) & (f == 0))
    def _():
        before = jnp.where(lax.broadcasted_iota(jnp.int32, (tm, tm), 0)
                           < lax.broadcasted_iota(jnp.int32, (tm, tm), 1), 1.0, 0.0).astype(BF16)
        rank_ref[...] = jnp.dot(s_ref[...].astype(BF16), before, preferred_element_type=F32)
        o_ref[...] = jnp.zeros(o_ref.shape, F32)

    sel = s_ref[pl.ds(e, 1), :]
    key = jnp.where(sel > 0.0, rank_ref[pl.ds(e, 1), :], -1.0)
    n_chunks = (jnp.sum(sel).astype(jnp.int32) + (R - 1)) // R
    r_id = lax.broadcasted_iota(jnp.int32, (R, tm), 0)

    def one_hot(c):
        return jnp.where(key == (r_id + c * R).astype(F32), 1.0, 0.0).astype(BF16)

    def rows_of(c):
        return pl.ds(pl.multiple_of(c * R, math.gcd(R, 256)), R)

    @pl.when(f == 0)
    def _():
        def gather(c, carry):
            xg_ref[rows_of(c), :] = jnp.dot(one_hot(c), xb_ref[...],
                                            preferred_element_type=F32).astype(BF16)
            return carry
        lax.fori_loop(0, n_chunks, gather, 0)

    def expert(c, carry):
        xg = xg_ref[rows_of(c), :]
        gate = jnp.dot(xg, wg_ref[...], preferred_element_type=F32)
        up = jnp.dot(xg, wu_ref[...], preferred_element_type=F32)
        y = _bdot(_silu(gate) * up, wd_ref[...])

        @pl.when(f == 0)
        def _():
            y_ref[rows_of(c), :] = y

        @pl.when(f > 0)
        def _():
            y_ref[rows_of(c), :] += y
        return carry

    lax.fori_loop(0, n_chunks, expert, 0)

    @pl.when(f == n_f - 1)
    def _():
        gt = g_ref[pl.ds(e, 1), :]
        h1 = gt.astype(BF16)
        r1 = gt - h1.astype(F32)
        h2 = r1.astype(BF16)
        h3 = (r1 - h2.astype(F32)).astype(BF16)
        g3 = jnp.concatenate([h1, h2, h3, jnp.zeros((5, tm), BF16)], axis=0)

        def scatter(c, carry):
            p = one_hot(c)
            g_row = jnp.sum(lax.dot_general(p, g3, (((1,), (1,)), ((), ())),
                                            preferred_element_type=F32), axis=1, keepdims=True)
            yw = (y_ref[rows_of(c), :] * g_row).astype(BF16)
            o_ref[...] += lax.dot_general(p, yw, (((0,), (0,)), ((), ())),
                                          preferred_element_type=F32)
            return carry
        lax.fori_loop(0, n_chunks, scatter, 0)


def moe_ffn(x, router, wg_bf16, wu_bf16, wd_bf16, *, tm=1024, tf=896, R=288):
    T, D = x.shape
    n_e, _, F = wg_bf16.shape
    tm = min(tm, T)
    R = min(R, tm)
    gates, sel = pl.pallas_call(
        _route_kernel,
        grid=(T // tm,),
        in_specs=[pl.BlockSpec((tm, D), lambda i: (i, 0)), _const_spec((n_e, D))],
        out_specs=[pl.BlockSpec((n_e, tm), lambda i: (0, i))] * 2,
        out_shape=[jax.ShapeDtypeStruct((n_e, T), F32)] * 2,
        compiler_params=_cparams(("parallel",)),
    )(x, router.T)
    return pl.pallas_call(
        functools.partial(_moe_kernel, R=R),
        grid=(T // tm, n_e, F // tf),
        in_specs=[pl.BlockSpec((tm, D), lambda i, e, f: (i, 0)),
                  pl.BlockSpec((n_e, tm), lambda i, e, f: (0, i)),
                  pl.BlockSpec((n_e, tm), lambda i, e, f: (0, i)),
                  pl.BlockSpec((None, D, tf), lambda i, e, f: (e, 0, f)),
                  pl.BlockSpec((None, D, tf), lambda i, e, f: (e, 0, f)),
                  pl.BlockSpec((None, tf, D), lambda i, e, f: (e, f, 0))],
        out_specs=pl.BlockSpec((tm, D), lambda i, e, f: (i, 0)),
        out_shape=jax.ShapeDtypeStruct((T, D), F32),
        scratch_shapes=[pltpu.VMEM((n_e, tm), F32), pltpu.VMEM((pl.cdiv(tm, R) * R, D), BF16),
                        pltpu.VMEM((pl.cdiv(tm, R) * R, D), F32)],
        compiler_params=_cparams(("parallel", "arbitrary", "arbitrary")),
    )(x.astype(BF16), gates, sel, wg_bf16, wu_bf16, wd_bf16)


def _layer(x, seq_len, p, lb, states, sb_past, mem_k, mem_v, ffn, alpha, layer, depth, kv_stack):
    T, D = x.shape
    B = T // seq_len
    W = p['rwkv_w0'].shape[0]
    pa = p['rwkv_mu'].shape[0]
    wc = p['sb_bias'].shape[0] * HEAD_DIM
    x, za, zb, q, k, v, k4, v4 = in_proj(x, p['ln_in_g'], p['ln_in_b'], p['w_in'], kv_stack, layer,
                                         depth, apply_ln=layer == 0, widths=(pa, 4 * W, wc, wc, wc))
    to3 = lambda a: a.reshape(B, seq_len, a.shape[-1])
    o_a, rwkv_s, shift = rwkv7(to3(za), states[1], states[0], p)
    o_b, hgrn_s = hgrn2(to3(zb), lb, p['hgrn_norm_g'], states[2])
    n_heads = p['sb_bias'].shape[0]
    if sb_past is None:
        o_c = sb_prompt(to3(q), to3(k), to3(v), p['sb_bias'], n_heads=n_heads)
    else:
        o_c = sb_sample(to3(q), to3(k), to3(v), p['sb_bias'], *sb_past, n_heads=n_heads)
    x = mix_out(o_a.reshape(T, W), o_b.reshape(T, W), o_c.reshape(T, wc), p['w_out'], x,
                p['ln_mix_g'], p['ln_mix_b'], alpha=alpha)
    mem_layer = layer if mem_k.shape[0] == depth else 0
    x = mem_block(x, p['mem_wq'], mem_k, mem_v, p['mem_wo'], p['ln_mem_g'], p['ln_mem_b'],
                  alpha=alpha, seq_len=seq_len, layer=mem_layer)
    x = ffn(x)
    return x, (rwkv_s, shift, hgrn_s, (k4, v4))


def kernel(x_prompt, x_sample, cache_sb_k, cache_sb_v, state_rwkv, state_rwkv_shift, state_hgrn,
           cache_mem_k, cache_mem_v, page_table, mem_prompt, ln_in_g, ln_in_b, w_in, rwkv_mu, rwkv_w0,
           rwkv_w_up, rwkv_a0, rwkv_a_up, rwkv_g_up, rwkv_k_k, rwkv_k_a, rwkv_r_k, rwkv_gn_g, rwkv_gn_b,
           hgrn_lb, hgrn_norm_g, sb_bias, w_out, ln_mix_g, ln_mix_b, mem_wq, mem_wk, mem_wv, mem_wo,
           ln_mem_g, ln_mem_b, ffn_w_gate, ffn_w_up, ffn_w_down, moe_router, moe_w_gate, moe_w_up,
           moe_w_down, ln_ffn_g, ln_ffn_b):
    B, L, D = x_prompt.shape
    Bs, Ls, _ = x_sample.shape
    depth = w_in.shape[0]
    H = state_rwkv.shape[2]
    n_heads_c = sb_bias.shape[1]
    n_mem = mem_prompt.shape[1]
    alpha = (2 * depth) ** 0.25
    bf = lambda a: a.astype(BF16)

    lb_sm = jax.nn.softmax(hgrn_lb.astype(F32), axis=0)
    lb_all = jnp.cumsum(lb_sm, axis=0) - lb_sm[0]

    mem_flat = mem_prompt.reshape(B * n_mem, D)
    zeros_p = (jnp.zeros((B, H, HEAD_DIM, HEAD_DIM), F32), jnp.zeros((B, rwkv_mu.shape[1]), F32),
               jnp.zeros((B, H, HEAD_DIM, HEAD_DIM), F32))

    xp = x_prompt.reshape(B * L, D)
    xs = x_sample.reshape(Bs * Ls, D)
    outs_p = [[] for _ in range(5)]
    outs_s = [[] for _ in range(3)]
    stacks = lambda n_tok: tuple(jnp.zeros((depth * n_tok * n_heads_c, HEAD_DIM), F32)
                                 for _ in range(2))
    kv_p, kv_s = stacks(B * L), stacks(Bs * Ls)
    for l in range(depth):
        p = dict(ln_in_g=ln_in_g, ln_in_b=ln_in_b, w_in=bf(w_in[l]), w_out=bf(w_out[l]),
                 rwkv_mu=rwkv_mu[l], rwkv_w0=rwkv_w0[l], rwkv_w_up=rwkv_w_up[l], rwkv_a0=rwkv_a0[l],
                 rwkv_a_up=rwkv_a_up[l], rwkv_g_up=rwkv_g_up[l], rwkv_k_k=rwkv_k_k[l],
                 rwkv_k_a=rwkv_k_a[l], rwkv_r_k=rwkv_r_k[l], rwkv_gn_g=rwkv_gn_g[l],
                 rwkv_gn_b=rwkv_gn_b[l], hgrn_norm_g=hgrn_norm_g[l], sb_bias=sb_bias[l],
                 ln_mix_g=ln_mix_g[l], ln_mix_b=ln_mix_b[l], mem_wq=bf(mem_wq[l]),
                 mem_wo=bf(mem_wo[l]), ln_mem_g=ln_mem_g[l], ln_mem_b=ln_mem_b[l])
        j = l // 2
        if l % 2 == 0:
            wg, wu, wd = bf(ffn_w_gate[j]), bf(ffn_w_up[j]), bf(ffn_w_down[j])
            ffn = lambda x, wg=wg, wu=wu, wd=wd, l=l: ffn_block(
                x, wg, wu, wd, ln_ffn_g[l], ln_ffn_b[l], alpha=alpha)
        else:
            wg, wu, wd = bf(moe_w_gate[j]), bf(moe_w_up[j]), bf(moe_w_down[j])
            ffn = lambda x, wg=wg, wu=wu, wd=wd, j=j, l=l: res_ln(
                x, moe_ffn(x, moe_router[j], wg, wu, wd), ln_ffn_g[l], ln_ffn_b[l], alpha=alpha)
        mk, mv = matmul2(mem_flat, bf(mem_wk[l]), bf(mem_wv[l]))
        mk = mk.reshape(1, B, n_mem, D)
        mv = mv.reshape(1, B, n_mem, D)
        xp, (s_a, sh, s_b, kv_p) = _layer(xp, L, p, lb_all[l], zeros_p, None, mk, mv, ffn,
                                          alpha, l, depth, kv_p)
        for lst, val in zip(outs_p, (s_a, sh, s_b,
                                     mk.reshape(B, n_mem, N_MEM_HEADS, D // N_MEM_HEADS),
                                     mv.reshape(B, n_mem, N_MEM_HEADS, D // N_MEM_HEADS))):
            lst.append(val)
        states = (state_rwkv[l], state_rwkv_shift[l], state_hgrn[l])
        xs, (s_a, sh, s_b, kv_s) = _layer(
            xs, Ls, p, lb_all[l], states, (cache_sb_k, cache_sb_v, page_table, l),
            cache_mem_k.reshape(depth, Bs, n_mem, D), cache_mem_v.reshape(depth, Bs, n_mem, D), ffn,
            alpha, l, depth, kv_s)
        for lst, val in zip(outs_s, (s_a, sh, s_b)):
            lst.append(val)
    kv5 = lambda a, nb, sl: a.reshape(depth, nb, sl, n_heads_c, HEAD_DIM)
    return (xp.reshape(B, L, D), xs.reshape(Bs, Ls, D),
            kv5(kv_p[0], B, L), kv5(kv_p[1], B, L), *(jnp.stack(o) for o in outs_p),
            kv5(kv_s[0], Bs, Ls), kv5(kv_s[1], Bs, Ls), *(jnp.stack(o) for o in outs_s))
```

```python
import functools
import math

import jax
import jax.numpy as jnp
from jax import lax
from jax.experimental import pallas as pl
from jax.experimental.pallas import tpu as pltpu

F32 = jnp.float32
BF16 = jnp.bfloat16

HEAD_DIM = 64
LANES = 128
PAGE_SIZE = 128
N_MEM_HEADS = 4
TOP_K = 2
LN_EPS = 1e-5
GN_EPS = 64e-5
RMS_EPS = 1e-6
VMEM_LIMIT = 56 * 1024 * 1024


def _cparams(sem):
    return pltpu.CompilerParams(dimension_semantics=sem, vmem_limit_bytes=VMEM_LIMIT)


def _const_spec(shape):
    nd = len(shape)
    return pl.BlockSpec(shape, lambda *_: (0,) * nd, pipeline_mode=pl.Buffered(1))


def _bdot(a, b):
    return jnp.dot(a.astype(BF16), b.astype(BF16), preferred_element_type=F32)


def _bdot_t(a, b):
    return lax.dot_general(a.astype(BF16), b.astype(BF16), (((1,), (1,)), ((), ())),
                           preferred_element_type=F32)


def _split2(x):
    hi = x.astype(BF16)
    lo = (x - hi.astype(F32)).astype(BF16)
    return hi, lo


def _dot2(x, w_bf16):
    hi, lo = _split2(x)
    return (jnp.dot(hi, w_bf16, preferred_element_type=F32)
            + jnp.dot(lo, w_bf16, preferred_element_type=F32))


def _layer_norm(x, g, b):
    mu = jnp.mean(x, -1, keepdims=True)
    xc = x - mu
    var = jnp.mean(xc * xc, -1, keepdims=True)
    return xc * lax.rsqrt(var + LN_EPS) * g + b


def _sigmoid(x):
    return 1.0 / (1.0 + jnp.exp(-x))


def _silu(x):
    return x * _sigmoid(x)


def _softplus(x):
    return jnp.maximum(x, 0.0) + jnp.log(1.0 + jnp.exp(-jnp.abs(x)))


def _head_block_ones(width):
    r = lax.broadcasted_iota(jnp.int32, (width, width), 0) // HEAD_DIM
    c = lax.broadcasted_iota(jnp.int32, (width, width), 1) // HEAD_DIM
    return jnp.where(r == c, 1.0, 0.0).astype(BF16)


def _in_proj_kernel(x_ref, g_ref, b_ref, w_ref, k4_in, v4_in,
                    xn_ref, za_ref, zb_ref, q_ref, k_ref, v_ref, k4_ref, v4_ref, *, apply_ln, n_heads):
    del k4_in, v4_in
    x = x_ref[...]
    if apply_ln:
        x = _layer_norm(x, g_ref[...], b_ref[...])
    xn_ref[...] = x
    z = _bdot(x, w_ref[...])
    c0 = 0
    for z_ref in (za_ref, zb_ref, q_ref, k_ref, v_ref):
        z_ref[...] = z[:, c0:c0 + z_ref.shape[1]]
        c0 += z_ref.shape[1]
    tm = x.shape[0]
    hd = k_ref.shape[1]
    for src0, dst in ((c0 - 2 * hd, k4_ref), (c0 - hd, v4_ref)):
        for h in range(n_heads):
            dst[pl.ds(h, tm, stride=n_heads), :] = z[:, src0 + h * HEAD_DIM:src0 + (h + 1) * HEAD_DIM]


def in_proj(x, g, b, w_bf16, kv_stack, layer, depth, *, apply_ln, widths, tm=512):
    T, D = x.shape
    N = w_bf16.shape[1]
    assert sum(widths) == N and all(wd % LANES == 0 for wd in widths)
    n_heads = widths[-1] // HEAD_DIM
    tm = min(tm, T)
    nt = T // tm
    row = lambda width: pl.BlockSpec((tm, width), lambda i: (i, 0))
    stack_spec = pl.BlockSpec((tm * n_heads, HEAD_DIM), lambda i: (layer * nt + i, 0))
    stack_sd = jax.ShapeDtypeStruct((depth * T * n_heads, HEAD_DIM), F32)
    return pl.pallas_call(
        functools.partial(_in_proj_kernel, apply_ln=apply_ln, n_heads=n_heads),
        grid=(nt,),
        in_specs=[row(D), _const_spec((1, D)), _const_spec((1, D)), _const_spec((D, N))]
                 + [pl.BlockSpec(memory_space=pl.ANY)] * 2,
        out_specs=[row(D)] + [row(wd) for wd in widths] + [stack_spec, stack_spec],
        out_shape=[jax.ShapeDtypeStruct((T, D), F32)]
                  + [jax.ShapeDtypeStruct((T, wd), F32) for wd in widths] + [stack_sd, stack_sd],
        input_output_aliases={4: 6, 5: 7},
        compiler_params=_cparams(("parallel",)),
    )(x, g.reshape(1, D), b.reshape(1, D), w_bf16, *kv_stack)


def _matmul2_kernel(x_ref, w1_ref, w2_ref, o1_ref, o2_ref):
    xb = x_ref[...].astype(BF16)
    o1_ref[...] = jnp.dot(xb, w1_ref[...], preferred_element_type=F32)
    o2_ref[...] = jnp.dot(xb, w2_ref[...], preferred_element_type=F32)


def matmul2(x, w1_bf16, w2_bf16, tm=512):
    T, K = x.shape
    N = w1_bf16.shape[1]
    tm = min(tm, T)
    out = pl.BlockSpec((tm, N), lambda i: (i, 0))
    return pl.pallas_call(
        _matmul2_kernel,
        grid=(T // tm,),
        in_specs=[pl.BlockSpec((tm, K), lambda i: (i, 0)), _const_spec((K, N)), _const_spec((K, N))],
        out_specs=[out, out],
        out_shape=[jax.ShapeDtypeStruct((T, N), F32)] * 2,
        compiler_params=_cparams(("parallel",)),
    )(x, w1_bf16, w2_bf16)


def _mix_out_kernel(oa_ref, ob_ref, oc_ref, wa_ref, wb_ref, wc_ref, res_ref, g_ref, b_ref, o_ref,
                    *, alpha):
    h = (_bdot(oa_ref[...], wa_ref[...]) + _bdot(ob_ref[...], wb_ref[...])
         + _bdot(oc_ref[...], wc_ref[...]))
    o_ref[...] = _layer_norm(alpha * res_ref[...] + h, g_ref[...], b_ref[...])


def mix_out(oa, ob, oc, w_out_bf16, res, g, b, *, alpha, tm=512):
    T, D = res.shape
    wa, wb, wc = oa.shape[1], ob.shape[1], oc.shape[1]
    tm = min(tm, T)
    row = lambda width: pl.BlockSpec((tm, width), lambda i: (i, 0))
    return pl.pallas_call(
        functools.partial(_mix_out_kernel, alpha=alpha),
        grid=(T // tm,),
        in_specs=[row(wa), row(wb), row(wc),
                  _const_spec((wa, D)), _const_spec((wb, D)), _const_spec((wc, D)),
                  row(D), _const_spec((1, D)), _const_spec((1, D))],
        out_specs=row(D),
        out_shape=jax.ShapeDtypeStruct((T, D), F32),
        compiler_params=_cparams(("parallel",)),
    )(oa, ob, oc, w_out_bf16[:wa], w_out_bf16[wa:wa + wb], w_out_bf16[wa + wb:], res,
      g.reshape(1, D), b.reshape(1, D))


def _res_ln_kernel(x_ref, y_ref, g_ref, b_ref, o_ref, *, alpha):
    o_ref[...] = _layer_norm(alpha * x_ref[...] + y_ref[...], g_ref[...], b_ref[...])


def res_ln(x, y, g, b, *, alpha, tm=1024):
    T, D = x.shape
    tm = min(tm, T)
    row = pl.BlockSpec((tm, D), lambda i: (i, 0))
    return pl.pallas_call(
        functools.partial(_res_ln_kernel, alpha=alpha),
        grid=(T // tm,),
        in_specs=[row, row, _const_spec((1, D)), _const_spec((1, D))],
        out_specs=row,
        out_shape=jax.ShapeDtypeStruct((T, D), F32),
        compiler_params=_cparams(("parallel",)),
    )(x, y, g.reshape(1, D), b.reshape(1, D))


def _mem_kernel(x_ref, wq_ref, mk_ref, mv_ref, wo_ref, g_ref, b_ref, o_ref, att_ref,
                *, alpha, nb, rows):
    x = x_ref[...]
    q = _bdot(x, wq_ref[...])
    D = q.shape[1]
    dh = D // N_MEM_HEADS
    scale = dh ** -0.5
    for bi in range(nb):
        r0 = bi * rows
        for h in range(N_MEM_HEADS):
            c0 = h * dh
            qh = q[r0:r0 + rows, c0:c0 + dh]
            kh = mk_ref[bi, :, c0:c0 + dh]
            vh = mv_ref[bi, :, c0:c0 + dh]
            s = _bdot_t(qh, kh) * scale
            s = s - jnp.max(s, -1, keepdims=True)
            p = jnp.exp(s)
            p = p / jnp.sum(p, -1, keepdims=True)
            att_ref[r0:r0 + rows, c0:c0 + dh] = _bdot(p, vh)
    y = _bdot(att_ref[...], wo_ref[...])
    o_ref[...] = _layer_norm(alpha * x + y, g_ref[...], b_ref[...])


def mem_block(x, wq_bf16, mk, mv, wo_bf16, g, b, *, alpha, seq_len, layer=0, tm=512):
    T, D = x.shape
    _, B, n_mem, _ = mk.shape
    if seq_len >= tm:
        nb, rows = 1, tm
        mem_map = lambda i: (layer, i // (seq_len // tm), 0, 0)
    else:
        nb, rows = min(B, 8), seq_len
        tm = nb * rows
        mem_map = lambda i: (layer, i, 0, 0)
    row = pl.BlockSpec((tm, D), lambda i: (i, 0))
    return pl.pallas_call(
        functools.partial(_mem_kernel, alpha=alpha, nb=nb, rows=rows),
        grid=(T // tm,),
        in_specs=[row, _const_spec((D, D)),
                  pl.BlockSpec((None, nb, n_mem, D), mem_map),
                  pl.BlockSpec((None, nb, n_mem, D), mem_map),
                  _const_spec((D, D)), _const_spec((1, D)), _const_spec((1, D))],
        out_specs=row,
        out_shape=jax.ShapeDtypeStruct((T, D), F32),
        scratch_shapes=[pltpu.VMEM((tm, D), F32)],
        compiler_params=_cparams(("parallel",)),
    )(x, wq_bf16, mk, mv, wo_bf16, g.reshape(1, D), b.reshape(1, D))


def _ffn_kernel(x_ref, wg_ref, wu_ref, wd_ref, g_ref, b_ref, o_ref, *, alpha, tf):
    x = x_ref[...]
    xb = x.astype(BF16)
    F = wg_ref.shape[1]
    y = jnp.zeros(x.shape, F32)
    for f0 in range(0, F, tf):
        gate = jnp.dot(xb, wg_ref[:, f0:f0 + tf], preferred_element_type=F32)
        up = jnp.dot(xb, wu_ref[:, f0:f0 + tf], preferred_element_type=F32)
        y = y + _bdot(_silu(gate) * up, wd_ref[f0:f0 + tf, :])
    o_ref[...] = _layer_norm(alpha * x + y, g_ref[...], b_ref[...])


def ffn_block(x, wg_bf16, wu_bf16, wd_bf16, g, b, *, alpha, tm=512, tf=256):
    T, D = x.shape
    F = wg_bf16.shape[1]
    tm = min(tm, T)
    row = pl.BlockSpec((tm, D), lambda i: (i, 0))
    return pl.pallas_call(
        functools.partial(_ffn_kernel, alpha=alpha, tf=tf),
        grid=(T // tm,),
        in_specs=[row, _const_spec((D, F)), _const_spec((D, F)), _const_spec((F, D)),
                  _const_spec((1, D)), _const_spec((1, D))],
        out_specs=row,
        out_shape=jax.ShapeDtypeStruct((T, D), F32),
        compiler_params=_cparams(("parallel",)),
    )(x, wg_bf16, wu_bf16, wd_bf16, g.reshape(1, D), b.reshape(1, D))


def _sb_prompt_kernel(bias_ref, q_ref, k_ref, v_ref, o_ref, acc_ref, c_ref, *, tq, tk, scale):
    hp = pl.program_id(1)
    qi = pl.program_id(2)
    n_sub = tq // tk
    q = q_ref[0] * scale
    half = lax.broadcasted_iota(jnp.int32, (tq, LANES), 1) // HEAD_DIM
    qm = [jnp.where(half == h2, q, 0.0).astype(BF16) for h2 in range(2)]
    bias = [bias_ref[2 * hp + h2] for h2 in range(2)]
    rr = lax.broadcasted_iota(jnp.int32, (tk, 2 * tk), 0)
    cc = lax.broadcasted_iota(jnp.int32, (tk, 2 * tk), 1)
    mw = jnp.where((cc >= tk) | (rr > cc), 1.0, 0.0).astype(BF16)
    causal = (lax.broadcasted_iota(jnp.int32, (tq, tq), 1)
              < lax.broadcasted_iota(jnp.int32, (tq, tq), 0))
    acc_ref[...] = jnp.zeros(acc_ref.shape, F32)
    c_ref[...] = jnp.zeros(c_ref.shape, F32)

    def block(j, masked):
        start = pl.multiple_of(j * tq, tq)
        kb = k_ref[0, pl.ds(start, tq), :].astype(BF16)
        vb = v_ref[0, pl.ds(start, tq), :].astype(BF16)
        for h2 in range(2):
            z = _bdot_t(qm[h2], kb) + bias[h2]
            sp = _softplus(z)
            spm = (jnp.where(causal, sp, 0.0) if masked else sp).astype(BF16)
            c = c_ref[h2]
            parts = [None] * n_sub
            for s in reversed(range(n_sub)):
                sl = slice(s * tk, (s + 1) * tk)
                r = jnp.dot(spm[:, sl], mw, preferred_element_type=F32)
                parts[s] = z[:, sl] - sp[:, sl] - r[:, :tk] - c
                c = c + r[:, tk:]
            c_ref[h2] = c
            a = jnp.exp(parts[0] if n_sub == 1 else jnp.concatenate(parts, axis=1))
            if masked:
                a = jnp.where(causal, a, 0.0)
            acc_ref[h2] += jnp.dot(a.astype(BF16), vb, preferred_element_type=F32)

    block(qi, True)

    def body(i, carry):
        block(qi - 1 - i, False)
        return carry

    lax.fori_loop(0, qi, body, 0)
    o_ref[0] = jnp.where(half == 0, acc_ref[0], acc_ref[1])


def sb_prompt(q, k, v, bias, *, n_heads, tq=512, tk=128):
    B, L, _ = q.shape
    tq = min(tq, L)
    tk = min(tk, tq)
    grid_spec = pltpu.PrefetchScalarGridSpec(
        num_scalar_prefetch=1,
        grid=(B, n_heads // 2, L // tq),
        in_specs=[pl.BlockSpec((1, tq, LANES), lambda b, hp, qi, bias: (b, qi, hp)),
                  pl.BlockSpec((1, L, LANES), lambda b, hp, qi, bias: (b, 0, hp)),
                  pl.BlockSpec((1, L, LANES), lambda b, hp, qi, bias: (b, 0, hp))],
        out_specs=pl.BlockSpec((1, tq, LANES), lambda b, hp, qi, bias: (b, qi, hp)),
        scratch_shapes=[pltpu.VMEM((2, tq, LANES), F32), pltpu.VMEM((2, tq, tk), F32)],
    )
    return pl.pallas_call(
        functools.partial(_sb_prompt_kernel, tq=tq, tk=tk, scale=HEAD_DIM ** -0.5),
        grid_spec=grid_spec,
        out_shape=jax.ShapeDtypeStruct((B, L, n_heads * HEAD_DIM), F32),
        compiler_params=_cparams(("parallel", "parallel", "arbitrary")),
    )(bias.astype(F32), q, k, v)


def _sb_sample_kernel(pt_ref, lay_ref, qbd_ref, bias_ref, kn_ref, vn_ref, *rest,
                      pp, n_heads, n_q, scale):
    k_refs, v_refs = rest[:pp], rest[pp:2 * pp]
    o_ref, acc_ref, c_ref = rest[2 * pp:]
    j = pl.program_id(1)
    qbd = qbd_ref[0]
    bias = bias_ref[...]
    HQ = qbd.shape[0]
    HD = qbd.shape[1]
    rr = lax.broadcasted_iota(jnp.int32, (PAGE_SIZE, 2 * PAGE_SIZE), 0)
    cc = lax.broadcasted_iota(jnp.int32, (PAGE_SIZE, 2 * PAGE_SIZE), 1)
    mw = jnp.where((cc >= PAGE_SIZE) | (rr > cc), 1.0, 0.0).astype(BF16)

    def attend(kts, vts, mask):
        n = len(kts)
        kt = kts[0] if n == 1 else jnp.concatenate(kts, axis=1)
        z = jnp.dot(qbd, kt, preferred_element_type=F32) * scale
        z = z + (bias if n == 1 else jnp.concatenate([bias] * n, axis=1))
        sp = _softplus(z)
        spm = sp if mask is None else jnp.where(mask, sp, 0.0)
        hi, lo = _split2(spm)
        rows = lambda x: jnp.concatenate([x[:, i * PAGE_SIZE:(i + 1) * PAGE_SIZE] for i in range(n)],
                                         axis=0) if n > 1 else x
        r = (jnp.dot(rows(hi), mw, preferred_element_type=F32)
             + jnp.dot(rows(lo), mw, preferred_element_type=F32))
        t = z - sp
        c = c_ref[...]
        parts = []
        for i in range(n):
            ri = r[i * HQ:(i + 1) * HQ]
            parts.append(t[:, i * PAGE_SIZE:(i + 1) * PAGE_SIZE] - ri[:, :PAGE_SIZE] - c)
            c = c + ri[:, PAGE_SIZE:]
        c_ref[...] = c
        a = jnp.exp(parts[0] if n == 1 else jnp.concatenate(parts, axis=1))
        if mask is not None:
            a = jnp.where(mask, a, 0.0)
        vt = vts[0] if n == 1 else jnp.concatenate(vts, axis=1)
        return lax.dot_general(vt, a.astype(BF16), (((1,), (1,)), ((), ())),
                               preferred_element_type=F32)

    @pl.when(j == 0)
    def _():
        c_ref[...] = jnp.zeros(c_ref.shape, F32)
        row_q = lax.broadcasted_iota(jnp.int32, (HQ, PAGE_SIZE), 0) % n_q
        key = lax.broadcasted_iota(jnp.int32, (HQ, PAGE_SIZE), 1)
        valid = (key < row_q) & (key < n_q)
        acc_ref[...] = attend([kn_ref[0].astype(BF16)], [vn_ref[0].astype(BF16)], valid)

    acc_ref[...] += attend([kr[...].reshape(HD, PAGE_SIZE).astype(BF16) for kr in k_refs],
                           [vr[...].reshape(HD, PAGE_SIZE).astype(BF16) for vr in v_refs], None)

    @pl.when(j == pl.num_programs(1) - 1)
    def _():
        o_ref[0] = acc_ref[...]


def sb_sample(q, k_new, v_new, bias, cache_k, cache_v, page_table, layer, *, n_heads, pp=16):
    Bs, n_q, HD = q.shape
    n_pages = page_table.shape[1]
    pp = min(pp, n_pages)
    HQ = n_heads * n_q
    eye = jnp.eye(n_heads, dtype=F32)
    qbd = jnp.einsum('bqhd,hg->bhqgd', q.reshape(Bs, n_q, n_heads, HEAD_DIM), eye)
    qbd = qbd.reshape(Bs, HQ, HD).astype(BF16)
    bias_rows = jnp.broadcast_to(jnp.repeat(bias.astype(F32), n_q)[:, None], (HQ, PAGE_SIZE))
    new_t = lambda a: jnp.pad(a.transpose(0, 2, 1), ((0, 0), (0, 0), (0, PAGE_SIZE - n_q)))
    pool_t = lambda c: c.transpose(0, 1, 3, 4, 2)

    def page_spec(i):
        return pl.BlockSpec(
            (None, None, n_heads, HEAD_DIM, PAGE_SIZE),
            lambda b, j, pt, lay: (lay[0], pt[b, n_pages - 1 - (j * pp + i)], 0, 0, 0))

    new_spec = pl.BlockSpec((1, HD, PAGE_SIZE), lambda b, j, pt, lay: (b, 0, 0))
    grid_spec = pltpu.PrefetchScalarGridSpec(
        num_scalar_prefetch=2,
        grid=(Bs, n_pages // pp),
        in_specs=[pl.BlockSpec((1, HQ, HD), lambda b, j, pt, lay: (b, 0, 0)),
                  pl.BlockSpec((HQ, PAGE_SIZE), lambda b, j, pt, lay: (0, 0)),
                  new_spec, new_spec]
                 + [page_spec(i) for i in range(pp)] + [page_spec(i) for i in range(pp)],
        out_specs=pl.BlockSpec((1, HD, HQ), lambda b, j, pt, lay: (b, 0, 0)),
        scratch_shapes=[pltpu.VMEM((HD, HQ), F32), pltpu.VMEM((HQ, PAGE_SIZE), F32)],
    )
    acc = pl.pallas_call(
        functools.partial(_sb_sample_kernel, pp=pp, n_heads=n_heads, n_q=n_q, scale=HEAD_DIM ** -0.5),
        grid_spec=grid_spec,
        out_shape=jax.ShapeDtypeStruct((Bs, HD, HQ), F32),
        compiler_params=_cparams(("parallel", "arbitrary")),
    )(page_table, jnp.full((1,), layer, jnp.int32), qbd, bias_rows, new_t(k_new), new_t(v_new),
      *([pool_t(cache_k)] * pp), *([pool_t(cache_v)] * pp))
    out = jnp.einsum('bhdgq,hg->bqhd', acc.reshape(Bs, n_heads, HEAD_DIM, n_heads, n_q), eye)
    return out.reshape(Bs, n_q, HD)


def _cumsum_rows(x, tril_bf16):
    C = x.shape[0]
    if C < 16:
        rows = [x[0:1]]
        for i in range(1, C):
            rows.append(rows[-1] + x[i:i + 1])
        return jnp.concatenate(rows, axis=0)
    h1 = x.astype(BF16)
    r1 = x - h1.astype(F32)
    h2 = r1.astype(BF16)
    h3 = (r1 - h2.astype(F32)).astype(BF16)
    dot = lambda h: jnp.dot(tril_bf16, h, preferred_element_type=F32)
    return dot(h1) + dot(h2) + dot(h3)


def _hgrn_kernel(z_ref, lb_ref, ng_ref, s0_ref, o_ref, sT_ref, st_ref, *, C, SB, n_chunks):
    ci = pl.program_id(1)
    W = lb_ref.shape[1]
    n_heads = W // HEAD_DIM
    n_sub = C // SB

    @pl.when(ci == 0)
    def _():
        st_ref[...] = s0_ref[...]

    ones_blk = _head_block_ones(W)
    blockmask = (lax.broadcasted_iota(jnp.int32, (W, W), 0) // HEAD_DIM
                 == lax.broadcasted_iota(jnp.int32, (W, W), 1) // HEAD_DIM)
    lane_head = lax.broadcasted_iota(jnp.int32, (SB, W), 1) // HEAD_DIM
    t_sub = lax.broadcasted_iota(jnp.int32, (SB, W), 0)
    tril = jnp.where(lax.broadcasted_iota(jnp.int32, (C, C), 0)
                     >= lax.broadcasted_iota(jnp.int32, (C, C), 1), 1.0, 0.0).astype(BF16)
    s_col = lax.broadcasted_iota(jnp.int32, (n_heads * SB, C), 1)
    lb = lb_ref[...]
    ng = ng_ref[...]

    def chunk(cc, carry):
        for bb in range(z_ref.shape[0]):
            chunk_of(bb, cc)
        return carry

    def chunk_of(bb, cc):
        r0 = pl.multiple_of(cc * C, C)
        zc = z_ref[bb, pl.ds(r0, C), :]
        q, fz, v, gate = zc[:, :W], zc[:, W:2 * W], zc[:, 2 * W:3 * W], zc[:, 3 * W:]
        f = lb + (1.0 - lb) * _sigmoid(fz)
        k = 1.0 - f
        b = _cumsum_rows(jnp.log(f), tril)
        st = st_ref[bb]
        o_inter = _bdot_t(q * jnp.exp(b), st)
        outs = []
        for I in range(n_sub):
            lo_, hi_ = I * SB, (I + 1) * SB
            bI, qI, kI, vI = b[lo_:hi_], q[lo_:hi_], k[lo_:hi_], v[lo_:hi_]
            ds = []
            for s in range(SB):
                e = jnp.exp(jnp.minimum(bI - bI[s:s + 1], 0.0))
                ds.append(jnp.where(t_sub >= s, e * qI * kI[s:s + 1], 0.0))
            G = jnp.dot(jnp.concatenate(ds, axis=0).astype(BF16), ones_blk,
                        preferred_element_type=F32)
            od = G[0:SB] * vI[0:1]
            for s in range(1, SB):
                od = od + G[s * SB:(s + 1) * SB] * vI[s:s + 1]
            if I > 0:
                rho = b[lo_ - 1:lo_]
                qs = qI * jnp.exp(bI - rho)
                kt = k * jnp.exp(jnp.minimum(rho - b, 0.0))
                qst = jnp.concatenate([jnp.where(lane_head == h, qs, 0.0) for h in range(n_heads)],
                                      axis=0)
                att = jnp.where(s_col < lo_, _bdot_t(qst, kt), 0.0)
                R = _bdot(att, v)
                for h in range(n_heads):
                    od = od + jnp.where(lane_head == h, R[h * SB:(h + 1) * SB], 0.0)
            outs.append(od)
        o = o_inter + (outs[0] if n_sub == 1 else jnp.concatenate(outs, axis=0))
        blast = b[C - 1:C]
        kd = k * jnp.exp(blast - b)
        upd = lax.dot_general(v.astype(BF16), kd.astype(BF16), (((0,), (0,)), ((), ())),
                              preferred_element_type=F32)
        st_ref[bb] = st * jnp.exp(blast) + jnp.where(blockmask, upd, 0.0)
        ms = _dot2(o * o, ones_blk) * (1.0 / HEAD_DIM)
        o_ref[bb, pl.ds(r0, C), :] = o * lax.rsqrt(ms + RMS_EPS) * ng * _silu(gate)

    lax.fori_loop(0, n_chunks, chunk, 0)

    @pl.when(ci == pl.num_programs(1) - 1)
    def _():
        sT_ref[...] = st_ref[...]


def hgrn2(zb, lb, norm_g, s0, *, rows_per_step=256):
    B, L, W4 = zb.shape
    W = W4 // 4
    H = W // HEAD_DIM
    C = math.gcd(L, 64)
    SB = min(16, C)
    rows = min(rows_per_step, L)
    eye = jnp.eye(H, dtype=F32)
    st0 = jnp.einsum('bhdv,hg->bhvgd', s0, eye).reshape(B, W, W)
    nb = 2 if B % 2 == 0 else 1
    o, sT = pl.pallas_call(
        functools.partial(_hgrn_kernel, C=C, SB=SB, n_chunks=rows // C),
        grid=(B // nb, L // rows),
        in_specs=[pl.BlockSpec((nb, rows, W4), lambda b, i: (b, i, 0)),
                  _const_spec((1, W)), _const_spec((1, W)),
                  pl.BlockSpec((nb, W, W), lambda b, i: (b, 0, 0))],
        out_specs=[pl.BlockSpec((nb, rows, W), lambda b, i: (b, i, 0)),
                   pl.BlockSpec((nb, W, W), lambda b, i: (b, 0, 0))],
        out_shape=[jax.ShapeDtypeStruct((B, L, W), F32), jax.ShapeDtypeStruct((B, W, W), F32)],
        scratch_shapes=[pltpu.VMEM((nb, W, W), F32)],
        compiler_params=_cparams(("parallel", "arbitrary")),
    )(zb, lb.reshape(1, W), norm_g.reshape(1, W), st0)
    s5 = sT.reshape(B, H, HEAD_DIM, H, HEAD_DIM)
    s_new = jnp.einsum('bhvgd,hg->bhdv', s5, eye)
    return o, s_new


def _rwkv_pre_kernel(z_ref, sh0_ref, mu_ref, w0_ref, wup_ref, a0_ref, aup_ref, gup_ref,
                     kkw_ref, kaw_ref, rk_ref, *refs, scan_layout, steps):
    if scan_layout:
        src_o, v_o, bonus_o, g_o, shift_o, prev_ref = refs
    else:
        *ops_o, v_o, bonus_o, g_o, shift_o, prev_ref = refs
    ti = pl.program_id(1)
    z = z_ref[0]
    tm, P = z.shape
    W = w0_ref.shape[1]

    @pl.when(ti == 0)
    def _():
        prev_ref[...] = sh0_ref[0]

    row = lax.broadcasted_iota(jnp.int32, (tm, P), 0)
    prev = jnp.where(row == 0, prev_ref[...], pltpu.roll(z, 1, axis=0))
    last = z[tm - 1:tm]
    prev_ref[...] = last
    shift_o[0] = last
    zs = z + (prev - z) * mu_ref[...]
    r, k, v, x4 = zs[:, :W], zs[:, W:2 * W], zs[:, 2 * W:3 * W], zs[:, 3 * W:]
    ones_blk = _head_block_ones(W)
    u = w0_ref[...] + _bdot(jnp.tanh(x4), wup_ref[...])
    log_w = -jnp.exp(-_softplus(-u) - 0.5)
    a = _sigmoid(a0_ref[...] + _bdot(x4, aup_ref[...]))
    kkr = k * kkw_ref[...]
    kk = kkr / jnp.maximum(jnp.sqrt(_dot2(kkr * kkr, ones_blk)), 1e-12)
    k2 = k * (1.0 + (a - 1.0) * kaw_ref[...])
    blk_r = lax.broadcasted_iota(jnp.int32, (tm, tm), 0)
    blk_c = lax.broadcasted_iota(jnp.int32, (tm, tm), 1)
    tril = jnp.where((blk_r >= blk_c) & (blk_r // steps == blk_c // steps), 1.0, 0.0).astype(BF16)
    log_g = _cumsum_rows(log_w, tril)
    gam = jnp.exp(log_g)
    inv_gam = jnp.exp(-log_g)
    scan_ops = (kk * jnp.exp(log_g - log_w), kk * a * inv_gam, k2 * inv_gam, r * gam, gam)
    if scan_layout:
        low = lax.broadcasted_iota(jnp.int32, (HEAD_DIM, LANES), 1) < HEAD_DIM
        for qi, xq in enumerate(scan_ops):
            for hp in range(W // LANES):
                for c2 in range(tm // LANES):
                    t_ = xq[c2 * LANES:(c2 + 1) * LANES, hp * LANES:(hp + 1) * LANES].T
                    top, bot = t_[:HEAD_DIM], t_[HEAD_DIM:]
                    src_o[hp, 2 * c2, qi] = jnp.where(low, top, pltpu.roll(bot, HEAD_DIM, axis=1))
                    src_o[hp, 2 * c2 + 1, qi] = jnp.where(low, pltpu.roll(top, HEAD_DIM, axis=1), bot)
    else:
        for o_ref, xq in zip(ops_o, scan_ops):
            o_ref[0] = xq
    for hp in range(W // LANES):
        v_o[0, hp] = v[:, hp * LANES:(hp + 1) * LANES]
    bonus_o[0] = _dot2(r * k2 * rk_ref[...], ones_blk) * v
    g_o[0] = _bdot(_sigmoid(x4), gup_ref[...])


def _rwkv_scan_kernel(src_ref, v_ref, s0_ref, o_ref, sT_ref, st_ref, lhs_ref, *, P, steps):
    c = pl.program_id(1)

    @pl.when(c == 0)
    def _():
        st_ref[...] = s0_ref[...]

    n_op = src_ref.shape[2]
    for p in range(P):
        x = src_ref[p, 0].reshape(n_op * HEAD_DIM, LANES)
        hi, lo = _split2(x)
        lhs_ref[p] = jnp.concatenate([hi, lo], axis=1)
    n_step_rows = (n_op - 1) * HEAD_DIM

    rr = lax.broadcasted_iota(jnp.int32, (2 * LANES, 2 * LANES), 0)
    cc = lax.broadcasted_iota(jnp.int32, (2 * LANES, 2 * LANES), 1)
    t_of_row = jnp.where((rr // HEAD_DIM) % 2 == (cc // HEAD_DIM) % 2,
                         rr % HEAD_DIM - cc // LANES, -1)

    def step2(i, carry):
        t0 = 2 * i
        sel = jnp.where(t_of_row == t0, 1.0, 0.0).astype(BF16)
        for p in range(P):
            cb = jnp.dot(lhs_ref[p, :n_step_rows], sel, preferred_element_type=F32)
            s = st_ref[p]
            for u in range(2):
                kk, ka, k2, r = (cb[i_ * HEAD_DIM:(i_ + 1) * HEAD_DIM, u * LANES:(u + 1) * LANES]
                                 for i_ in range(4))
                skk = jnp.sum(s * kk, axis=0, keepdims=True)
                s = s - ka * skk + k2 * v_ref[p, pl.ds(t0 + u, 1), :]
                o_ref[p, pl.ds(t0 + u, 1), :] = jnp.sum(s * r, axis=0, keepdims=True)
            st_ref[p] = s
        return carry

    lax.fori_loop(0, steps // 2, step2, 0)

    sel_last = jnp.where(t_of_row == steps - 1, 1.0, 0.0).astype(BF16)
    for p in range(P):
        gam = jnp.dot(lhs_ref[p, n_step_rows:], sel_last, preferred_element_type=F32)
        st_ref[p] = st_ref[p] * gam[:, :LANES]

    @pl.when(c == pl.num_programs(1) - 1)
    def _():
        sT_ref[...] = st_ref[...]


def _rwkv_post_kernel(o_ref, bonus_ref, g_ref, gng_ref, gnb_ref, out_ref):
    o = jnp.concatenate([o_ref[0, hp] for hp in range(o_ref.shape[1])], axis=1)
    ones_blk = _head_block_ones(o.shape[1])
    inv = 1.0 / HEAD_DIM
    d = o - _dot2(o, ones_blk) * inv
    var = _dot2(d * d, ones_blk) * inv
    out_ref[0] = (d * lax.rsqrt(var + GN_EPS) * gng_ref[...] + gnb_ref[...] + bonus_ref[0]) * g_ref[0]


def rwkv7(za, shift0, s0, p, *, tm=512, pairs_per_step=16):
    B, L, P = za.shape
    W = p['rwkv_w0'].shape[0]
    H = W // HEAD_DIM
    HP = W // LANES
    tm = min(tm, L)
    n_low = P - 3 * W
    rank_w, rank_a = p['rwkv_w_up'].shape[0], p['rwkv_a_up'].shape[0]
    pad_rows = lambda m, r0: jnp.zeros((n_low, W), F32).at[r0:r0 + m.shape[0]].set(m).astype(BF16)
    wup = pad_rows(p['rwkv_w_up'], 0)
    aup = pad_rows(p['rwkv_a_up'], rank_w)
    gup = pad_rows(p['rwkv_g_up'], rank_w + rank_a)
    vec = lambda a: a.reshape(1, -1)
    tok = pl.BlockSpec((1, tm, W), lambda b, i: (b, i, 0))
    tok_sd = jax.ShapeDtypeStruct((B, L, W), F32)
    pair_spec = pl.BlockSpec((1, HP, tm, LANES), lambda b, i: (b, 0, i, 0))
    pair_sd = jax.ShapeDtypeStruct((B, HP, L, LANES), F32)
    steps = min(HEAD_DIM, L)
    assert steps % 2 == 0
    NC = L // steps
    NP = B * HP
    scan_layout = tm % LANES == 0
    if scan_layout:
        ops_specs = [pl.BlockSpec((HP, tm // HEAD_DIM, 5, HEAD_DIM, LANES),
                                  lambda b, i: (b, i, 0, 0, 0))]
        ops_sds = [jax.ShapeDtypeStruct((NP, NC, 5, HEAD_DIM, LANES), F32)]
    else:
        ops_specs, ops_sds = [tok] * 5, [tok_sd] * 5
    *ops, v, bonus, g, shift = pl.pallas_call(
        functools.partial(_rwkv_pre_kernel, scan_layout=scan_layout, steps=steps),
        grid=(B, L // tm),
        in_specs=[pl.BlockSpec((1, tm, P), lambda b, i: (b, i, 0)),
                  pl.BlockSpec((1, 1, P), lambda b, i: (b, 0, 0)),
                  _const_spec((1, P)), _const_spec((1, W)), _const_spec((n_low, W)),
                  _const_spec((1, W)), _const_spec((n_low, W)), _const_spec((n_low, W)),
                  _const_spec((1, W)), _const_spec((1, W)), _const_spec((1, W))],
        out_specs=ops_specs + [pair_spec, tok, tok, pl.BlockSpec((1, 1, P), lambda b, i: (b, 0, 0))],
        out_shape=ops_sds + [pair_sd, tok_sd, tok_sd, jax.ShapeDtypeStruct((B, 1, P), F32)],
        scratch_shapes=[pltpu.VMEM((1, P), F32)],
        compiler_params=_cparams(("parallel", "arbitrary")),
    )(za, shift0.reshape(B, 1, P), vec(p['rwkv_mu']), vec(p['rwkv_w0']), wup, vec(p['rwkv_a0']),
      aup, gup, vec(p['rwkv_k_k']), vec(p['rwkv_k_a']), vec(p['rwkv_r_k']))

    if scan_layout:
        src, = ops
    else:
        src = jnp.stack(ops)
        src = src.reshape(5, B, NC, steps, HP, 2, HEAD_DIM).transpose(1, 4, 2, 0, 6, 5, 3)
        src = jnp.pad(src, ((0, 0),) * 6 + ((0, HEAD_DIM - steps),))
        src = src.reshape(NP, NC, 5, HEAD_DIM, LANES)
    st0 = s0.reshape(B, HP, 2, HEAD_DIM, HEAD_DIM).transpose(0, 1, 4, 2, 3).reshape(NP, HEAD_DIM, LANES)
    PP = min(pairs_per_step, NP)
    o, sT = pl.pallas_call(
        functools.partial(_rwkv_scan_kernel, P=PP, steps=steps),
        grid=(NP // PP, NC),
        in_specs=[pl.BlockSpec((PP, 1, 5, HEAD_DIM, LANES), lambda g_, c: (g_, c, 0, 0, 0)),
                  pl.BlockSpec((PP, steps, LANES), lambda g_, c: (g_, c, 0)),
                  pl.BlockSpec((PP, HEAD_DIM, LANES), lambda g_, c: (g_, 0, 0))],
        out_specs=[pl.BlockSpec((PP, steps, LANES), lambda g_, c: (g_, c, 0)),
                   pl.BlockSpec((PP, HEAD_DIM, LANES), lambda g_, c: (g_, 0, 0))],
        out_shape=[jax.ShapeDtypeStruct((NP, L, LANES), F32),
                   jax.ShapeDtypeStruct((NP, HEAD_DIM, LANES), F32)],
        scratch_shapes=[pltpu.VMEM((PP, HEAD_DIM, LANES), F32),
                        pltpu.VMEM((PP, 5 * HEAD_DIM, 2 * LANES), BF16)],
        compiler_params=_cparams(("parallel", "arbitrary")),
    )(src, v.reshape(NP, L, LANES), st0)
    s_new = sT.reshape(B, HP, HEAD_DIM, 2, HEAD_DIM).transpose(0, 1, 3, 4, 2).reshape(B, H, HEAD_DIM, HEAD_DIM)

    out = pl.pallas_call(
        _rwkv_post_kernel,
        grid=(B, L // tm),
        in_specs=[pair_spec, tok, tok, _const_spec((1, W)), _const_spec((1, W))],
        out_specs=tok,
        out_shape=tok_sd,
        compiler_params=_cparams(("parallel", "parallel")),
    )(o.reshape(B, HP, L, LANES), bonus, g, vec(p['rwkv_gn_g']), vec(p['rwkv_gn_b']))
    return out, s_new, shift.reshape(B, P)


def _route_kernel(x_ref, rt_ref, g_ref, s_ref):
    logits = lax.dot_general(rt_ref[...], x_ref[...], (((1,), (1,)), ((), ())),
                             precision=lax.Precision.HIGHEST, preferred_element_type=F32)
    n_e = logits.shape[0]
    e_id = lax.broadcasted_iota(jnp.int32, logits.shape, 0)
    m1 = jnp.max(logits, axis=0, keepdims=True)
    i1 = jnp.min(jnp.where(logits == m1, e_id, n_e), axis=0, keepdims=True)
    rest = jnp.where(e_id == i1, -jnp.inf, logits)
    m2 = jnp.max(rest, axis=0, keepdims=True)
    i2 = jnp.min(jnp.where(rest == m2, e_id, n_e), axis=0, keepdims=True)
    t = jnp.exp(m2 - m1)
    g1 = 1.0 / (1.0 + t)
    g2 = t / (1.0 + t)
    g_ref[...] = jnp.where(e_id == i1, g1, jnp.where(e_id == i2, g2, 0.0))
    s_ref[...] = jnp.where(e_id == i1, 1.0, jnp.where(e_id == i2, 1.0, 0.0))


def _moe_kernel(xb_ref, g_ref, s_ref, wg_ref, wu_ref, wd_ref, o_ref, rank_ref, xg_ref, y_ref, *, R):
    e = pl.program_id(1)
    f = pl.program_id(2)
    n_f = pl.num_programs(2)
    tm = xb_ref.shape[0]

    @pl.when((e == 0) & (f == 0))
    def _():
        before = jnp.where(lax.broadcasted_iota(jnp.int32, (tm, tm), 0)
                           < lax.broadcasted_iota(jnp.int32, (tm, tm), 1), 1.0, 0.0).astype(BF16)
        rank_ref[...] = jnp.dot(s_ref[...].astype(BF16), before, preferred_element_type=F32)
        o_ref[...] = jnp.zeros(o_ref.shape, F32)

    sel = s_ref[pl.ds(e, 1), :]
    key = jnp.where(sel > 0.0, rank_ref[pl.ds(e, 1), :], -1.0)
    n_chunks = (jnp.sum(sel).astype(jnp.int32) + (R - 1)) // R
    r_id = lax.broadcasted_iota(jnp.int32, (R, tm), 0)

    def one_hot(c):
        return jnp.where(key == (r_id + c * R).astype(F32), 1.0, 0.0).astype(BF16)

    def rows_of(c):
        return pl.ds(pl.multiple_of(c * R, math.gcd(R, 256)), R)

    @pl.when(f == 0)
    def _():
        def gather(c, carry):
            xg_ref[rows_of(c), :] = jnp.dot(one_hot(c), xb_ref[...],
                                            preferred_element_type=F32).astype(BF16)
            return carry
        lax.fori_loop(0, n_chunks, gather, 0)

    def expert(c, carry):
        xg = xg_ref[rows_of(c), :]
        gate = jnp.dot(xg, wg_ref[...], preferred_element_type=F32)
        up = jnp.dot(xg, wu_ref[...], preferred_element_type=F32)
        y = _bdot(_silu(gate) * up, wd_ref[...])

        @pl.when(f == 0)
        def _():
            y_ref[rows_of(c), :] = y

        @pl.when(f > 0)
        def _():
            y_ref[rows_of(c), :] += y
        return carry

    lax.fori_loop(0, n_chunks, expert, 0)

    @pl.when(f == n_f - 1)
    def _():
        gt = g_ref[pl.ds(e, 1), :]
        h1 = gt.astype(BF16)
        r1 = gt - h1.astype(F32)
        h2 = r1.astype(BF16)
        h3 = (r1 - h2.astype(F32)).astype(BF16)
        g3 = jnp.concatenate([h1, h2, h3, jnp.zeros((5, tm), BF16)], axis=0)

        def scatter(c, carry):
            p = one_hot(c)
            g_row = jnp.sum(lax.dot_general(p, g3, (((1,), (1,)), ((), ())),
                                            preferred_element_type=F32), axis=1, keepdims=True)
            yw = (y_ref[rows_of(c), :] * g_row).astype(BF16)
            o_ref[...] += lax.dot_general(p, yw, (((0,), (0,)), ((), ())),
                                          preferred_element_type=F32)
            return carry
        lax.fori_loop(0, n_chunks, scatter, 0)


def moe_ffn(x, router, wg_bf16, wu_bf16, wd_bf16, *, tm=1024, tf=896, R=288):
    T, D = x.shape
    n_e, _, F = wg_bf16.shape
    tm = min(tm, T)
    R = min(R, tm)
    gates, sel = pl.pallas_call(
        _route_kernel,
        grid=(T // tm,),
        in_specs=[pl.BlockSpec((tm, D), lambda i: (i, 0)), _const_spec((n_e, D))],
        out_specs=[pl.BlockSpec((n_e, tm), lambda i: (0, i))] * 2,
        out_shape=[jax.ShapeDtypeStruct((n_e, T), F32)] * 2,
        compiler_params=_cparams(("parallel",)),
    )(x, router.T)
    return pl.pallas_call(
        functools.partial(_moe_kernel, R=R),
        grid=(T // tm, n_e, F // tf),
        in_specs=[pl.BlockSpec((tm, D), lambda i, e, f: (i, 0)),
                  pl.BlockSpec((n_e, tm), lambda i, e, f: (0, i)),
                  pl.BlockSpec((n_e, tm), lambda i, e, f: (0, i)),
                  pl.BlockSpec((None, D, tf), lambda i, e, f: (e, 0, f)),
                  pl.BlockSpec((None, D, tf), lambda i, e, f: (e, 0, f)),
                  pl.BlockSpec((None, tf, D), lambda i, e, f: (e, f, 0))],
        out_specs=pl.BlockSpec((tm, D), lambda i, e, f: (i, 0)),
        out_shape=jax.ShapeDtypeStruct((T, D), F32),
        scratch_shapes=[pltpu.VMEM((n_e, tm), F32), pltpu.VMEM((pl.cdiv(tm, R) * R, D), BF16),
                        pltpu.VMEM((pl.cdiv(tm, R) * R, D), F32)],
        compiler_params=_cparams(("parallel", "arbitrary", "arbitrary")),
    )(x.astype(BF16), gates, sel, wg_bf16, wu_bf16, wd_bf16)


def _layer(x, seq_len, p, lb, states, sb_past, mem_k, mem_v, ffn, alpha, layer, depth, kv_stack):
    T, D = x.shape
    B = T // seq_len
    W = p['rwkv_w0'].shape[0]
    pa = p['rwkv_mu'].shape[0]
    wc = p['sb_bias'].shape[0] * HEAD_DIM
    x, za, zb, q, k, v, k4, v4 = in_proj(x, p['ln_in_g'], p['ln_in_b'], p['w_in'], kv_stack, layer,
                                         depth, apply_ln=layer == 0, widths=(pa, 4 * W, wc, wc, wc))
    to3 = lambda a: a.reshape(B, seq_len, a.shape[-1])
    o_a, rwkv_s, shift = rwkv7(to3(za), states[1], states[0], p)
    o_b, hgrn_s = hgrn2(to3(zb), lb, p['hgrn_norm_g'], states[2])
    n_heads = p['sb_bias'].shape[0]
    if sb_past is None:
        o_c = sb_prompt(to3(q), to3(k), to3(v), p['sb_bias'], n_heads=n_heads)
    else:
        o_c = sb_sample(to3(q), to3(k), to3(v), p['sb_bias'], *sb_past, n_heads=n_heads)
    x = mix_out(o_a.reshape(T, W), o_b.reshape(T, W), o_c.reshape(T, wc), p['w_out'], x,
                p['ln_mix_g'], p['ln_mix_b'], alpha=alpha)
    mem_layer = layer if mem_k.shape[0] == depth else 0
    x = mem_block(x, p['mem_wq'], mem_k, mem_v, p['mem_wo'], p['ln_mem_g'], p['ln_mem_b'],
                  alpha=alpha, seq_len=seq_len, layer=mem_layer)
    x = ffn(x)
    return x, (rwkv_s, shift, hgrn_s, (k4, v4))


def kernel(x_prompt, x_sample, cache_sb_k, cache_sb_v, state_rwkv, state_rwkv_shift, state_hgrn,
           cache_mem_k, cache_mem_v, page_table, mem_prompt, ln_in_g, ln_in_b, w_in, rwkv_mu, rwkv_w0,
           rwkv_w_up, rwkv_a0, rwkv_a_up, rwkv_g_up, rwkv_k_k, rwkv_k_a, rwkv_r_k, rwkv_gn_g, rwkv_gn_b,
           hgrn_lb, hgrn_norm_g, sb_bias, w_out, ln_mix_g, ln_mix_b, mem_wq, mem_wk, mem_wv, mem_wo,
           ln_mem_g, ln_mem_b, ffn_w_gate, ffn_w_up, ffn_w_down, moe_router, moe_w_gate, moe_w_up,
           moe_w_down, ln_ffn_g, ln_ffn_b):
    B, L, D = x_prompt.shape
    Bs, Ls, _ = x_sample.shape
    depth = w_in.shape[0]
    H = state_rwkv.shape[2]
    n_heads_c = sb_bias.shape[1]
    n_mem = mem_prompt.shape[1]
    alpha = (2 * depth) ** 0.25
    bf = lambda a: a.astype(BF16)

    lb_sm = jax.nn.softmax(hgrn_lb.astype(F32), axis=0)
    lb_all = jnp.cumsum(lb_sm, axis=0) - lb_sm[0]

    mem_flat = mem_prompt.reshape(B * n_mem, D)
    zeros_p = (jnp.zeros((B, H, HEAD_DIM, HEAD_DIM), F32), jnp.zeros((B, rwkv_mu.shape[1]), F32),
               jnp.zeros((B, H, HEAD_DIM, HEAD_DIM), F32))

    xp = x_prompt.reshape(B * L, D)
    xs = x_sample.reshape(Bs * Ls, D)
    outs_p = [[] for _ in range(5)]
    outs_s = [[] for _ in range(3)]
    stacks = lambda n_tok: tuple(jnp.zeros((depth * n_tok * n_heads_c, HEAD_DIM), F32)
                                 for _ in range(2))
    kv_p, kv_s = stacks(B * L), stacks(Bs * Ls)
    for l in range(depth):
        p = dict(ln_in_g=ln_in_g, ln_in_b=ln_in_b, w_in=bf(w_in[l]), w_out=bf(w_out[l]),
                 rwkv_mu=rwkv_mu[l], rwkv_w0=rwkv_w0[l], rwkv_w_up=rwkv_w_up[l], rwkv_a0=rwkv_a0[l],
                 rwkv_a_up=rwkv_a_up[l], rwkv_g_up=rwkv_g_up[l], rwkv_k_k=rwkv_k_k[l],
                 rwkv_k_a=rwkv_k_a[l], rwkv_r_k=rwkv_r_k[l], rwkv_gn_g=rwkv_gn_g[l],
                 rwkv_gn_b=rwkv_gn_b[l], hgrn_norm_g=hgrn_norm_g[l], sb_bias=sb_bias[l],
                 ln_mix_g=ln_mix_g[l], ln_mix_b=ln_mix_b[l], mem_wq=bf(mem_wq[l]),
                 mem_wo=bf(mem_wo[l]), ln_mem_g=ln_mem_g[l], ln_mem_b=ln_mem_b[l])
        j = l // 2
        if l % 2 == 0:
            wg, wu, wd = bf(ffn_w_gate[j]), bf(ffn_w_up[j]), bf(ffn_w_down[j])
            ffn = lambda x, wg=wg, wu=wu, wd=wd, l=l: ffn_block(
                x, wg, wu, wd, ln_ffn_g[l], ln_ffn_b[l], alpha=alpha)
        else:
            wg, wu, wd = bf(moe_w_gate[j]), bf(moe_w_up[j]), bf(moe_w_down[j])
            ffn = lambda x, wg=wg, wu=wu, wd=wd, j=j, l=l: res_ln(
                x, moe_ffn(x, moe_router[j], wg, wu, wd), ln_ffn_g[l], ln_ffn_b[l], alpha=alpha)
        mk, mv = matmul2(mem_flat, bf(mem_wk[l]), bf(mem_wv[l]))
        mk = mk.reshape(1, B, n_mem, D)
        mv = mv.reshape(1, B, n_mem, D)
        xp, (s_a, sh, s_b, kv_p) = _layer(xp, L, p, lb_all[l], zeros_p, None, mk, mv, ffn,
                                          alpha, l, depth, kv_p)
        for lst, val in zip(outs_p, (s_a, sh, s_b,
                                     mk.reshape(B, n_mem, N_MEM_HEADS, D // N_MEM_HEADS),
                                     mv.reshape(B, n_mem, N_MEM_HEADS, D // N_MEM_HEADS))):
            lst.append(val)
        states = (state_rwkv[l], state_rwkv_shift[l], state_hgrn[l])
        xs, (s_a, sh, s_b, kv_s) = _layer(
            xs, Ls, p, lb_all[l], states, (cache_sb_k, cache_sb_v, page_table, l),
            cache_mem_k.reshape(depth, Bs, n_mem, D), cache_mem_v.reshape(depth, Bs, n_mem, D), ffn,
            alpha, l, depth, kv_s)
        for lst, val in zip(outs_s, (s_a, sh, s_b)):
            lst.append(val)
    kv5 = lambda a, nb, sl: a.reshape(depth, nb, sl, n_heads_c, HEAD_DIM)
    return (xp.reshape(B, L, D), xs.reshape(Bs, Ls, D),
            kv5(kv_p[0], B, L), kv5(kv_p[1], B, L), *(jnp.stack(o) for o in outs_p),
            kv5(kv_s[0], Bs, Ls), kv5(kv_s[1], Bs, Ls), *(jnp.stack(o) for o in outs_s))
```

```python
import functools
import math

import jax
import jax.numpy as jnp
from jax import lax
from jax.experimental import pallas as pl
from jax.experimental.pallas import tpu as pltpu

F32 = jnp.float32
BF16 = jnp.bfloat16

HEAD_DIM = 64
LANES = 128
PAGE_SIZE = 128
N_MEM_HEADS = 4
TOP_K = 2
LN_EPS = 1e-5
GN_EPS = 64e-5
RMS_EPS = 1e-6
VMEM_LIMIT = 56 * 1024 * 1024


def _cparams(sem):
    return pltpu.CompilerParams(dimension_semantics=sem, vmem_limit_bytes=VMEM_LIMIT)


def _const_spec(shape):
    nd = len(shape)
    return pl.BlockSpec(shape, lambda *_: (0,) * nd, pipeline_mode=pl.Buffered(1))


def _bdot(a, b):
    return jnp.dot(a.astype(BF16), b.astype(BF16), preferred_element_type=F32)


def _bdot_t(a, b):
    return lax.dot_general(a.astype(BF16), b.astype(BF16), (((1,), (1,)), ((), ())),
                           preferred_element_type=F32)


def _split2(x):
    hi = x.astype(BF16)
    lo = (x - hi.astype(F32)).astype(BF16)
    return hi, lo


def _dot2(x, w_bf16):
    hi, lo = _split2(x)
    return (jnp.dot(hi, w_bf16, preferred_element_type=F32)
            + jnp.dot(lo, w_bf16, preferred_element_type=F32))


def _layer_norm(x, g, b):
    mu = jnp.mean(x, -1, keepdims=True)
    xc = x - mu
    var = jnp.mean(xc * xc, -1, keepdims=True)
    return xc * lax.rsqrt(var + LN_EPS) * g + b


def _sigmoid(x):
    return 1.0 / (1.0 + jnp.exp(-x))


def _silu(x):
    return x * _sigmoid(x)


def _softplus(x):
    return jnp.maximum(x, 0.0) + jnp.log(1.0 + jnp.exp(-jnp.abs(x)))


def _head_block_ones(width):
    r = lax.broadcasted_iota(jnp.int32, (width, width), 0) // HEAD_DIM
    c = lax.broadcasted_iota(jnp.int32, (width, width), 1) // HEAD_DIM
    return jnp.where(r == c, 1.0, 0.0).astype(BF16)


def _in_proj_kernel(x_ref, g_ref, b_ref, w_ref, k4_in, v4_in,
                    xn_ref, za_ref, zb_ref, q_ref, k_ref, v_ref, k4_ref, v4_ref, *, apply_ln, n_heads):
    del k4_in, v4_in
    x = x_ref[...]
    if apply_ln:
        x = _layer_norm(x, g_ref[...], b_ref[...])
    xn_ref[...] = x
    z = _bdot(x, w_ref[...])
    c0 = 0
    for z_ref in (za_ref, zb_ref, q_ref, k_ref, v_ref):
        z_ref[...] = z[:, c0:c0 + z_ref.shape[1]]
        c0 += z_ref.shape[1]
    tm = x.shape[0]
    hd = k_ref.shape[1]
    for src0, dst in ((c0 - 2 * hd, k4_ref), (c0 - hd, v4_ref)):
        for h in range(n_heads):
            dst[pl.ds(h, tm, stride=n_heads), :] = z[:, src0 + h * HEAD_DIM:src0 + (h + 1) * HEAD_DIM]


def in_proj(x, g, b, w_bf16, kv_stack, layer, depth, *, apply_ln, widths, tm=512):
    T, D = x.shape
    N = w_bf16.shape[1]
    assert sum(widths) == N and all(wd % LANES == 0 for wd in widths)
    n_heads = widths[-1] // HEAD_DIM
    tm = min(tm, T)
    nt = T // tm
    row = lambda width: pl.BlockSpec((tm, width), lambda i: (i, 0))
    stack_spec = pl.BlockSpec((tm * n_heads, HEAD_DIM), lambda i: (layer * nt + i, 0))
    stack_sd = jax.ShapeDtypeStruct((depth * T * n_heads, HEAD_DIM), F32)
    return pl.pallas_call(
        functools.partial(_in_proj_kernel, apply_ln=apply_ln, n_heads=n_heads),
        grid=(nt,),
        in_specs=[row(D), _const_spec((1, D)), _const_spec((1, D)), _const_spec((D, N))]
                 + [pl.BlockSpec(memory_space=pl.ANY)] * 2,
        out_specs=[row(D)] + [row(wd) for wd in widths] + [stack_spec, stack_spec],
        out_shape=[jax.ShapeDtypeStruct((T, D), F32)]
                  + [jax.ShapeDtypeStruct((T, wd), F32) for wd in widths] + [stack_sd, stack_sd],
        input_output_aliases={4: 6, 5: 7},
        compiler_params=_cparams(("parallel",)),
    )(x, g.reshape(1, D), b.reshape(1, D), w_bf16, *kv_stack)


def _matmul2_kernel(x_ref, w1_ref, w2_ref, o1_ref, o2_ref):
    xb = x_ref[...].astype(BF16)
    o1_ref[...] = jnp.dot(xb, w1_ref[...], preferred_element_type=F32)
    o2_ref[...] = jnp.dot(xb, w2_ref[...], preferred_element_type=F32)


def matmul2(x, w1_bf16, w2_bf16, tm=512):
    T, K = x.shape
    N = w1_bf16.shape[1]
    tm = min(tm, T)
    out = pl.BlockSpec((tm, N), lambda i: (i, 0))
    return pl.pallas_call(
        _matmul2_kernel,
        grid=(T // tm,),
        in_specs=[pl.BlockSpec((tm, K), lambda i: (i, 0)), _const_spec((K, N)), _const_spec((K, N))],
        out_specs=[out, out],
        out_shape=[jax.ShapeDtypeStruct((T, N), F32)] * 2,
        compiler_params=_cparams(("parallel",)),
    )(x, w1_bf16, w2_bf16)


def _mix_out_kernel(oa_ref, ob_ref, oc_ref, wa_ref, wb_ref, wc_ref, res_ref, g_ref, b_ref, o_ref,
                    *, alpha):
    h = (_bdot(oa_ref[...], wa_ref[...]) + _bdot(ob_ref[...], wb_ref[...])
         + _bdot(oc_ref[...], wc_ref[...]))
    o_ref[...] = _layer_norm(alpha * res_ref[...] + h, g_ref[...], b_ref[...])


def mix_out(oa, ob, oc, w_out_bf16, res, g, b, *, alpha, tm=512):
    T, D = res.shape
    wa, wb, wc = oa.shape[1], ob.shape[1], oc.shape[1]
    tm = min(tm, T)
    row = lambda width: pl.BlockSpec((tm, width), lambda i: (i, 0))
    return pl.pallas_call(
        functools.partial(_mix_out_kernel, alpha=alpha),
        grid=(T // tm,),
        in_specs=[row(wa), row(wb), row(wc),
                  _const_spec((wa, D)), _const_spec((wb, D)), _const_spec((wc, D)),
                  row(D), _const_spec((1, D)), _const_spec((1, D))],
        out_specs=row(D),
        out_shape=jax.ShapeDtypeStruct((T, D), F32),
        compiler_params=_cparams(("parallel",)),
    )(oa, ob, oc, w_out_bf16[:wa], w_out_bf16[wa:wa + wb], w_out_bf16[wa + wb:], res,
      g.reshape(1, D), b.reshape(1, D))


def _res_ln_kernel(x_ref, y_ref, g_ref, b_ref, o_ref, *, alpha):
    o_ref[...] = _layer_norm(alpha * x_ref[...] + y_ref[...], g_ref[...], b_ref[...])


def res_ln(x, y, g, b, *, alpha, tm=1024):
    T, D = x.shape
    tm = min(tm, T)
    row = pl.BlockSpec((tm, D), lambda i: (i, 0))
    return pl.pallas_call(
        functools.partial(_res_ln_kernel, alpha=alpha),
        grid=(T // tm,),
        in_specs=[row, row, _const_spec((1, D)), _const_spec((1, D))],
        out_specs=row,
        out_shape=jax.ShapeDtypeStruct((T, D), F32),
        compiler_params=_cparams(("parallel",)),
    )(x, y, g.reshape(1, D), b.reshape(1, D))


def _mem_kernel(x_ref, wq_ref, mk_ref, mv_ref, wo_ref, g_ref, b_ref, o_ref, att_ref,
                *, alpha, nb, rows):
    x = x_ref[...]
    q = _bdot(x, wq_ref[...])
    D = q.shape[1]
    dh = D // N_MEM_HEADS
    scale = dh ** -0.5
    for bi in range(nb):
        r0 = bi * rows
        for h in range(N_MEM_HEADS):
            c0 = h * dh
            qh = q[r0:r0 + rows, c0:c0 + dh]
            kh = mk_ref[bi, :, c0:c0 + dh]
            vh = mv_ref[bi, :, c0:c0 + dh]
            s = _bdot_t(qh, kh) * scale
            s = s - jnp.max(s, -1, keepdims=True)
            p = jnp.exp(s)
            p = p / jnp.sum(p, -1, keepdims=True)
            att_ref[r0:r0 + rows, c0:c0 + dh] = _bdot(p, vh)
    y = _bdot(att_ref[...], wo_ref[...])
    o_ref[...] = _layer_norm(alpha * x + y, g_ref[...], b_ref[...])


def mem_block(x, wq_bf16, mk, mv, wo_bf16, g, b, *, alpha, seq_len, layer=0, tm=512):
    T, D = x.shape
    _, B, n_mem, _ = mk.shape
    if seq_len >= tm:
        nb, rows = 1, tm
        mem_map = lambda i: (layer, i // (seq_len // tm), 0, 0)
    else:
        nb, rows = min(B, 8), seq_len
        tm = nb * rows
        mem_map = lambda i: (layer, i, 0, 0)
    row = pl.BlockSpec((tm, D), lambda i: (i, 0))
    return pl.pallas_call(
        functools.partial(_mem_kernel, alpha=alpha, nb=nb, rows=rows),
        grid=(T // tm,),
        in_specs=[row, _const_spec((D, D)),
                  pl.BlockSpec((None, nb, n_mem, D), mem_map),
                  pl.BlockSpec((None, nb, n_mem, D), mem_map),
                  _const_spec((D, D)), _const_spec((1, D)), _const_spec((1, D))],
        out_specs=row,
        out_shape=jax.ShapeDtypeStruct((T, D), F32),
        scratch_shapes=[pltpu.VMEM((tm, D), F32)],
        compiler_params=_cparams(("parallel",)),
    )(x, wq_bf16, mk, mv, wo_bf16, g.reshape(1, D), b.reshape(1, D))


def _ffn_kernel(x_ref, wg_ref, wu_ref, wd_ref, g_ref, b_ref, o_ref, *, alpha, tf):
    x = x_ref[...]
    xb = x.astype(BF16)
    F = wg_ref.shape[1]
    y = jnp.zeros(x.shape, F32)
    for f0 in range(0, F, tf):
        gate = jnp.dot(xb, wg_ref[:, f0:f0 + tf], preferred_element_type=F32)
        up = jnp.dot(xb, wu_ref[:, f0:f0 + tf], preferred_element_type=F32)
        y = y + _bdot(_silu(gate) * up, wd_ref[f0:f0 + tf, :])
    o_ref[...] = _layer_norm(alpha * x + y, g_ref[...], b_ref[...])


def ffn_block(x, wg_bf16, wu_bf16, wd_bf16, g, b, *, alpha, tm=512, tf=256):
    T, D = x.shape
    F = wg_bf16.shape[1]
    tm = min(tm, T)
    row = pl.BlockSpec((tm, D), lambda i: (i, 0))
    return pl.pallas_call(
        functools.partial(_ffn_kernel, alpha=alpha, tf=tf),
        grid=(T // tm,),
        in_specs=[row, _const_spec((D, F)), _const_spec((D, F)), _const_spec((F, D)),
                  _const_spec((1, D)), _const_spec((1, D))],
        out_specs=row,
        out_shape=jax.ShapeDtypeStruct((T, D), F32),
        compiler_params=_cparams(("parallel",)),
    )(x, wg_bf16, wu_bf16, wd_bf16, g.reshape(1, D), b.reshape(1, D))


def _sb_prompt_kernel(bias_ref, q_ref, k_ref, v_ref, o_ref, acc_ref, c_ref, *, tq, tk, scale):
    hp = pl.program_id(1)
    qi = pl.program_id(2)
    n_sub = tq // tk
    q = q_ref[0] * scale
    half = lax.broadcasted_iota(jnp.int32, (tq, LANES), 1) // HEAD_DIM
    qm = [jnp.where(half == h2, q, 0.0).astype(BF16) for h2 in range(2)]
    bias = [bias_ref[2 * hp + h2] for h2 in range(2)]
    rr = lax.broadcasted_iota(jnp.int32, (tk, 2 * tk), 0)
    cc = lax.broadcasted_iota(jnp.int32, (tk, 2 * tk), 1)
    mw = jnp.where((cc >= tk) | (rr > cc), 1.0, 0.0).astype(BF16)
    causal = (lax.broadcasted_iota(jnp.int32, (tq, tq), 1)
              < lax.broadcasted_iota(jnp.int32, (tq, tq), 0))
    acc_ref[...] = jnp.zeros(acc_ref.shape, F32)
    c_ref[...] = jnp.zeros(c_ref.shape, F32)

    def block(j, masked):
        start = pl.multiple_of(j * tq, tq)
        kb = k_ref[0, pl.ds(start, tq), :].astype(BF16)
        vb = v_ref[0, pl.ds(start, tq), :].astype(BF16)
        for h2 in range(2):
            z = _bdot_t(qm[h2], kb) + bias[h2]
            sp = _softplus(z)
            spm = (jnp.where(causal, sp, 0.0) if masked else sp).astype(BF16)
            c = c_ref[h2]
            parts = [None] * n_sub
            for s in reversed(range(n_sub)):
                sl = slice(s * tk, (s + 1) * tk)
                r = jnp.dot(spm[:, sl], mw, preferred_element_type=F32)
                parts[s] = z[:, sl] - sp[:, sl] - r[:, :tk] - c
                c = c + r[:, tk:]
            c_ref[h2] = c
            a = jnp.exp(parts[0] if n_sub == 1 else jnp.concatenate(parts, axis=1))
            if masked:
                a = jnp.where(causal, a, 0.0)
            acc_ref[h2] += jnp.dot(a.astype(BF16), vb, preferred_element_type=F32)

    block(qi, True)

    def body(i, carry):
        block(qi - 1 - i, False)
        return carry

    lax.fori_loop(0, qi, body, 0)
    o_ref[0] = jnp.where(half == 0, acc_ref[0], acc_ref[1])


def sb_prompt(q, k, v, bias, *, n_heads, tq=512, tk=128):
    B, L, _ = q.shape
    tq = min(tq, L)
    tk = min(tk, tq)
    grid_spec = pltpu.PrefetchScalarGridSpec(
        num_scalar_prefetch=1,
        grid=(B, n_heads // 2, L // tq),
        in_specs=[pl.BlockSpec((1, tq, LANES), lambda b, hp, qi, bias: (b, qi, hp)),
                  pl.BlockSpec((1, L, LANES), lambda b, hp, qi, bias: (b, 0, hp)),
                  pl.BlockSpec((1, L, LANES), lambda b, hp, qi, bias: (b, 0, hp))],
        out_specs=pl.BlockSpec((1, tq, LANES), lambda b, hp, qi, bias: (b, qi, hp)),
        scratch_shapes=[pltpu.VMEM((2, tq, LANES), F32), pltpu.VMEM((2, tq, tk), F32)],
    )
    return pl.pallas_call(
        functools.partial(_sb_prompt_kernel, tq=tq, tk=tk, scale=HEAD_DIM ** -0.5),
        grid_spec=grid_spec,
        out_shape=jax.ShapeDtypeStruct((B, L, n_heads * HEAD_DIM), F32),
        compiler_params=_cparams(("parallel", "parallel", "arbitrary")),
    )(bias.astype(F32), q, k, v)


def _sb_sample_kernel(pt_ref, lay_ref, qbd_ref, bias_ref, kn_ref, vn_ref, *rest,
                      pp, n_heads, n_q, scale):
    k_refs, v_refs = rest[:pp], rest[pp:2 * pp]
    o_ref, acc_ref, c_ref = rest[2 * pp:]
    j = pl.program_id(1)
    qbd = qbd_ref[0]
    bias = bias_ref[...]
    HQ = qbd.shape[0]
    HD = qbd.shape[1]
    rr = lax.broadcasted_iota(jnp.int32, (PAGE_SIZE, 2 * PAGE_SIZE), 0)
    cc = lax.broadcasted_iota(jnp.int32, (PAGE_SIZE, 2 * PAGE_SIZE), 1)
    mw = jnp.where((cc >= PAGE_SIZE) | (rr > cc), 1.0, 0.0).astype(BF16)

    def attend(kts, vts, mask):
        n = len(kts)
        kt = kts[0] if n == 1 else jnp.concatenate(kts, axis=1)
        z = jnp.dot(qbd, kt, preferred_element_type=F32) * scale
        z = z + (bias if n == 1 else jnp.concatenate([bias] * n, axis=1))
        sp = _softplus(z)
        spm = sp if mask is None else jnp.where(mask, sp, 0.0)
        hi, lo = _split2(spm)
        rows = lambda x: jnp.concatenate([x[:, i * PAGE_SIZE:(i + 1) * PAGE_SIZE] for i in range(n)],
                                         axis=0) if n > 1 else x
        r = (jnp.dot(rows(hi), mw, preferred_element_type=F32)
             + jnp.dot(rows(lo), mw, preferred_element_type=F32))
        t = z - sp
        c = c_ref[...]
        parts = []
        for i in range(n):
            ri = r[i * HQ:(i + 1) * HQ]
            parts.append(t[:, i * PAGE_SIZE:(i + 1) * PAGE_SIZE] - ri[:, :PAGE_SIZE] - c)
            c = c + ri[:, PAGE_SIZE:]
        c_ref[...] = c
        a = jnp.exp(parts[0] if n == 1 else jnp.concatenate(parts, axis=1))
        if mask is not None:
            a = jnp.where(mask, a, 0.0)
        vt = vts[0] if n == 1 else jnp.concatenate(vts, axis=1)
        return lax.dot_general(vt, a.astype(BF16), (((1,), (1,)), ((), ())),
                               preferred_element_type=F32)

    @pl.when(j == 0)
    def _():
        c_ref[...] = jnp.zeros(c_ref.shape, F32)
        row_q = lax.broadcasted_iota(jnp.int32, (HQ, PAGE_SIZE), 0) % n_q
        key = lax.broadcasted_iota(jnp.int32, (HQ, PAGE_SIZE), 1)
        valid = (key < row_q) & (key < n_q)
        acc_ref[...] = attend([kn_ref[0].astype(BF16)], [vn_ref[0].astype(BF16)], valid)

    acc_ref[...] += attend([kr[...].reshape(HD, PAGE_SIZE).astype(BF16) for kr in k_refs],
                           [vr[...].reshape(HD, PAGE_SIZE).astype(BF16) for vr in v_refs], None)

    @pl.when(j == pl.num_programs(1) - 1)
    def _():
        o_ref[0] = acc_ref[...]


def sb_sample(q, k_new, v_new, bias, cache_k, cache_v, page_table, layer, *, n_heads, pp=16):
    Bs, n_q, HD = q.shape
    n_pages = page_table.shape[1]
    pp = min(pp, n_pages)
    HQ = n_heads * n_q
    eye = jnp.eye(n_heads, dtype=F32)
    qbd = jnp.einsum('bqhd,hg->bhqgd', q.reshape(Bs, n_q, n_heads, HEAD_DIM), eye)
    qbd = qbd.reshape(Bs, HQ, HD).astype(BF16)
    bias_rows = jnp.broadcast_to(jnp.repeat(bias.astype(F32), n_q)[:, None], (HQ, PAGE_SIZE))
    new_t = lambda a: jnp.pad(a.transpose(0, 2, 1), ((0, 0), (0, 0), (0, PAGE_SIZE - n_q)))
    pool_t = lambda c: c.transpose(0, 1, 3, 4, 2)

    def page_spec(i):
        return pl.BlockSpec(
            (None, None, n_heads, HEAD_DIM, PAGE_SIZE),
            lambda b, j, pt, lay: (lay[0], pt[b, n_pages - 1 - (j * pp + i)], 0, 0, 0))

    new_spec = pl.BlockSpec((1, HD, PAGE_SIZE), lambda b, j, pt, lay: (b, 0, 0))
    grid_spec = pltpu.PrefetchScalarGridSpec(
        num_scalar_prefetch=2,
        grid=(Bs, n_pages // pp),
        in_specs=[pl.BlockSpec((1, HQ, HD), lambda b, j, pt, lay: (b, 0, 0)),
                  pl.BlockSpec((HQ, PAGE_SIZE), lambda b, j, pt, lay: (0, 0)),
                  new_spec, new_spec]
                 + [page_spec(i) for i in range(pp)] + [page_spec(i) for i in range(pp)],
        out_specs=pl.BlockSpec((1, HD, HQ), lambda b, j, pt, lay: (b, 0, 0)),
        scratch_shapes=[pltpu.VMEM((HD, HQ), F32), pltpu.VMEM((HQ, PAGE_SIZE), F32)],
    )
    acc = pl.pallas_call(
        functools.partial(_sb_sample_kernel, pp=pp, n_heads=n_heads, n_q=n_q, scale=HEAD_DIM ** -0.5),
        grid_spec=grid_spec,
        out_shape=jax.ShapeDtypeStruct((Bs, HD, HQ), F32),
        compiler_params=_cparams(("parallel", "arbitrary")),
    )(page_table, jnp.full((1,), layer, jnp.int32), qbd, bias_rows, new_t(k_new), new_t(v_new),
      *([pool_t(cache_k)] * pp), *([pool_t(cache_v)] * pp))
    out = jnp.einsum('bhdgq,hg->bqhd', acc.reshape(Bs, n_heads, HEAD_DIM, n_heads, n_q), eye)
    return out.reshape(Bs, n_q, HD)


def _cumsum_rows(x, tril_bf16):
    C = x.shape[0]
    if C < 16:
        rows = [x[0:1]]
        for i in range(1, C):
            rows.append(rows[-1] + x[i:i + 1])
        return jnp.concatenate(rows, axis=0)
    h1 = x.astype(BF16)
    r1 = x - h1.astype(F32)
    h2 = r1.astype(BF16)
    h3 = (r1 - h2.astype(F32)).astype(BF16)
    dot = lambda h: jnp.dot(tril_bf16, h, preferred_element_type=F32)
    return dot(h1) + dot(h2) + dot(h3)


def _hgrn_kernel(z_ref, lb_ref, ng_ref, s0_ref, o_ref, sT_ref, st_ref, *, C, SB, n_chunks):
    ci = pl.program_id(1)
    W = lb_ref.shape[1]
    n_heads = W // HEAD_DIM
    n_sub = C // SB

    @pl.when(ci == 0)
    def _():
        st_ref[...] = s0_ref[...]

    ones_blk = _head_block_ones(W)
    blockmask = (lax.broadcasted_iota(jnp.int32, (W, W), 0) // HEAD_DIM
                 == lax.broadcasted_iota(jnp.int32, (W, W), 1) // HEAD_DIM)
    lane_head = lax.broadcasted_iota(jnp.int32, (SB, W), 1) // HEAD_DIM
    t_sub = lax.broadcasted_iota(jnp.int32, (SB, W), 0)
    tril = jnp.where(lax.broadcasted_iota(jnp.int32, (C, C), 0)
                     >= lax.broadcasted_iota(jnp.int32, (C, C), 1), 1.0, 0.0).astype(BF16)
    s_col = lax.broadcasted_iota(jnp.int32, (n_heads * SB, C), 1)
    lb = lb_ref[...]
    ng = ng_ref[...]

    def chunk(cc, carry):
        for bb in range(z_ref.shape[0]):
            chunk_of(bb, cc)
        return carry

    def chunk_of(bb, cc):
        r0 = pl.multiple_of(cc * C, C)
        zc = z_ref[bb, pl.ds(r0, C), :]
        q, fz, v, gate = zc[:, :W], zc[:, W:2 * W], zc[:, 2 * W:3 * W], zc[:, 3 * W:]
        f = lb + (1.0 - lb) * _sigmoid(fz)
        k = 1.0 - f
        b = _cumsum_rows(jnp.log(f), tril)
        st = st_ref[bb]
        o_inter = _bdot_t(q * jnp.exp(b), st)
        outs = []
        for I in range(n_sub):
            lo_, hi_ = I * SB, (I + 1) * SB
            bI, qI, kI, vI = b[lo_:hi_], q[lo_:hi_], k[lo_:hi_], v[lo_:hi_]
            ds = []
            for s in range(SB):
                e = jnp.exp(jnp.minimum(bI - bI[s:s + 1], 0.0))
                ds.append(jnp.where(t_sub >= s, e * qI * kI[s:s + 1], 0.0))
            G = jnp.dot(jnp.concatenate(ds, axis=0).astype(BF16), ones_blk,
                        preferred_element_type=F32)
            od = G[0:SB] * vI[0:1]
            for s in range(1, SB):
                od = od + G[s * SB:(s + 1) * SB] * vI[s:s + 1]
            if I > 0:
                rho = b[lo_ - 1:lo_]
                qs = qI * jnp.exp(bI - rho)
                kt = k * jnp.exp(jnp.minimum(rho - b, 0.0))
                qst = jnp.concatenate([jnp.where(lane_head == h, qs, 0.0) for h in range(n_heads)],
                                      axis=0)
                att = jnp.where(s_col < lo_, _bdot_t(qst, kt), 0.0)
                R = _bdot(att, v)
                for h in range(n_heads):
                    od = od + jnp.where(lane_head == h, R[h * SB:(h + 1) * SB], 0.0)
            outs.append(od)
        o = o_inter + (outs[0] if n_sub == 1 else jnp.concatenate(outs, axis=0))
        blast = b[C - 1:C]
        kd = k * jnp.exp(blast - b)
        upd = lax.dot_general(v.astype(BF16), kd.astype(BF16), (((0,), (0,)), ((), ())),
                              preferred_element_type=F32)
        st_ref[bb] = st * jnp.exp(blast) + jnp.where(blockmask, upd, 0.0)
        ms = _dot2(o * o, ones_blk) * (1.0 / HEAD_DIM)
        o_ref[bb, pl.ds(r0, C), :] = o * lax.rsqrt(ms + RMS_EPS) * ng * _silu(gate)

    lax.fori_loop(0, n_chunks, chunk, 0)

    @pl.when(ci == pl.num_programs(1) - 1)
    def _():
        sT_ref[...] = st_ref[...]


def hgrn2(zb, lb, norm_g, s0, *, rows_per_step=256):
    B, L, W4 = zb.shape
    W = W4 // 4
    H = W // HEAD_DIM
    C = math.gcd(L, 64)
    SB = min(16, C)
    rows = min(rows_per_step, L)
    eye = jnp.eye(H, dtype=F32)
    st0 = jnp.einsum('bhdv,hg->bhvgd', s0, eye).reshape(B, W, W)
    nb = 4 if B % 4 == 0 else 1
    o, sT = pl.pallas_call(
        functools.partial(_hgrn_kernel, C=C, SB=SB, n_chunks=rows // C),
        grid=(B // nb, L // rows),
        in_specs=[pl.BlockSpec((nb, rows, W4), lambda b, i: (b, i, 0)),
                  _const_spec((1, W)), _const_spec((1, W)),
                  pl.BlockSpec((nb, W, W), lambda b, i: (b, 0, 0))],
        out_specs=[pl.BlockSpec((nb, rows, W), lambda b, i: (b, i, 0)),
                   pl.BlockSpec((nb, W, W), lambda b, i: (b, 0, 0))],
        out_shape=[jax.ShapeDtypeStruct((B, L, W), F32), jax.ShapeDtypeStruct((B, W, W), F32)],
        scratch_shapes=[pltpu.VMEM((nb, W, W), F32)],
        compiler_params=_cparams(("parallel", "arbitrary")),
    )(zb, lb.reshape(1, W), norm_g.reshape(1, W), st0)
    s5 = sT.reshape(B, H, HEAD_DIM, H, HEAD_DIM)
    s_new = jnp.einsum('bhvgd,hg->bhdv', s5, eye)
    return o, s_new


def _rwkv_pre_kernel(z_ref, sh0_ref, mu_ref, w0_ref, wup_ref, a0_ref, aup_ref, gup_ref,
                     kkw_ref, kaw_ref, rk_ref, *refs, scan_layout, steps):
    if scan_layout:
        src_o, v_o, bonus_o, g_o, shift_o, prev_ref = refs
    else:
        *ops_o, v_o, bonus_o, g_o, shift_o, prev_ref = refs
    ti = pl.program_id(1)
    z = z_ref[0]
    tm, P = z.shape
    W = w0_ref.shape[1]

    @pl.when(ti == 0)
    def _():
        prev_ref[...] = sh0_ref[0]

    row = lax.broadcasted_iota(jnp.int32, (tm, P), 0)
    prev = jnp.where(row == 0, prev_ref[...], pltpu.roll(z, 1, axis=0))
    last = z[tm - 1:tm]
    prev_ref[...] = last
    shift_o[0] = last
    zs = z + (prev - z) * mu_ref[...]
    r, k, v, x4 = zs[:, :W], zs[:, W:2 * W], zs[:, 2 * W:3 * W], zs[:, 3 * W:]
    ones_blk = _head_block_ones(W)
    u = w0_ref[...] + _bdot(jnp.tanh(x4), wup_ref[...])
    log_w = -jnp.exp(-_softplus(-u) - 0.5)
    a = _sigmoid(a0_ref[...] + _bdot(x4, aup_ref[...]))
    kkr = k * kkw_ref[...]
    kk = kkr / jnp.maximum(jnp.sqrt(_dot2(kkr * kkr, ones_blk)), 1e-12)
    k2 = k * (1.0 + (a - 1.0) * kaw_ref[...])
    blk_r = lax.broadcasted_iota(jnp.int32, (tm, tm), 0)
    blk_c = lax.broadcasted_iota(jnp.int32, (tm, tm), 1)
    tril = jnp.where((blk_r >= blk_c) & (blk_r // steps == blk_c // steps), 1.0, 0.0).astype(BF16)
    log_g = _cumsum_rows(log_w, tril)
    gam = jnp.exp(log_g)
    inv_gam = jnp.exp(-log_g)
    scan_ops = (kk * jnp.exp(log_g - log_w), kk * a * inv_gam, k2 * inv_gam, r * gam, gam)
    if scan_layout:
        low = lax.broadcasted_iota(jnp.int32, (HEAD_DIM, LANES), 1) < HEAD_DIM
        for qi, xq in enumerate(scan_ops):
            for hp in range(W // LANES):
                for c2 in range(tm // LANES):
                    t_ = xq[c2 * LANES:(c2 + 1) * LANES, hp * LANES:(hp + 1) * LANES].T
                    top, bot = t_[:HEAD_DIM], t_[HEAD_DIM:]
                    src_o[hp, 2 * c2, qi] = jnp.where(low, top, pltpu.roll(bot, HEAD_DIM, axis=1))
                    src_o[hp, 2 * c2 + 1, qi] = jnp.where(low, pltpu.roll(top, HEAD_DIM, axis=1), bot)
    else:
        for o_ref, xq in zip(ops_o, scan_ops):
            o_ref[0] = xq
    for hp in range(W // LANES):
        v_o[0, hp] = v[:, hp * LANES:(hp + 1) * LANES]
    bonus_o[0] = _dot2(r * k2 * rk_ref[...], ones_blk) * v
    g_o[0] = _bdot(_sigmoid(x4), gup_ref[...])


def _rwkv_scan_kernel(src_ref, v_ref, s0_ref, o_ref, sT_ref, st_ref, lhs_ref, *, P, steps):
    c = pl.program_id(1)

    @pl.when(c == 0)
    def _():
        st_ref[...] = s0_ref[...]

    n_op = src_ref.shape[2]
    for p in range(P):
        x = src_ref[p, 0].reshape(n_op * HEAD_DIM, LANES)
        hi, lo = _split2(x)
        lhs_ref[p] = jnp.concatenate([hi, lo], axis=1)
    n_step_rows = (n_op - 1) * HEAD_DIM

    rr = lax.broadcasted_iota(jnp.int32, (2 * LANES, 2 * LANES), 0)
    cc = lax.broadcasted_iota(jnp.int32, (2 * LANES, 2 * LANES), 1)
    t_of_row = jnp.where((rr // HEAD_DIM) % 2 == (cc // HEAD_DIM) % 2,
                         rr % HEAD_DIM - cc // LANES, -1)

    def step2(i, carry):
        t0 = 2 * i
        sel = jnp.where(t_of_row == t0, 1.0, 0.0).astype(BF16)
        for p in range(P):
            cb = jnp.dot(lhs_ref[p, :n_step_rows], sel, preferred_element_type=F32)
            s = st_ref[p]
            for u in range(2):
                kk, ka, k2, r = (cb[i_ * HEAD_DIM:(i_ + 1) * HEAD_DIM, u * LANES:(u + 1) * LANES]
                                 for i_ in range(4))
                skk = jnp.sum(s * kk, axis=0, keepdims=True)
                s = s - ka * skk + k2 * v_ref[p, pl.ds(t0 + u, 1), :]
                o_ref[p, pl.ds(t0 + u, 1), :] = jnp.sum(s * r, axis=0, keepdims=True)
            st_ref[p] = s
        return carry

    lax.fori_loop(0, steps // 2, step2, 0)

    sel_last = jnp.where(t_of_row == steps - 1, 1.0, 0.0).astype(BF16)
    for p in range(P):
        gam = jnp.dot(lhs_ref[p, n_step_rows:], sel_last, preferred_element_type=F32)
        st_ref[p] = st_ref[p] * gam[:, :LANES]

    @pl.when(c == pl.num_programs(1) - 1)
    def _():
        sT_ref[...] = st_ref[...]


def _rwkv_post_kernel(o_ref, bonus_ref, g_ref, gng_ref, gnb_ref, out_ref):
    o = jnp.concatenate([o_ref[0, hp] for hp in range(o_ref.shape[1])], axis=1)
    ones_blk = _head_block_ones(o.shape[1])
    inv = 1.0 / HEAD_DIM
    d = o - _dot2(o, ones_blk) * inv
    var = _dot2(d * d, ones_blk) * inv
    out_ref[0] = (d * lax.rsqrt(var + GN_EPS) * gng_ref[...] + gnb_ref[...] + bonus_ref[0]) * g_ref[0]


def rwkv7(za, shift0, s0, p, *, tm=512, pairs_per_step=16):
    B, L, P = za.shape
    W = p['rwkv_w0'].shape[0]
    H = W // HEAD_DIM
    HP = W // LANES
    tm = min(tm, L)
    n_low = P - 3 * W
    rank_w, rank_a = p['rwkv_w_up'].shape[0], p['rwkv_a_up'].shape[0]
    pad_rows = lambda m, r0: jnp.zeros((n_low, W), F32).at[r0:r0 + m.shape[0]].set(m).astype(BF16)
    wup = pad_rows(p['rwkv_w_up'], 0)
    aup = pad_rows(p['rwkv_a_up'], rank_w)
    gup = pad_rows(p['rwkv_g_up'], rank_w + rank_a)
    vec = lambda a: a.reshape(1, -1)
    tok = pl.BlockSpec((1, tm, W), lambda b, i: (b, i, 0))
    tok_sd = jax.ShapeDtypeStruct((B, L, W), F32)
    pair_spec = pl.BlockSpec((1, HP, tm, LANES), lambda b, i: (b, 0, i, 0))
    pair_sd = jax.ShapeDtypeStruct((B, HP, L, LANES), F32)
    steps = min(HEAD_DIM, L)
    assert steps % 2 == 0
    NC = L // steps
    NP = B * HP
    scan_layout = tm % LANES == 0
    if scan_layout:
        ops_specs = [pl.BlockSpec((HP, tm // HEAD_DIM, 5, HEAD_DIM, LANES),
                                  lambda b, i: (b, i, 0, 0, 0))]
        ops_sds = [jax.ShapeDtypeStruct((NP, NC, 5, HEAD_DIM, LANES), F32)]
    else:
        ops_specs, ops_sds = [tok] * 5, [tok_sd] * 5
    *ops, v, bonus, g, shift = pl.pallas_call(
        functools.partial(_rwkv_pre_kernel, scan_layout=scan_layout, steps=steps),
        grid=(B, L // tm),
        in_specs=[pl.BlockSpec((1, tm, P), lambda b, i: (b, i, 0)),
                  pl.BlockSpec((1, 1, P), lambda b, i: (b, 0, 0)),
                  _const_spec((1, P)), _const_spec((1, W)), _const_spec((n_low, W)),
                  _const_spec((1, W)), _const_spec((n_low, W)), _const_spec((n_low, W)),
                  _const_spec((1, W)), _const_spec((1, W)), _const_spec((1, W))],
        out_specs=ops_specs + [pair_spec, tok, tok, pl.BlockSpec((1, 1, P), lambda b, i: (b, 0, 0))],
        out_shape=ops_sds + [pair_sd, tok_sd, tok_sd, jax.ShapeDtypeStruct((B, 1, P), F32)],
        scratch_shapes=[pltpu.VMEM((1, P), F32)],
        compiler_params=_cparams(("parallel", "arbitrary")),
    )(za, shift0.reshape(B, 1, P), vec(p['rwkv_mu']), vec(p['rwkv_w0']), wup, vec(p['rwkv_a0']),
      aup, gup, vec(p['rwkv_k_k']), vec(p['rwkv_k_a']), vec(p['rwkv_r_k']))

    if scan_layout:
        src, = ops
    else:
        src = jnp.stack(ops)
        src = src.reshape(5, B, NC, steps, HP, 2, HEAD_DIM).transpose(1, 4, 2, 0, 6, 5, 3)
        src = jnp.pad(src, ((0, 0),) * 6 + ((0, HEAD_DIM - steps),))
        src = src.reshape(NP, NC, 5, HEAD_DIM, LANES)
    st0 = s0.reshape(B, HP, 2, HEAD_DIM, HEAD_DIM).transpose(0, 1, 4, 2, 3).reshape(NP, HEAD_DIM, LANES)
    PP = min(pairs_per_step, NP)
    o, sT = pl.pallas_call(
        functools.partial(_rwkv_scan_kernel, P=PP, steps=steps),
        grid=(NP // PP, NC),
        in_specs=[pl.BlockSpec((PP, 1, 5, HEAD_DIM, LANES), lambda g_, c: (g_, c, 0, 0, 0)),
                  pl.BlockSpec((PP, steps, LANES), lambda g_, c: (g_, c, 0)),
                  pl.BlockSpec((PP, HEAD_DIM, LANES), lambda g_, c: (g_, 0, 0))],
        out_specs=[pl.BlockSpec((PP, steps, LANES), lambda g_, c: (g_, c, 0)),
                   pl.BlockSpec((PP, HEAD_DIM, LANES), lambda g_, c: (g_, 0, 0))],
        out_shape=[jax.ShapeDtypeStruct((NP, L, LANES), F32),
                   jax.ShapeDtypeStruct((NP, HEAD_DIM, LANES), F32)],
        scratch_shapes=[pltpu.VMEM((PP, HEAD_DIM, LANES), F32),
                        pltpu.VMEM((PP, 5 * HEAD_DIM, 2 * LANES), BF16)],
        compiler_params=_cparams(("parallel", "arbitrary")),
    )(src, v.reshape(NP, L, LANES), st0)
    s_new = sT.reshape(B, HP, HEAD_DIM, 2, HEAD_DIM).transpose(0, 1, 3, 4, 2).reshape(B, H, HEAD_DIM, HEAD_DIM)

    out = pl.pallas_call(
        _rwkv_post_kernel,
        grid=(B, L // tm),
        in_specs=[pair_spec, tok, tok, _const_spec((1, W)), _const_spec((1, W))],
        out_specs=tok,
        out_shape=tok_sd,
        compiler_params=_cparams(("parallel", "parallel")),
    )(o.reshape(B, HP, L, LANES), bonus, g, vec(p['rwkv_gn_g']), vec(p['rwkv_gn_b']))
    return out, s_new, shift.reshape(B, P)


def _route_kernel(x_ref, rt_ref, g_ref, s_ref):
    logits = lax.dot_general(rt_ref[...], x_ref[...], (((1,), (1,)), ((), ())),
                             precision=lax.Precision.HIGHEST, preferred_element_type=F32)
    n_e = logits.shape[0]
    e_id = lax.broadcasted_iota(jnp.int32, logits.shape, 0)
    m1 = jnp.max(logits, axis=0, keepdims=True)
    i1 = jnp.min(jnp.where(logits == m1, e_id, n_e), axis=0, keepdims=True)
    rest = jnp.where(e_id == i1, -jnp.inf, logits)
    m2 = jnp.max(rest, axis=0, keepdims=True)
    i2 = jnp.min(jnp.where(rest == m2, e_id, n_e), axis=0, keepdims=True)
    t = jnp.exp(m2 - m1)
    g1 = 1.0 / (1.0 + t)
    g2 = t / (1.0 + t)
    g_ref[...] = jnp.where(e_id == i1, g1, jnp.where(e_id == i2, g2, 0.0))
    s_ref[...] = jnp.where(e_id == i1, 1.0, jnp.where(e_id == i2, 1.0, 0.0))


def _moe_kernel(xb_ref, g_ref, s_ref, wg_ref, wu_ref, wd_ref, o_ref, rank_ref, xg_ref, y_ref, *, R):
    e = pl.program_id(1)
    f = pl.program_id(2)
    n_f = pl.num_programs(2)
    tm = xb_ref.shape[0]

    @pl.when((e == 0) & (f == 0))
    def _():
        before = jnp.where(lax.broadcasted_iota(jnp.int32, (tm, tm), 0)
                           < lax.broadcasted_iota(jnp.int32, (tm, tm), 1), 1.0, 0.0).astype(BF16)
        rank_ref[...] = jnp.dot(s_ref[...].astype(BF16), before, preferred_element_type=F32)
        o_ref[...] = jnp.zeros(o_ref.shape, F32)

    sel = s_ref[pl.ds(e, 1), :]
    key = jnp.where(sel > 0.0, rank_ref[pl.ds(e, 1), :], -1.0)
    n_chunks = (jnp.sum(sel).astype(jnp.int32) + (R - 1)) // R
    r_id = lax.broadcasted_iota(jnp.int32, (R, tm), 0)

    def one_hot(c):
        return jnp.where(key == (r_id + c * R).astype(F32), 1.0, 0.0).astype(BF16)

    def rows_of(c):
        return pl.ds(pl.multiple_of(c * R, math.gcd(R, 256)), R)

    @pl.when(f == 0)
    def _():
        def gather(c, carry):
            xg_ref[rows_of(c), :] = jnp.dot(one_hot(c), xb_ref[...],
                                            preferred_element_type=F32).astype(BF16)
            return carry
        lax.fori_loop(0, n_chunks, gather, 0)

    def expert(c, carry):
        xg = xg_ref[rows_of(c), :]
        gate = jnp.dot(xg, wg_ref[...], preferred_element_type=F32)
        up = jnp.dot(xg, wu_ref[...], preferred_element_type=F32)
        y = _bdot(_silu(gate) * up, wd_ref[...])

        @pl.when(f == 0)
        def _():
            y_ref[rows_of(c), :] = y

        @pl.when(f > 0)
        def _():
            y_ref[rows_of(c), :] += y
        return carry

    lax.fori_loop(0, n_chunks, expert, 0)

    @pl.when(f == n_f - 1)
    def _():
        gt = g_ref[pl.ds(e, 1), :]
        h1 = gt.astype(BF16)
        r1 = gt - h1.astype(F32)
        h2 = r1.astype(BF16)
        h3 = (r1 - h2.astype(F32)).astype(BF16)
        g3 = jnp.concatenate([h1, h2, h3, jnp.zeros((5, tm), BF16)], axis=0)

        def scatter(c, carry):
            p = one_hot(c)
            g_row = jnp.sum(lax.dot_general(p, g3, (((1,), (1,)), ((), ())),
                                            preferred_element_type=F32), axis=1, keepdims=True)
            yw = (y_ref[rows_of(c), :] * g_row).astype(BF16)
            o_ref[...] += lax.dot_general(p, yw, (((0,), (0,)), ((), ())),
                                          preferred_element_type=F32)
            return carry
        lax.fori_loop(0, n_chunks, scatter, 0)


def moe_ffn(x, router, wg_bf16, wu_bf16, wd_bf16, *, tm=1024, tf=896, R=288):
    T, D = x.shape
    n_e, _, F = wg_bf16.shape
    tm = min(tm, T)
    R = min(R, tm)
    gates, sel = pl.pallas_call(
        _route_kernel,
        grid=(T // tm,),
        in_specs=[pl.BlockSpec((tm, D), lambda i: (i, 0)), _const_spec((n_e, D))],
        out_specs=[pl.BlockSpec((n_e, tm), lambda i: (0, i))] * 2,
        out_shape=[jax.ShapeDtypeStruct((n_e, T), F32)] * 2,
        compiler_params=_cparams(("parallel",)),
    )(x, router.T)
    return pl.pallas_call(
        functools.partial(_moe_kernel, R=R),
        grid=(T // tm, n_e, F // tf),
        in_specs=[pl.BlockSpec((tm, D), lambda i, e, f: (i, 0)),
                  pl.BlockSpec((n_e, tm), lambda i, e, f: (0, i)),
                  pl.BlockSpec((n_e, tm), lambda i, e, f: (0, i)),
                  pl.BlockSpec((None, D, tf), lambda i, e, f: (e, 0, f)),
                  pl.BlockSpec((None, D, tf), lambda i, e, f: (e, 0, f)),
                  pl.BlockSpec((None, tf, D), lambda i, e, f: (e, f, 0))],
        out_specs=pl.BlockSpec((tm, D), lambda i, e, f: (i, 0)),
        out_shape=jax.ShapeDtypeStruct((T, D), F32),
        scratch_shapes=[pltpu.VMEM((n_e, tm), F32), pltpu.VMEM((pl.cdiv(tm, R) * R, D), BF16),
                        pltpu.VMEM((pl.cdiv(tm, R) * R, D), F32)],
        compiler_params=_cparams(("parallel", "arbitrary", "arbitrary")),
    )(x.astype(BF16), gates, sel, wg_bf16, wu_bf16, wd_bf16)


def _layer(x, seq_len, p, lb, states, sb_past, mem_k, mem_v, ffn, alpha, layer, depth, kv_stack):
    T, D = x.shape
    B = T // seq_len
    W = p['rwkv_w0'].shape[0]
    pa = p['rwkv_mu'].shape[0]
    wc = p['sb_bias'].shape[0] * HEAD_DIM
    x, za, zb, q, k, v, k4, v4 = in_proj(x, p['ln_in_g'], p['ln_in_b'], p['w_in'], kv_stack, layer,
                                         depth, apply_ln=layer == 0, widths=(pa, 4 * W, wc, wc, wc))
    to3 = lambda a: a.reshape(B, seq_len, a.shape[-1])
    o_a, rwkv_s, shift = rwkv7(to3(za), states[1], states[0], p)
    o_b, hgrn_s = hgrn2(to3(zb), lb, p['hgrn_norm_g'], states[2])
    n_heads = p['sb_bias'].shape[0]
    if sb_past is None:
        o_c = sb_prompt(to3(q), to3(k), to3(v), p['sb_bias'], n_heads=n_heads)
    else:
        o_c = sb_sample(to3(q), to3(k), to3(v), p['sb_bias'], *sb_past, n_heads=n_heads)
    x = mix_out(o_a.reshape(T, W), o_b.reshape(T, W), o_c.reshape(T, wc), p['w_out'], x,
                p['ln_mix_g'], p['ln_mix_b'], alpha=alpha)
    mem_layer = layer if mem_k.shape[0] == depth else 0
    x = mem_block(x, p['mem_wq'], mem_k, mem_v, p['mem_wo'], p['ln_mem_g'], p['ln_mem_b'],
                  alpha=alpha, seq_len=seq_len, layer=mem_layer)
    x = ffn(x)
    return x, (rwkv_s, shift, hgrn_s, (k4, v4))


def kernel(x_prompt, x_sample, cache_sb_k, cache_sb_v, state_rwkv, state_rwkv_shift, state_hgrn,
           cache_mem_k, cache_mem_v, page_table, mem_prompt, ln_in_g, ln_in_b, w_in, rwkv_mu, rwkv_w0,
           rwkv_w_up, rwkv_a0, rwkv_a_up, rwkv_g_up, rwkv_k_k, rwkv_k_a, rwkv_r_k, rwkv_gn_g, rwkv_gn_b,
           hgrn_lb, hgrn_norm_g, sb_bias, w_out, ln_mix_g, ln_mix_b, mem_wq, mem_wk, mem_wv, mem_wo,
           ln_mem_g, ln_mem_b, ffn_w_gate, ffn_w_up, ffn_w_down, moe_router, moe_w_gate, moe_w_up,
           moe_w_down, ln_ffn_g, ln_ffn_b):
    B, L, D = x_prompt.shape
    Bs, Ls, _ = x_sample.shape
    depth = w_in.shape[0]
    H = state_rwkv.shape[2]
    n_heads_c = sb_bias.shape[1]
    n_mem = mem_prompt.shape[1]
    alpha = (2 * depth) ** 0.25
    bf = lambda a: a.astype(BF16)

    lb_sm = jax.nn.softmax(hgrn_lb.astype(F32), axis=0)
    lb_all = jnp.cumsum(lb_sm, axis=0) - lb_sm[0]

    mem_flat = mem_prompt.reshape(B * n_mem, D)
    zeros_p = (jnp.zeros((B, H, HEAD_DIM, HEAD_DIM), F32), jnp.zeros((B, rwkv_mu.shape[1]), F32),
               jnp.zeros((B, H, HEAD_DIM, HEAD_DIM), F32))

    xp = x_prompt.reshape(B * L, D)
    xs = x_sample.reshape(Bs * Ls, D)
    outs_p = [[] for _ in range(5)]
    outs_s = [[] for _ in range(3)]
    stacks = lambda n_tok: tuple(jnp.zeros((depth * n_tok * n_heads_c, HEAD_DIM), F32)
                                 for _ in range(2))
    kv_p, kv_s = stacks(B * L), stacks(Bs * Ls)
    for l in range(depth):
        p = dict(ln_in_g=ln_in_g, ln_in_b=ln_in_b, w_in=bf(w_in[l]), w_out=bf(w_out[l]),
                 rwkv_mu=rwkv_mu[l], rwkv_w0=rwkv_w0[l], rwkv_w_up=rwkv_w_up[l], rwkv_a0=rwkv_a0[l],
                 rwkv_a_up=rwkv_a_up[l], rwkv_g_up=rwkv_g_up[l], rwkv_k_k=rwkv_k_k[l],
                 rwkv_k_a=rwkv_k_a[l], rwkv_r_k=rwkv_r_k[l], rwkv_gn_g=rwkv_gn_g[l],
                 rwkv_gn_b=rwkv_gn_b[l], hgrn_norm_g=hgrn_norm_g[l], sb_bias=sb_bias[l],
                 ln_mix_g=ln_mix_g[l], ln_mix_b=ln_mix_b[l], mem_wq=bf(mem_wq[l]),
                 mem_wo=bf(mem_wo[l]), ln_mem_g=ln_mem_g[l], ln_mem_b=ln_mem_b[l])
        j = l // 2
        if l % 2 == 0:
            wg, wu, wd = bf(ffn_w_gate[j]), bf(ffn_w_up[j]), bf(ffn_w_down[j])
            ffn = lambda x, wg=wg, wu=wu, wd=wd, l=l: ffn_block(
                x, wg, wu, wd, ln_ffn_g[l], ln_ffn_b[l], alpha=alpha)
        else:
            wg, wu, wd = bf(moe_w_gate[j]), bf(moe_w_up[j]), bf(moe_w_down[j])
            ffn = lambda x, wg=wg, wu=wu, wd=wd, j=j, l=l: res_ln(
                x, moe_ffn(x, moe_router[j], wg, wu, wd), ln_ffn_g[l], ln_ffn_b[l], alpha=alpha)
        mk, mv = matmul2(mem_flat, bf(mem_wk[l]), bf(mem_wv[l]))
        mk = mk.reshape(1, B, n_mem, D)
        mv = mv.reshape(1, B, n_mem, D)
        xp, (s_a, sh, s_b, kv_p) = _layer(xp, L, p, lb_all[l], zeros_p, None, mk, mv, ffn,
                                          alpha, l, depth, kv_p)
        for lst, val in zip(outs_p, (s_a, sh, s_b,
                                     mk.reshape(B, n_mem, N_MEM_HEADS, D // N_MEM_HEADS),
                                     mv.reshape(B, n_mem, N_MEM_HEADS, D // N_MEM_HEADS))):
            lst.append(val)
        states = (state_rwkv[l], state_rwkv_shift[l], state_hgrn[l])
        xs, (s_a, sh, s_b, kv_s) = _layer(
            xs, Ls, p, lb_all[l], states, (cache_sb_k, cache_sb_v, page_table, l),
            cache_mem_k.reshape(depth, Bs, n_mem, D), cache_mem_v.reshape(depth, Bs, n_mem, D), ffn,
            alpha, l, depth, kv_s)
        for lst, val in zip(outs_s, (s_a, sh, s_b)):
            lst.append(val)
    kv5 = lambda a, nb, sl: a.reshape(depth, nb, sl, n_heads_c, HEAD_DIM)
    return (xp.reshape(B, L, D), xs.reshape(Bs, Ls, D),
            kv5(kv_p[0], B, L), kv5(kv_p[1], B, L), *(jnp.stack(o) for o in outs_p),
            kv5(kv_s[0], Bs, Ls), kv5(kv_s[1], Bs, Ls), *(jnp.stack(o) for o in outs_s))
```
